```python
import math
import jax, jax.numpy as jnp
from jax import lax
import numpy as np


D_MODEL = 2048
BATCH = 32
SEQ = 256
DEPTH = 2
DEC_BATCH = 8
DEC_SEQ = 1024
PAST_LEN = 256

GRID_W = 64
HEAD_DIM = 128
A_HEADS = 4
A_KV_HEADS = 2
WINDOW = 128
BLK = 128
B_HEADS = 4
B_NOPE = 128
B_ROPE = 64
B_V = 128
B_RANK = 256
C_HEADS = 4
C_KV_HEADS = 2
D_HEADS = 4
D_HALF = 64
D_FF = 4 * D_MODEL
QBLK = 128
ROPE_BASE = 10000.0
EPS = 1e-6
MOD_CHUNKS = 6

IN_SIZES = (A_HEADS * HEAD_DIM, A_KV_HEADS * HEAD_DIM, A_KV_HEADS * HEAD_DIM,
            B_HEADS * B_NOPE, B_HEADS * B_ROPE, B_RANK, B_ROPE,
            C_HEADS * HEAD_DIM, C_KV_HEADS * HEAD_DIM, C_KV_HEADS * HEAD_DIM,
            D_HEADS * 2 * D_HALF, D_HEADS * 2 * D_HALF, D_HEADS * HEAD_DIM)
IN_WIDTH = sum(IN_SIZES)
IN_OFFSETS = tuple(sum(IN_SIZES[:i]) for i in range(1, len(IN_SIZES)))

kernel_name = 'hybrid_parallel_heads_diffusion_step'


def rmsnorm(x, g):
    xf = x.astype(jnp.float32)
    y = xf * lax.rsqrt(jnp.mean(xf * xf, axis=-1, keepdims=True) + EPS)
    return (y * g.astype(jnp.float32)).astype(x.dtype)


def _rope_1d(x, pos):
    half = x.shape[-1] // 2
    inv = ROPE_BASE ** (-jnp.arange(half, dtype=jnp.float32) / half)
    ang = pos.astype(jnp.float32)[:, None] * inv[None, :]
    cos = jnp.cos(ang).astype(x.dtype)
    sin = jnp.sin(ang).astype(x.dtype)
    x1, x2 = x[..., :half], x[..., half:]
    return jnp.concatenate([x1 * cos - x2 * sin, x1 * sin + x2 * cos], axis=-1)


def rope_2d(x):
    T = x.shape[-2]
    rows = T // GRID_W
    row = jnp.repeat(jnp.arange(rows), GRID_W)
    col = jnp.arange(rows * GRID_W) % GRID_W
    h = x.shape[-1] // 2
    return jnp.concatenate([_rope_1d(x[..., :h], row), _rope_1d(x[..., h:], col)], axis=-1)


def attend(q, k, v, scale, sink=None):
    B, Hq, T, dk = q.shape
    Hk = k.shape[1]
    G = Hq // Hk
    dv = v.shape[-1]
    nb = T // QBLK
    qb = q.reshape(B, Hk, G, nb, QBLK, dk).transpose(3, 0, 1, 2, 4, 5)

    def one_block(qi):
        s = jnp.einsum('bhgqd,bhsd->bhgqs', qi, k).astype(jnp.float32) * scale
        if sink is not None:
            sk = jnp.broadcast_to(sink.astype(jnp.float32).reshape(1, Hk, G, 1, 1), s.shape[:-1] + (1,))
            p = jax.nn.softmax(jnp.concatenate([s, sk], axis=-1), axis=-1)[..., :-1]
        else:
            p = jax.nn.softmax(s, axis=-1)
        return jnp.einsum('bhgqs,bhsd->bhgqd', p.astype(v.dtype), v)

    o = lax.map(one_block, qb)
    return o.transpose(1, 2, 3, 0, 4, 5).reshape(B, Hq, T, dv)


def banded_attend(q, k, v, k_ctx, v_ctx, sink, scale):
    B, Hq, T, dk = q.shape
    Hk = k.shape[1]
    G = Hq // Hk
    dv = v.shape[-1]
    nb = T // BLK
    S = k_ctx.shape[2]
    pad = ((0, 0), (0, 0), (BLK, BLK), (0, 0))
    kp = jnp.pad(k, pad).reshape(B, Hk, nb + 2, BLK, dk)
    vp = jnp.pad(v, pad).reshape(B, Hk, nb + 2, BLK, dv)
    kb = jnp.concatenate([kp[:, :, :-2], kp[:, :, 1:-1], kp[:, :, 2:]], axis=3)
    vb = jnp.concatenate([vp[:, :, :-2], vp[:, :, 1:-1], vp[:, :, 2:]], axis=3)
    qb = q.reshape(B, Hk, G, nb, BLK, dk)
    s_loc = jnp.einsum('bhgnqd,bhnkd->bhgnqk', qb, kb).astype(jnp.float32) * scale
    qpos = jnp.arange(nb)[:, None] * BLK + jnp.arange(BLK)[None, :]
    kpos = (jnp.arange(nb)[:, None] - 1) * BLK + jnp.arange(3 * BLK)[None, :]
    ok = (jnp.abs(qpos[:, :, None] - kpos[:, None, :]) <= WINDOW) & (kpos[:, None, :] >= 0) & (kpos[:, None, :] < T)
    s_loc = jnp.where(ok, s_loc, -jnp.inf)
    s_ctx = jnp.einsum('bhgnqd,bhsd->bhgnqs', qb, k_ctx).astype(jnp.float32) * scale
    sk = jnp.broadcast_to(sink.astype(jnp.float32).reshape(1, Hk, G, 1, 1, 1), s_loc.shape[:-1] + (1,))
    p = jax.nn.softmax(jnp.concatenate([s_ctx, s_loc, sk], axis=-1), axis=-1).astype(v.dtype)
    o = (jnp.einsum('bhgnqs,bhsd->bhgnqd', p[..., :S], v_ctx)
         + jnp.einsum('bhgnqk,bhnkd->bhgnqd', p[..., S:S + 3 * BLK], vb))
    return o.reshape(B, Hq, T, dv)


def mla_expand(c_lat, k_rope, w_uk, w_uv):
    k_nope = jnp.einsum('bsr,rhd->bhsd', c_lat, w_uk)
    k = jnp.concatenate([k_nope, jnp.broadcast_to(k_rope, k_nope.shape[:3] + (B_ROPE,))], axis=-1)
    v = jnp.einsum('bsr,rhd->bhsd', c_lat, w_uv)
    return k, v


def token_mixers(h, p, l, ctx):
    B, T, _ = h.shape
    latent = ctx is not None
    pos = rope_2d if latent else (lambda u: u)
    z = jnp.einsum('btd,de->bte', h, p['w_in'])
    (aq, ak, av, bqn, bqr, bckv, bkr, cq, ck, cv, dq, dk, dv) = jnp.split(z, IN_OFFSETS, axis=-1)

    def heads(u, n):
        return u.reshape(B, T, n, -1).transpose(0, 2, 1, 3)

    def merge(o):
        return o.transpose(0, 2, 1, 3).reshape(B, T, -1)

    qa = pos(heads(aq, A_HEADS))
    ka = pos(heads(ak, A_KV_HEADS))
    va = heads(av, A_KV_HEADS)
    if latent:
        o_a = banded_attend(qa, ka, va, ctx['a_k'], ctx['a_v'], p['a_sink'], HEAD_DIM ** -0.5)
    else:
        o_a = attend(qa, ka, va, HEAD_DIM ** -0.5, p['a_sink'])

    c_lat = rmsnorm(bckv, p['b_g_kv'])
    qb_ = jnp.concatenate([heads(bqn, B_HEADS), pos(heads(bqr, B_HEADS))], axis=-1)
    kb_, vb_ = mla_expand(c_lat, pos(bkr[:, None]), p['b_w_uk'], p['b_w_uv'])
    if latent:
        kb_c, vb_c = mla_expand(ctx['mla'][..., :B_RANK], ctx['mla'][:, None, :, B_RANK:], p['b_w_uk'], p['b_w_uv'])
        kb_ = jnp.concatenate([kb_c, kb_], axis=2)
        vb_ = jnp.concatenate([vb_c, vb_], axis=2)
    o_b = attend(qb_, kb_, vb_, (B_NOPE + B_ROPE) ** -0.5)

    qc = pos(rmsnorm(heads(cq, C_HEADS), p['c_gq']))
    kc = rmsnorm(heads(ck, C_KV_HEADS), p['c_gk'])
    vc = heads(cv, C_KV_HEADS)
    if latent:
        kc_s = jnp.concatenate([ctx['c_k'], pos(kc)], axis=2)
        vc_s = jnp.concatenate([ctx['c_v'], vc], axis=2)
    else:
        kc_s, vc_s = kc, vc
    o_c = attend(qc, kc_s, vc_s, HEAD_DIM ** -0.5)

    def pos2(u):
        return jnp.concatenate([pos(u[..., :D_HALF]), pos(u[..., D_HALF:])], axis=-1)
    qd = pos2(heads(dq, D_HEADS))
    kd = heads(dk, D_HEADS)
    vd = heads(dv, D_HEADS)
    if latent:
        kd_s = jnp.concatenate([ctx['d_k'], pos2(kd)], axis=2)
        vd_s = jnp.concatenate([ctx['d_v'], vd], axis=2)
    else:
        kd_s, vd_s = kd, vd
    lam_init = 0.8 - 0.6 * math.exp(-0.3 * l)
    lamp = p['d_lam'].astype(jnp.float32)
    lam = jnp.exp(jnp.sum(lamp[0] * lamp[1])) - jnp.exp(jnp.sum(lamp[2] * lamp[3])) + lam_init
    o1 = attend(qd[..., :D_HALF], kd_s[..., :D_HALF], vd_s, D_HALF ** -0.5)
    o2 = attend(qd[..., D_HALF:], kd_s[..., D_HALF:], vd_s, D_HALF ** -0.5)
    o_d = rmsnorm(o1 - lam.astype(o1.dtype) * o2, p['d_g_out']) * (1.0 - lam_init)

    o = jnp.concatenate([merge(o_a), merge(o_b), merge(o_c), merge(o_d)], axis=-1)
    out = jnp.einsum('bte,ed->btd', o, p['w_out'])
    if latent:
        return out, None
    return out, (ka, va, jnp.concatenate([c_lat, bkr], axis=-1), kc, vc, kd, vd)


def block(x, mod, p, l, ctx):
    sh1, sc1, g1, sh2, sc2, g2 = jnp.split(mod, MOD_CHUNKS, axis=-1)
    h = rmsnorm(x, p['g_attn_pre']) * (1 + sc1) + sh1
    o, ctx_out = token_mixers(h, p, l, ctx)
    x = x + g1 * rmsnorm(o, p['g_attn_post'])
    h = rmsnorm(x, p['g_mlp_pre']) * (1 + sc2) + sh2
    f = jnp.square(jax.nn.relu(h @ p['w_up'])) @ p['w_down']
    x = x + g2 * rmsnorm(f, p['g_mlp_post'])
    return x, ctx_out


def setup_inputs(seed: int = 0) -> dict:
    key = jax.random.key(seed)
    ks = jax.random.split(key, 29)
    D = D_MODEL

    def nrm(i, shape, scale=1.0):
        return jax.random.normal(ks[i], shape, jnp.float32) * scale

    def gain(i, shape):
        return 1.0 + 0.01 * jax.random.normal(ks[i], shape, jnp.float32)

    return {
        'x_prompt': nrm(0, (BATCH, SEQ, D)),
        'x_sample': nrm(1, (DEC_BATCH, DEC_SEQ, D)),
        'cache_a_k': nrm(2, (DEC_BATCH, DEPTH, A_KV_HEADS, PAST_LEN, HEAD_DIM)),
        'cache_a_v': nrm(3, (DEC_BATCH, DEPTH, A_KV_HEADS, PAST_LEN, HEAD_DIM)),
        'cache_mla': nrm(4, (DEC_BATCH, DEPTH, PAST_LEN, B_RANK + B_ROPE)),
        'cache_c_k': nrm(5, (DEC_BATCH, DEPTH, C_KV_HEADS, PAST_LEN, HEAD_DIM)),
        'cache_c_v': nrm(6, (DEC_BATCH, DEPTH, C_KV_HEADS, PAST_LEN, HEAD_DIM)),
        'cache_d_k': nrm(7, (DEC_BATCH, DEPTH, D_HEADS, PAST_LEN, 2 * D_HALF)),
        'cache_d_v': nrm(8, (DEC_BATCH, DEPTH, D_HEADS, PAST_LEN, HEAD_DIM)),
        'c': nrm(9, (DEC_BATCH, D)),
        'c_ctx': nrm(10, (D,)),
        'w_ada': nrm(11, (DEPTH, D, MOD_CHUNKS * D), 0.5 * D ** -0.5),
        'b_ada': nrm(12, (DEPTH, MOD_CHUNKS * D), 0.01),
        'g_attn_pre': gain(13, (DEPTH, D)),
        'g_attn_post': gain(14, (DEPTH, D)),
        'g_mlp_pre': gain(15, (DEPTH, D)),
        'g_mlp_post': gain(16, (DEPTH, D)),
        'w_in': nrm(17, (DEPTH, D, IN_WIDTH), D ** -0.5),
        'a_sink': nrm(18, (DEPTH, A_HEADS), 0.5),
        'b_g_kv': gain(19, (DEPTH, B_RANK)),
        'b_w_uk': nrm(20, (DEPTH, B_RANK, B_HEADS, B_NOPE), B_RANK ** -0.5),
        'b_w_uv': nrm(21, (DEPTH, B_RANK, B_HEADS, B_V), B_RANK ** -0.5),
        'c_gq': gain(22, (DEPTH, HEAD_DIM)),
        'c_gk': gain(23, (DEPTH, HEAD_DIM)),
        'd_lam': nrm(24, (DEPTH, 4, D_HALF), 0.1),
        'd_g_out': gain(25, (DEPTH, HEAD_DIM)),
        'w_out': nrm(26, (DEPTH, D, D), D ** -0.5),
        'w_up': nrm(27, (DEPTH, D, D_FF), D ** -0.5),
        'w_down': nrm(28, (DEPTH, D_FF, D), D_FF ** -0.5),
    }


def reference(x_prompt, x_sample, cache_a_k, cache_a_v, cache_mla, cache_c_k, cache_c_v, cache_d_k, cache_d_v,
              c, c_ctx, w_ada, b_ada, g_attn_pre, g_attn_post, g_mlp_pre, g_mlp_post, w_in, a_sink,
              b_g_kv, b_w_uk, b_w_uv, c_gq, c_gk, d_lam, d_g_out, w_out, w_up, w_down):
    def layer_params(l):
        return {'g_attn_pre': g_attn_pre[l], 'g_attn_post': g_attn_post[l],
                'g_mlp_pre': g_mlp_pre[l], 'g_mlp_post': g_mlp_post[l],
                'w_in': w_in[l], 'a_sink': a_sink[l], 'b_g_kv': b_g_kv[l],
                'b_w_uk': b_w_uk[l], 'b_w_uv': b_w_uv[l], 'c_gq': c_gq[l], 'c_gk': c_gk[l],
                'd_lam': d_lam[l], 'd_g_out': d_g_out[l], 'w_out': w_out[l],
                'w_up': w_up[l], 'w_down': w_down[l]}

    x = x_prompt
    states = [[] for _ in range(7)]
    for l in range(DEPTH):
        mod = (jax.nn.silu(c_ctx) @ w_ada[l] + b_ada[l])[None, None, :]
        x, st = block(x, mod, layer_params(l), l, None)
        for lst, t in zip(states, st):
            lst.append(t)
    y_prompt = x

    x = x_sample
    for l in range(DEPTH):
        mod = (jax.nn.silu(c) @ w_ada[l] + b_ada[l])[:, None, :]
        ctx = {'a_k': cache_a_k[:, l], 'a_v': cache_a_v[:, l], 'mla': cache_mla[:, l],
               'c_k': cache_c_k[:, l], 'c_v': cache_c_v[:, l],
               'd_k': cache_d_k[:, l], 'd_v': cache_d_v[:, l]}
        x, _ = block(x, mod, layer_params(l), l, ctx)
    y_sample = x

    return (y_prompt, y_sample,
            jnp.stack(states[0], axis=1), jnp.stack(states[1], axis=1), jnp.stack(states[2], axis=1),
            jnp.stack(states[3], axis=1), jnp.stack(states[4], axis=1),
            jnp.stack(states[5], axis=1), jnp.stack(states[6], axis=1))
```

```python
import functools
import math

import jax
import jax.numpy as jnp
from jax import lax
from jax.experimental import pallas as pl
from jax.experimental.pallas import tpu as pltpu

D_MODEL = 2048
BATCH = 32
SEQ = 256
DEPTH = 2
DEC_BATCH = 8
DEC_SEQ = 1024
PAST_LEN = 256
GRID_W = 64
HEAD_DIM = 128
WINDOW = 128
B_ROPE = 64
B_RANK = 256
D_HALF = 64
D_FF = 4 * D_MODEL
ROPE_BASE = 10000.0
EPS = 1e-6
MOD_CHUNKS = 6
MOD_ROWS = 16

GROUP = 512
Z_MAIN = 4608
Z_WIDTH = 4864
Z_TILE = Z_WIDTH // 2
KEYS = PAST_LEN + DEC_SEQ

LANES = 128
VMEM_LIMIT = 56 * 1024 * 1024

BF = jnp.bfloat16
F32 = jnp.float32
NEG = -1e30


def _dot(a, b):
    return jnp.dot(a, b, preferred_element_type=F32)


def _dot_nt(a, b):
    return lax.dot_general(a, b, (((1,), (1,)), ((), ())), preferred_element_type=F32)


def _rms(x, g):
    return x * lax.rsqrt(jnp.mean(x * x, axis=-1, keepdims=True) + EPS) * g


def _params(sem):
    return pltpu.CompilerParams(dimension_semantics=sem, vmem_limit_bytes=VMEM_LIMIT)


def _softmax_pv(s, v):
    m = jnp.max(s, axis=-1, keepdims=True)
    p = jnp.exp(s - m)
    l = jnp.sum(p, axis=-1, keepdims=True)
    return _dot(p.astype(BF), v) / l


def _softmax_pv_sink(s, v, sink):
    m = jnp.maximum(jnp.max(s, axis=-1, keepdims=True), sink)
    p = jnp.exp(s - m)
    l = jnp.sum(p, axis=-1, keepdims=True) + jnp.exp(sink - m)
    return _dot(p.astype(BF), v) / l


def _rope(x, cos, sin_signed, half):
    lane = lax.broadcasted_iota(jnp.int32, (8, LANES), 1)
    from_up = pltpu.roll(lane, half, 1)[0:1, :] == (lane[0:1, :] ^ half)
    partner = jnp.where(from_up, pltpu.roll(x, half, 1), pltpu.roll(x, LANES - half, 1))
    return x * cos + partner * sin_signed


def _lam(lam_ref, lam_init):
    lp = lam_ref[...]
    a = jnp.sum(lp[0:1, :] * lp[1:2, :], axis=-1, keepdims=True)
    b = jnp.sum(lp[2:3, :] * lp[3:4, :], axis=-1, keepdims=True)
    return jnp.exp(a) - jnp.exp(b) + lam_init


def _mod_kernel(c_ref, w_ref, b_ref, o_ref):
    c = c_ref[...]
    s = c * jax.nn.sigmoid(c)
    o_ref[...] = _dot(s.astype(BF), w_ref[...].astype(BF)) + b_ref[...]


def _modulation(cvec, w_ada, b_ada):
    tn = 1024
    n = MOD_CHUNKS * D_MODEL
    return pl.pallas_call(
        _mod_kernel,
        grid=(DEPTH, n // tn),
        in_specs=[
            pl.BlockSpec((MOD_ROWS, D_MODEL), lambda l, j: (0, 0)),
            pl.BlockSpec((None, D_MODEL, tn), lambda l, j: (l, 0, j)),
            pl.BlockSpec((None, 1, tn), lambda l, j: (l, 0, j)),
        ],
        out_specs=pl.BlockSpec((None, MOD_ROWS, tn), lambda l, j: (l, 0, j)),
        out_shape=jax.ShapeDtypeStruct((DEPTH, MOD_ROWS, n), F32),
        compiler_params=_params(("arbitrary", "arbitrary")),
        name="modulation",
    )(cvec, w_ada, b_ada.reshape(DEPTH, 1, n))


def _inproj_kernel(x_ref, mod_ref, g_ref, w_ref, z_ref):
    h = _rms(x_ref[...], g_ref[...]) * (1.0 + mod_ref[1:2, :]) + mod_ref[0:1, :]
    z_ref[...] = _dot(h.astype(BF), w_ref[...])


def _in_projection(x, mod, mod_row, l, g, w, tm):
    m = x.shape[0]
    return pl.pallas_call(
        _inproj_kernel,
        grid=(Z_WIDTH // Z_TILE, m // tm),
        in_specs=[
            pl.BlockSpec((tm, D_MODEL), lambda j, i: (i, 0)),
            pl.BlockSpec((None, None, MOD_CHUNKS, D_MODEL), lambda j, i: (l, mod_row(i), 0, 0)),
            pl.BlockSpec((1, D_MODEL), lambda j, i: (0, 0)),
            pl.BlockSpec((D_MODEL, Z_TILE), lambda j, i: (0, j)),
        ],
        out_specs=pl.BlockSpec((tm, Z_TILE), lambda j, i: (i, j)),
        out_shape=jax.ShapeDtypeStruct((m, Z_WIDTH), F32),
        compiler_params=_params(("arbitrary", "arbitrary")),
        name="in_projection",
    )(x, mod, g, w)


def _outproj_kernel(oa_ref, ob_ref, oc_ref, od_ref, w_ref, x_ref, mod_ref, g_ref, y_ref):
    acc = _dot(oa_ref[...], w_ref[0:GROUP, :])
    acc += _dot(ob_ref[...], w_ref[GROUP:2 * GROUP, :])
    acc += _dot(oc_ref[...], w_ref[2 * GROUP:3 * GROUP, :])
    acc += _dot(od_ref[...], w_ref[3 * GROUP:4 * GROUP, :])
    y_ref[...] = x_ref[...] + mod_ref[2:3, :] * _rms(acc, g_ref[...])


def _out_projection(o_parts, w, x, mod, mod_row, l, g, tm):
    m = x.shape[0]
    o_spec = pl.BlockSpec((tm, GROUP), lambda i: (i, 0))
    return pl.pallas_call(
        _outproj_kernel,
        grid=(m // tm,),
        in_specs=[
            o_spec, o_spec, o_spec, o_spec,
            pl.BlockSpec((D_MODEL, D_MODEL), lambda i: (0, 0)),
            pl.BlockSpec((tm, D_MODEL), lambda i: (i, 0)),
            pl.BlockSpec((None, None, MOD_CHUNKS, D_MODEL), lambda i: (l, mod_row(i), 0, 0)),
            pl.BlockSpec((1, D_MODEL), lambda i: (0, 0)),
        ],
        out_specs=pl.BlockSpec((tm, D_MODEL), lambda i: (i, 0)),
        out_shape=jax.ShapeDtypeStruct((m, D_MODEL), F32),
        compiler_params=_params(("arbitrary",)),
        name="out_projection",
    )(*o_parts, w, x, mod, g)


def _mlp_kernel(x_ref, mod_ref, gpre_ref, gpost_ref, wu_ref, wd_ref, y_ref, h_ref, acc_ref):
    k = pl.program_id(1)

    @pl.when(k == 0)
    def _():
        h = _rms(x_ref[...], gpre_ref[...]) * (1.0 + mod_ref[4:5, :]) + mod_ref[3:4, :]
        h_ref[...] = h.astype(BF)
        acc_ref[...] = jnp.zeros_like(acc_ref)

    u = jnp.square(jnp.maximum(_dot(h_ref[...], wu_ref[...]), 0.0))
    acc_ref[...] += _dot(u.astype(BF), wd_ref[...])

    @pl.when(k == pl.num_programs(1) - 1)
    def _():
        y_ref[...] = x_ref[...] + mod_ref[5:6, :] * _rms(acc_ref[...], gpost_ref[...])


def _mlp(x, mod, mod_row, l, gpre, gpost, wu, wd, tm, tf):
    m = x.shape[0]
    return pl.pallas_call(
        _mlp_kernel,
        grid=(m // tm, D_FF // tf),
        in_specs=[
            pl.BlockSpec((tm, D_MODEL), lambda i, k: (i, 0)),
            pl.BlockSpec((None, None, MOD_CHUNKS, D_MODEL), lambda i, k: (l, mod_row(i), 0, 0)),
            pl.BlockSpec((1, D_MODEL), lambda i, k: (0, 0)),
            pl.BlockSpec((1, D_MODEL), lambda i, k: (0, 0)),
            pl.BlockSpec((D_MODEL, tf), lambda i, k: (0, k)),
            pl.BlockSpec((tf, D_MODEL), lambda i, k: (k, 0)),
        ],
        out_specs=pl.BlockSpec((tm, D_MODEL), lambda i, k: (i, 0)),
        out_shape=jax.ShapeDtypeStruct((m, D_MODEL), F32),
        scratch_shapes=[pltpu.VMEM((tm, D_MODEL), BF), pltpu.VMEM((tm, D_MODEL), F32)],
        compiler_params=_params(("arbitrary", "arbitrary")),
        name="mlp",
    )(x, mod, gpre, gpost, wu, wd)


def _ctx_mixer_kernel(lam_init, z_ref, sink_ref, gkv_ref, wuk_ref, wuv_ref, gq_ref, gk_ref, lam_ref, gout_ref,
                      oa_ref, ob_ref, oc_ref, od_ref, ka_ref, va_ref, mla_ref, kc_ref, vc_ref, kd_ref, vd_ref):
    hd = HEAD_DIM

    scale = HEAD_DIM ** -0.5
    for hk in range(2):
        k = z_ref[:, 512 + hk * hd:512 + (hk + 1) * hd]
        v = z_ref[:, 768 + hk * hd:768 + (hk + 1) * hd]
        ka_ref[hk] = k
        va_ref[hk] = v
        kb, vb = k.astype(BF), v.astype(BF)
        for g in range(2):
            h = hk * 2 + g
            q = z_ref[:, h * hd:(h + 1) * hd].astype(BF)
            o = _softmax_pv_sink(_dot_nt(q, kb) * scale, vb, sink_ref[h])
            oa_ref[:, h * hd:(h + 1) * hd] = o.astype(BF)

    scale = (HEAD_DIM + B_ROPE) ** -0.5
    c_lat = _rms(z_ref[:, 1792:2048], gkv_ref[...])
    kr = z_ref[:, Z_MAIN:Z_MAIN + B_ROPE]
    mla_ref[:, 0:B_RANK] = c_lat
    mla_ref[:, B_RANK:B_RANK + B_ROPE] = kr
    cb, krb = c_lat.astype(BF), kr.astype(BF)
    for h in range(4):
        kn = _dot(cb, wuk_ref[:, h * hd:(h + 1) * hd]).astype(BF)
        v = _dot(cb, wuv_ref[:, h * hd:(h + 1) * hd]).astype(BF)
        qn = z_ref[:, 1024 + h * hd:1024 + (h + 1) * hd].astype(BF)
        qr = z_ref[:, 1536 + h * B_ROPE:1536 + (h + 1) * B_ROPE].astype(BF)
        s = (_dot_nt(qn, kn) + _dot_nt(qr, krb)) * scale
        ob_ref[:, h * hd:(h + 1) * hd] = _softmax_pv(s, v).astype(BF)

    scale = HEAD_DIM ** -0.5
    for hk in range(2):
        k = _rms(z_ref[:, 2560 + hk * hd:2560 + (hk + 1) * hd], gk_ref[...])
        v = z_ref[:, 2816 + hk * hd:2816 + (hk + 1) * hd]
        kc_ref[hk] = k
        vc_ref[hk] = v
        kb, vb = k.astype(BF), v.astype(BF)
        for g in range(2):
            h = hk * 2 + g
            q = _rms(z_ref[:, 2048 + h * hd:2048 + (h + 1) * hd], gq_ref[...]).astype(BF)
            oc_ref[:, h * hd:(h + 1) * hd] = _softmax_pv(_dot_nt(q, kb) * scale, vb).astype(BF)

    scale = D_HALF ** -0.5
    lam = _lam(lam_ref, lam_init)
    lane = lax.broadcasted_iota(jnp.int32, (SEQ, hd), 1)
    for h in range(4):
        k = z_ref[:, 3584 + h * hd:3584 + (h + 1) * hd]
        v = z_ref[:, 4096 + h * hd:4096 + (h + 1) * hd]
        kd_ref[h] = k
        vd_ref[h] = v
        kb, vb = k.astype(BF), v.astype(BF)
        q = z_ref[:, 3072 + h * hd:3072 + (h + 1) * hd]
        q1 = jnp.where(lane < D_HALF, q, 0.0).astype(BF)
        q2 = jnp.where(lane >= D_HALF, q, 0.0).astype(BF)
        o1 = _softmax_pv(_dot_nt(q1, kb) * scale, vb)
        o2 = _softmax_pv(_dot_nt(q2, kb) * scale, vb)
        od_ref[:, h * hd:(h + 1) * hd] = (_rms(o1 - lam * o2, gout_ref[...]) * (1.0 - lam_init)).astype(BF)


def _full(shape):
    return pl.BlockSpec(shape, lambda b: (0,) * len(shape))


def _ctx_mixers(z, p, lam_init):
    m = z.shape[0]
    o_spec = pl.BlockSpec((SEQ, GROUP), lambda b: (b, 0))
    o_shape = jax.ShapeDtypeStruct((m, GROUP), BF)

    def st_spec(h):
        return pl.BlockSpec((None, h, SEQ, HEAD_DIM), lambda b: (b, 0, 0, 0))

    def st_shape(h):
        return jax.ShapeDtypeStruct((BATCH, h, SEQ, HEAD_DIM), F32)

    return pl.pallas_call(
        functools.partial(_ctx_mixer_kernel, lam_init),
        grid=(BATCH,),
        in_specs=[
            pl.BlockSpec((SEQ, Z_WIDTH), lambda b: (b, 0)),
            pl.BlockSpec(memory_space=pltpu.SMEM),
            _full((1, B_RANK)), _full((B_RANK, GROUP)), _full((B_RANK, GROUP)),
            _full((1, HEAD_DIM)), _full((1, HEAD_DIM)), _full((4, D_HALF)), _full((1, HEAD_DIM)),
        ],
        out_specs=[
            o_spec, o_spec, o_spec, o_spec,
            st_spec(2), st_spec(2),
            pl.BlockSpec((None, SEQ, B_RANK + B_ROPE), lambda b: (b, 0, 0)),
            st_spec(2), st_spec(2), st_spec(4), st_spec(4),
        ],
        out_shape=[
            o_shape, o_shape, o_shape, o_shape,
            st_shape(2), st_shape(2),
            jax.ShapeDtypeStruct((BATCH, SEQ, B_RANK + B_ROPE), F32),
            st_shape(2), st_shape(2), st_shape(4), st_shape(4),
        ],
        compiler_params=_params(("arbitrary",)),
        name="ctx_mixers",
    )(z, p['a_sink'], p['b_g_kv'], p['w_uk'], p['w_uv'], p['c_gq'], p['c_gk'], p['d_lam'], p['d_g_out'])


QBLK = 256


def _lat_a_kernel(z_ref, ck_ref, cv_ref, cos_ref, sin_ref, sink_ref, o_ref, q_s, k_s, v_s):
    hd = HEAD_DIM
    cos, sin = cos_ref[...], sin_ref[...]
    zeros = jnp.zeros((WINDOW, hd), BF)
    for hk in range(2):
        k_s[hk, 0:WINDOW, :] = zeros
        k_s[hk, WINDOW + DEC_SEQ:, :] = zeros
        v_s[hk, 0:WINDOW, :] = zeros
        v_s[hk, WINDOW + DEC_SEQ:, :] = zeros
        k = _rope(z_ref[:, 512 + hk * hd:512 + (hk + 1) * hd], cos, sin, 32)
        k_s[hk, WINDOW:WINDOW + DEC_SEQ, :] = k.astype(BF)
        v_s[hk, WINDOW:WINDOW + DEC_SEQ, :] = z_ref[:, 768 + hk * hd:768 + (hk + 1) * hd].astype(BF)
    for h in range(4):
        q_s[h] = _rope(z_ref[:, h * hd:(h + 1) * hd], cos, sin, 32).astype(BF)

    scale = HEAD_DIM ** -0.5
    r_io = lax.broadcasted_iota(jnp.int32, (WINDOW, 3 * WINDOW), 0)
    c_io = lax.broadcasted_iota(jnp.int32, (WINDOW, 3 * WINDOW), 1)
    for hk in range(2):
        kc = ck_ref[hk].astype(BF)
        vc = cv_ref[hk].astype(BF)

        def body(i, carry, hk=hk, kc=kc, vc=vc):
            r0 = pl.multiple_of(i * WINDOW, WINDOW)
            kl = k_s[hk, pl.ds(r0, 3 * WINDOW), :]
            vl = v_s[hk, pl.ds(r0, 3 * WINDOW), :]
            kpos = (i - 1) * WINDOW + c_io
            qpos = i * WINDOW + r_io
            ok = (jnp.abs(qpos - kpos) <= WINDOW) & (kpos >= 0) & (kpos < DEC_SEQ)
            for g in range(2):
                h = hk * 2 + g
                q = q_s[h, pl.ds(r0, WINDOW), :]
                sink = sink_ref[h]
                sc = _dot_nt(q, kc) * scale
                sl = jnp.where(ok, _dot_nt(q, kl) * scale, NEG)
                m = jnp.maximum(jnp.maximum(jnp.max(sc, axis=-1, keepdims=True),
                                            jnp.max(sl, axis=-1, keepdims=True)), sink)
                pc = jnp.exp(sc - m)
                pw = jnp.exp(sl - m)
                l = (jnp.sum(pc, axis=-1, keepdims=True) + jnp.sum(pw, axis=-1, keepdims=True)
                     + jnp.exp(sink - m))
                o = (_dot(pc.astype(BF), vc) + _dot(pw.astype(BF), vl)) / l
                o_ref[pl.ds(r0, WINDOW), h * hd:(h + 1) * hd] = o.astype(BF)
            return carry

        lax.fori_loop(0, DEC_SEQ // WINDOW, body, 0)


def _lat_b_kernel(z_ref, zr_ref, mla_ref, gkv_ref, wuk_ref, wuv_ref, cos_ref, sin_ref, o_ref, c_s, k_s, v_s, q_s):
    hd = HEAD_DIM
    cos, sin = cos_ref[...], sin_ref[...]
    scale = (HEAD_DIM + B_ROPE) ** -0.5
    c_s[0:PAST_LEN, :] = mla_ref[:, 0:B_RANK].astype(BF)
    c_s[PAST_LEN:, :] = _rms(z_ref[:, 768:1024], gkv_ref[...]).astype(BF)
    k_s[0:PAST_LEN, hd:2 * hd] = jnp.zeros((PAST_LEN, hd), BF)
    k_s[0:PAST_LEN, hd:hd + B_ROPE] = mla_ref[:, B_RANK:B_RANK + B_ROPE].astype(BF)
    k_s[PAST_LEN:, hd:2 * hd] = _rope(zr_ref[...], cos, sin, 16).astype(BF)
    lane = lax.broadcasted_iota(jnp.int32, (DEC_SEQ, hd), 1)
    for h in range(4):
        k_s[:, 0:hd] = _dot(c_s[...], wuk_ref[:, h * hd:(h + 1) * hd]).astype(BF)
        v_s[...] = _dot(c_s[...], wuv_ref[:, h * hd:(h + 1) * hd]).astype(BF)
        q_s[:, 0:hd] = z_ref[:, h * hd:(h + 1) * hd].astype(BF)
        qr = _rope(z_ref[:, 512 + (h // 2) * hd:512 + (h // 2 + 1) * hd], cos, sin, 16)
        if h % 2:
            qr = pltpu.roll(qr, B_ROPE, 1)
        q_s[:, hd:2 * hd] = jnp.where(lane < B_ROPE, qr, 0.0).astype(BF)

        def body(i, carry, h=h):
            r0 = pl.multiple_of(i * QBLK, QBLK)
            s = _dot_nt(q_s[pl.ds(r0, QBLK), :], k_s[...]) * scale
            o_ref[pl.ds(r0, QBLK), h * hd:(h + 1) * hd] = _softmax_pv(s, v_s[...]).astype(BF)
            return carry

        lax.fori_loop(0, DEC_SEQ // QBLK, body, 0)


def _lat_c_kernel(z_ref, ck_ref, cv_ref, gq_ref, gk_ref, cos_ref, sin_ref, o_ref, k_s, v_s, q_s):
    hd = HEAD_DIM
    cos, sin = cos_ref[...], sin_ref[...]
    scale = HEAD_DIM ** -0.5
    for hk in range(2):
        k_s[0:PAST_LEN, :] = ck_ref[hk].astype(BF)
        k = _rms(z_ref[:, 512 + hk * hd:512 + (hk + 1) * hd], gk_ref[...])
        k_s[PAST_LEN:, :] = _rope(k, cos, sin, 32).astype(BF)
        v_s[0:PAST_LEN, :] = cv_ref[hk].astype(BF)
        v_s[PAST_LEN:, :] = z_ref[:, 768 + hk * hd:768 + (hk + 1) * hd].astype(BF)
        for g in range(2):
            h = hk * 2 + g
            q = _rms(z_ref[:, h * hd:(h + 1) * hd], gq_ref[...])
            q_s[...] = _rope(q, cos, sin, 32).astype(BF)

            def body(i, carry, h=h):
                r0 = pl.multiple_of(i * QBLK, QBLK)
                s = _dot_nt(q_s[pl.ds(r0, QBLK), :], k_s[...]) * scale
                o_ref[pl.ds(r0, QBLK), h * hd:(h + 1) * hd] = _softmax_pv(s, v_s[...]).astype(BF)
                return carry

            lax.fori_loop(0, DEC_SEQ // QBLK, body, 0)


def _lat_d_kernel(lam_init, z_ref, dk_ref, dv_ref, lam_ref, gout_ref, cos_ref, sin_ref, o_ref, k_s, v_s, q1_s, q2_s):
    hd = HEAD_DIM
    cos, sin = cos_ref[...], sin_ref[...]
    scale = D_HALF ** -0.5
    lam = _lam(lam_ref, lam_init)
    lane = lax.broadcasted_iota(jnp.int32, (DEC_SEQ, hd), 1)
    for h in range(4):
        k_s[0:PAST_LEN, :] = dk_ref[h].astype(BF)
        k_s[PAST_LEN:, :] = _rope(z_ref[:, 512 + h * hd:512 + (h + 1) * hd], cos, sin, 16).astype(BF)
        v_s[0:PAST_LEN, :] = dv_ref[h].astype(BF)
        v_s[PAST_LEN:, :] = z_ref[:, 1024 + h * hd:1024 + (h + 1) * hd].astype(BF)
        q = _rope(z_ref[:, h * hd:(h + 1) * hd], cos, sin, 16)
        q1_s[...] = jnp.where(lane < D_HALF, q, 0.0).astype(BF)
        q2_s[...] = jnp.where(lane >= D_HALF, q, 0.0).astype(BF)

        def body(i, carry, h=h):
            r0 = pl.multiple_of(i * QBLK, QBLK)
            o1 = _softmax_pv(_dot_nt(q1_s[pl.ds(r0, QBLK), :], k_s[...]) * scale, v_s[...])
            o2 = _softmax_pv(_dot_nt(q2_s[pl.ds(r0, QBLK), :], k_s[...]) * scale, v_s[...])
            o = _rms(o1 - lam * o2, gout_ref[...]) * (1.0 - lam_init)
            o_ref[pl.ds(r0, QBLK), h * hd:(h + 1) * hd] = o.astype(BF)
            return carry

        lax.fori_loop(0, DEC_SEQ // QBLK, body, 0)


def _lat_mixers(z, caches, l, p, tabs, lam_init):
    m = z.shape[0]
    hd = HEAD_DIM
    cos128, sin128, cos64, sin64 = tabs
    o_spec = pl.BlockSpec((DEC_SEQ, GROUP), lambda b: (b, 0))
    o_shape = jax.ShapeDtypeStruct((m, GROUP), BF)
    tab_spec = _full((DEC_SEQ, hd))

    def cache_spec(h):
        return pl.BlockSpec((None, None, h, PAST_LEN, hd), lambda b: (b, l, 0, 0, 0))

    def zspec(width, j):
        return pl.BlockSpec((DEC_SEQ, width), lambda b: (b, j))

    cp = _params(("arbitrary",))
    kv = lambda rows: pltpu.VMEM((rows, hd), BF)

    o_a = pl.pallas_call(
        _lat_a_kernel, grid=(DEC_BATCH,),
        in_specs=[zspec(1024, 0), cache_spec(2), cache_spec(2), tab_spec, tab_spec,
                  pl.BlockSpec(memory_space=pltpu.SMEM)],
        out_specs=o_spec, out_shape=o_shape,
        scratch_shapes=[pltpu.VMEM((4, DEC_SEQ, hd), BF),
                        pltpu.VMEM((2, DEC_SEQ + 2 * WINDOW, hd), BF),
                        pltpu.VMEM((2, DEC_SEQ + 2 * WINDOW, hd), BF)],
        compiler_params=cp, name="lat_mixer_a",
    )(z, caches['a_k'], caches['a_v'], cos128, sin128, p['a_sink'])

    o_b = pl.pallas_call(
        _lat_b_kernel, grid=(DEC_BATCH,),
        in_specs=[zspec(1024, 1), zspec(LANES, Z_MAIN // LANES),
                  pl.BlockSpec((None, None, PAST_LEN, B_RANK + B_ROPE), lambda b: (b, l, 0, 0)),
                  _full((1, B_RANK)), _full((B_RANK, GROUP)), _full((B_RANK, GROUP)), tab_spec, tab_spec],
        out_specs=o_spec, out_shape=o_shape,
        scratch_shapes=[pltpu.VMEM((KEYS, B_RANK), BF), pltpu.VMEM((KEYS, 2 * hd), BF), kv(KEYS),
                        pltpu.VMEM((DEC_SEQ, 2 * hd), BF)],
        compiler_params=cp, name="lat_mixer_b",
    )(z, z, caches['mla'], p['b_g_kv'], p['w_uk'], p['w_uv'], cos64, sin64)

    o_c = pl.pallas_call(
        _lat_c_kernel, grid=(DEC_BATCH,),
        in_specs=[zspec(1024, 2), cache_spec(2), cache_spec(2), _full((1, hd)), _full((1, hd)), tab_spec, tab_spec],
        out_specs=o_spec, out_shape=o_shape,
        scratch_shapes=[kv(KEYS), kv(KEYS), kv(DEC_SEQ)],
        compiler_params=cp, name="lat_mixer_c",
    )(z, caches['c_k'], caches['c_v'], p['c_gq'], p['c_gk'], cos128, sin128)

    o_d = pl.pallas_call(
        functools.partial(_lat_d_kernel, lam_init), grid=(DEC_BATCH,),
        in_specs=[zspec(1536, 2), cache_spec(4), cache_spec(4), _full((4, D_HALF)), _full((1, hd)), tab_spec, tab_spec],
        out_specs=o_spec, out_shape=o_shape,
        scratch_shapes=[kv(KEYS), kv(KEYS), kv(DEC_SEQ), kv(DEC_SEQ)],
        compiler_params=cp, name="lat_mixer_d",
    )(z, caches['d_k'], caches['d_v'], p['d_lam'], p['d_g_out'], cos64, sin64)

    return o_a, o_b, o_c, o_d


def _rope_tables():
    t = jnp.arange(DEC_SEQ)
    row = (t // GRID_W).astype(F32)[:, None]
    col = (t % GRID_W).astype(F32)[:, None]

    def table(half):
        inv = ROPE_BASE ** (-jnp.arange(half, dtype=F32) / half)
        ar, ac = row * inv[None, :], col * inv[None, :]
        cos = jnp.concatenate([jnp.cos(ar), jnp.cos(ar), jnp.cos(ac), jnp.cos(ac)], axis=-1)
        sin = jnp.concatenate([-jnp.sin(ar), jnp.sin(ar), -jnp.sin(ac), jnp.sin(ac)], axis=-1)
        return cos, sin

    cos128, sin128 = table(32)
    cos64, sin64 = table(16)
    return cos128, sin128, jnp.tile(cos64, (1, 2)), jnp.tile(sin64, (1, 2))


def kernel(x_prompt, x_sample, cache_a_k, cache_a_v, cache_mla, cache_c_k, cache_c_v, cache_d_k, cache_d_v,
           c, c_ctx, w_ada, b_ada, g_attn_pre, g_attn_post, g_mlp_pre, g_mlp_post, w_in, a_sink,
           b_g_kv, b_w_uk, b_w_uv, c_gq, c_gk, d_lam, d_g_out, w_out, w_up, w_down):
    w_in_p = jnp.concatenate(
        [w_in[:, :, :2048], w_in[:, :, 2112:], w_in[:, :, 2048:2112],
         jnp.zeros((DEPTH, D_MODEL, Z_WIDTH - 4672), F32)], axis=-1).astype(BF)
    w_out_b = w_out.astype(BF)
    w_up_b = w_up.astype(BF)
    w_down_b = w_down.astype(BF)
    w_uk_b = b_w_uk.reshape(DEPTH, B_RANK, GROUP).astype(BF)
    w_uv_b = b_w_uv.reshape(DEPTH, B_RANK, GROUP).astype(BF)

    cvec = jnp.concatenate([c_ctx[None, :], c, jnp.zeros((MOD_ROWS - 1 - DEC_BATCH, D_MODEL), F32)], axis=0)
    mod = _modulation(cvec, w_ada, b_ada).reshape(DEPTH, MOD_ROWS, MOD_CHUNKS, D_MODEL)
    tabs = _rope_tables()

    tm = 512
    ctx_row = lambda i: 0
    lat_row = lambda i: 1 + (i * tm) // DEC_SEQ

    xc = x_prompt.reshape(BATCH * SEQ, D_MODEL)
    xl = x_sample.reshape(DEC_BATCH * DEC_SEQ, D_MODEL)
    states = [[] for _ in range(7)]
    for l in range(DEPTH):
        lam_init = 0.8 - 0.6 * math.exp(-0.3 * l)
        p = {'a_sink': a_sink[l], 'b_g_kv': b_g_kv[l][None], 'w_uk': w_uk_b[l], 'w_uv': w_uv_b[l],
             'c_gq': c_gq[l][None], 'c_gk': c_gk[l][None], 'd_lam': d_lam[l], 'd_g_out': d_g_out[l][None]}
        caches = {'a_k': cache_a_k, 'a_v': cache_a_v, 'mla': cache_mla, 'c_k': cache_c_k, 'c_v': cache_c_v,
                  'd_k': cache_d_k, 'd_v': cache_d_v}
        g_pre, g_post = g_attn_pre[l][None], g_attn_post[l][None]
        g_mpre, g_mpost = g_mlp_pre[l][None], g_mlp_post[l][None]

        zc = _in_projection(xc, mod, ctx_row, l, g_pre, w_in_p[l], tm)
        outs = _ctx_mixers(zc, p, lam_init)
        for lst, t in zip(states, outs[4:]):
            lst.append(t)
        xc = _out_projection(outs[:4], w_out_b[l], xc, mod, ctx_row, l, g_post, tm)
        xc = _mlp(xc, mod, ctx_row, l, g_mpre, g_mpost, w_up_b[l], w_down_b[l], tm, 1024)

        zl = _in_projection(xl, mod, lat_row, l, g_pre, w_in_p[l], tm)
        o_parts = _lat_mixers(zl, caches, l, p, tabs, lam_init)
        xl = _out_projection(o_parts, w_out_b[l], xl, mod, lat_row, l, g_post, tm)
        xl = _mlp(xl, mod, lat_row, l, g_mpre, g_mpost, w_up_b[l], w_down_b[l], tm, 1024)

    y_prompt = xc.reshape(BATCH, SEQ, D_MODEL)
    y_sample = xl.reshape(DEC_BATCH, DEC_SEQ, D_MODEL)
    return (y_prompt, y_sample) + tuple(jnp.stack(s, axis=1) for s in states)
```

```python
import functools
import math

import jax
import jax.numpy as jnp
from jax import lax
from jax.experimental import pallas as pl
from jax.experimental.pallas import tpu as pltpu

D_MODEL = 2048
BATCH = 32
SEQ = 256
DEPTH = 2
DEC_BATCH = 8
DEC_SEQ = 1024
PAST_LEN = 256
GRID_W = 64
HEAD_DIM = 128
WINDOW = 128
B_ROPE = 64
B_RANK = 256
D_HALF = 64
D_FF = 4 * D_MODEL
ROPE_BASE = 10000.0
EPS = 1e-6
MOD_CHUNKS = 6
MOD_ROWS = 16

GROUP = 512
Z_MAIN = 4608
Z_WIDTH = 4864
Z_TILE = Z_WIDTH // 2
KEYS = PAST_LEN + DEC_SEQ

LANES = 128
VMEM_LIMIT = 56 * 1024 * 1024

BF = jnp.bfloat16
F32 = jnp.float32
NEG = -1e30
LOG2E = math.log2(math.e)


def _dot(a, b):
    return jnp.dot(a, b, preferred_element_type=F32)


def _dot_nt(a, b):
    return lax.dot_general(a, b, (((1,), (1,)), ((), ())), preferred_element_type=F32)


def _rms(x, g):
    return x * lax.rsqrt(jnp.mean(x * x, axis=-1, keepdims=True) + EPS) * g


def _params(sem):
    return pltpu.CompilerParams(dimension_semantics=sem, vmem_limit_bytes=VMEM_LIMIT)


def _with_ones(v):
    return jnp.concatenate([v, jnp.ones(v.shape, v.dtype)], axis=1)


def _attend(q, k, v1, sink2=None):
    s = _dot_nt(q, k)
    m = jnp.max(s, axis=-1, keepdims=True)
    if sink2 is not None:
        m = jnp.maximum(m, sink2)
    o = _dot(jnp.exp2(s - m).astype(BF), v1)
    l = o[:, HEAD_DIM:]
    if sink2 is not None:
        l = l + jnp.exp2(sink2 - m)
    return o[:, :HEAD_DIM] / l


def _rope(x, cos, sin_signed, half):
    lane = lax.broadcasted_iota(jnp.int32, (8, LANES), 1)
    from_up = pltpu.roll(lane, half, 1)[0:1, :] == (lane[0:1, :] ^ half)
    partner = jnp.where(from_up, pltpu.roll(x, half, 1), pltpu.roll(x, LANES - half, 1))
    return x * cos + partner * sin_signed


def _lam(lam_ref, lam_init):
    lp = lam_ref[...]
    a = jnp.sum(lp[0:1, :] * lp[1:2, :], axis=-1, keepdims=True)
    b = jnp.sum(lp[2:3, :] * lp[3:4, :], axis=-1, keepdims=True)
    return jnp.exp(a) - jnp.exp(b) + lam_init


def _mod_kernel(c_ref, w_ref, b_ref, o_ref):
    c = c_ref[...]
    s = c * jax.nn.sigmoid(c)
    o_ref[...] = _dot(s.astype(BF), w_ref[...].astype(BF)) + b_ref[...]


def _modulation(cvec, w_ada, b_ada):
    tn = 1024
    n = MOD_CHUNKS * D_MODEL
    return pl.pallas_call(
        _mod_kernel,
        grid=(DEPTH, n // tn),
        in_specs=[
            pl.BlockSpec((MOD_ROWS, D_MODEL), lambda l, j: (0, 0)),
            pl.BlockSpec((None, D_MODEL, tn), lambda l, j: (l, 0, j)),
            pl.BlockSpec((None, 1, tn), lambda l, j: (l, 0, j)),
        ],
        out_specs=pl.BlockSpec((None, MOD_ROWS, tn), lambda l, j: (l, 0, j)),
        out_shape=jax.ShapeDtypeStruct((DEPTH, MOD_ROWS, n), F32),
        compiler_params=_params(("arbitrary", "arbitrary")),
        name="modulation",
    )(cvec, w_ada, b_ada.reshape(DEPTH, 1, n))


def _inproj_kernel(x_ref, mod_ref, g_ref, w_ref, z_ref):
    h = _rms(x_ref[...], g_ref[...]) * (1.0 + mod_ref[1:2, :]) + mod_ref[0:1, :]
    z_ref[...] = _dot(h.astype(BF), w_ref[...])


def _in_projection(x, mod, mod_row, l, g, w, tm):
    m = x.shape[0]
    return pl.pallas_call(
        _inproj_kernel,
        grid=(Z_WIDTH // Z_TILE, m // tm),
        in_specs=[
            pl.BlockSpec((tm, D_MODEL), lambda j, i: (i, 0)),
            pl.BlockSpec((None, None, MOD_CHUNKS, D_MODEL), lambda j, i: (l, mod_row(i), 0, 0)),
            pl.BlockSpec((1, D_MODEL), lambda j, i: (0, 0)),
            pl.BlockSpec((None, D_MODEL, Z_TILE), lambda j, i: (l, 0, j)),
        ],
        out_specs=pl.BlockSpec((tm, Z_TILE), lambda j, i: (i, j)),
        out_shape=jax.ShapeDtypeStruct((m, Z_WIDTH), F32),
        compiler_params=_params(("arbitrary", "arbitrary")),
        name="in_projection",
    )(x, mod, g, w)


def _outproj_kernel(oa_ref, ob_ref, oc_ref, od_ref, w_ref, x_ref, mod_ref, g_ref, y_ref):
    acc = _dot(oa_ref[...], w_ref[0:GROUP, :])
    acc += _dot(ob_ref[...], w_ref[GROUP:2 * GROUP, :])
    acc += _dot(oc_ref[...], w_ref[2 * GROUP:3 * GROUP, :])
    acc += _dot(od_ref[...], w_ref[3 * GROUP:4 * GROUP, :])
    y_ref[...] = x_ref[...] + mod_ref[2:3, :] * _rms(acc, g_ref[...])


def _out_projection(o_parts, w, x, mod, mod_row, l, g, tm):
    m = x.shape[0]
    o_spec = pl.BlockSpec((tm, GROUP), lambda i: (i, 0))
    return pl.pallas_call(
        _outproj_kernel,
        grid=(m // tm,),
        in_specs=[
            o_spec, o_spec, o_spec, o_spec,
            pl.BlockSpec((None, D_MODEL, D_MODEL), lambda i: (l, 0, 0)),
            pl.BlockSpec((tm, D_MODEL), lambda i: (i, 0)),
            pl.BlockSpec((None, None, MOD_CHUNKS, D_MODEL), lambda i: (l, mod_row(i), 0, 0)),
            pl.BlockSpec((1, D_MODEL), lambda i: (0, 0)),
        ],
        out_specs=pl.BlockSpec((tm, D_MODEL), lambda i: (i, 0)),
        out_shape=jax.ShapeDtypeStruct((m, D_MODEL), F32),
        compiler_params=_params(("arbitrary",)),
        name="out_projection",
    )(*o_parts, w, x, mod, g)


def _mlp_kernel(x_ref, mod_ref, gpre_ref, gpost_ref, wu_ref, wd_ref, y_ref, h_ref, acc_ref):
    k = pl.program_id(1)

    @pl.when(k == 0)
    def _():
        h = _rms(x_ref[...], gpre_ref[...]) * (1.0 + mod_ref[4:5, :]) + mod_ref[3:4, :]
        h_ref[...] = h.astype(BF)
        acc_ref[...] = jnp.zeros_like(acc_ref)

    u = jnp.square(jnp.maximum(_dot(h_ref[...], wu_ref[...]), 0.0))
    acc_ref[...] += _dot(u.astype(BF), wd_ref[...])

    @pl.when(k == pl.num_programs(1) - 1)
    def _():
        y_ref[...] = x_ref[...] + mod_ref[5:6, :] * _rms(acc_ref[...], gpost_ref[...])


def _mlp(x, mod, mod_row, l, gpre, gpost, wu, wd, tm, tf):
    m = x.shape[0]
    return pl.pallas_call(
        _mlp_kernel,
        grid=(m // tm, D_FF // tf),
        in_specs=[
            pl.BlockSpec((tm, D_MODEL), lambda i, k: (i, 0)),
            pl.BlockSpec((None, None, MOD_CHUNKS, D_MODEL), lambda i, k: (l, mod_row(i), 0, 0)),
            pl.BlockSpec((1, D_MODEL), lambda i, k: (0, 0)),
            pl.BlockSpec((1, D_MODEL), lambda i, k: (0, 0)),
            pl.BlockSpec((None, D_MODEL, tf), lambda i, k: (l, 0, k)),
            pl.BlockSpec((None, tf, D_MODEL), lambda i, k: (l, k, 0)),
        ],
        out_specs=pl.BlockSpec((tm, D_MODEL), lambda i, k: (i, 0)),
        out_shape=jax.ShapeDtypeStruct((m, D_MODEL), F32),
        scratch_shapes=[pltpu.VMEM((tm, D_MODEL), BF), pltpu.VMEM((tm, D_MODEL), F32)],
        compiler_params=_params(("arbitrary", "arbitrary")),
        name="mlp",
    )(x, mod, gpre, gpost, wu, wd)


N_STATES = 7
STATE_HEADS = (2, 2, None, 2, 2, 4, 4)


def _ctx_mixer_kernel(lam_init, n_prev, z_ref, sink_ref, gkv_ref, wuk_ref, wuv_ref, gq_ref, gk_ref, lam_ref, gout_ref,
                      *refs):
    prev = refs[:n_prev]
    oa_ref, ob_ref, oc_ref, od_ref = refs[n_prev:n_prev + 4]
    st = refs[n_prev + 4:]
    if n_prev:
        for p_ref, s_ref in zip(prev, st):
            s_ref[0] = p_ref[...]
        ka_ref, va_ref, mla_ref, kc_ref, vc_ref, kd_ref, vd_ref = [s.at[1] for s in st]
    else:
        ka_ref, va_ref, mla_ref, kc_ref, vc_ref, kd_ref, vd_ref = st
    hd = HEAD_DIM

    c = HEAD_DIM ** -0.5 * LOG2E
    for hk in range(2):
        k = z_ref[:, 512 + hk * hd:512 + (hk + 1) * hd]
        v = z_ref[:, 768 + hk * hd:768 + (hk + 1) * hd]
        ka_ref[hk] = k
        va_ref[hk] = v
        kb, v1 = k.astype(BF), _with_ones(v.astype(BF))
        for g in range(2):
            h = hk * 2 + g
            q = (z_ref[:, h * hd:(h + 1) * hd] * c).astype(BF)
            oa_ref[:, h * hd:(h + 1) * hd] = _attend(q, kb, v1, sink_ref[h] * LOG2E).astype(BF)

    c = (HEAD_DIM + B_ROPE) ** -0.5 * LOG2E
    c_lat = _rms(z_ref[:, 1792:2048], gkv_ref[...])
    kr = z_ref[:, Z_MAIN:Z_MAIN + B_ROPE]
    mla_ref[:, 0:B_RANK] = c_lat
    mla_ref[:, B_RANK:B_RANK + B_ROPE] = kr
    cb, krb = c_lat.astype(BF), kr.astype(BF)
    for h in range(4):
        kn = _dot(cb, wuk_ref[:, h * hd:(h + 1) * hd]).astype(BF)
        v1 = _with_ones(_dot(cb, wuv_ref[:, h * hd:(h + 1) * hd]).astype(BF))
        qn = (z_ref[:, 1024 + h * hd:1024 + (h + 1) * hd] * c).astype(BF)
        qr = (z_ref[:, 1536 + h * B_ROPE:1536 + (h + 1) * B_ROPE] * c).astype(BF)
        s = _dot_nt(qn, kn) + _dot_nt(qr, krb)
        m = jnp.max(s, axis=-1, keepdims=True)
        o = _dot(jnp.exp2(s - m).astype(BF), v1)
        ob_ref[:, h * hd:(h + 1) * hd] = (o[:, :hd] / o[:, hd:]).astype(BF)

    c = HEAD_DIM ** -0.5 * LOG2E
    for hk in range(2):
        k = _rms(z_ref[:, 2560 + hk * hd:2560 + (hk + 1) * hd], gk_ref[...])
        v = z_ref[:, 2816 + hk * hd:2816 + (hk + 1) * hd]
        kc_ref[hk] = k
        vc_ref[hk] = v
        kb, v1 = k.astype(BF), _with_ones(v.astype(BF))
        for g in range(2):
            h = hk * 2 + g
            q = (_rms(z_ref[:, 2048 + h * hd:2048 + (h + 1) * hd], gq_ref[...]) * c).astype(BF)
            oc_ref[:, h * hd:(h + 1) * hd] = _attend(q, kb, v1).astype(BF)

    c = D_HALF ** -0.5 * LOG2E
    lam = _lam(lam_ref, lam_init)
    lane = lax.broadcasted_iota(jnp.int32, (SEQ, hd), 1)
    for h in range(4):
        k = z_ref[:, 3584 + h * hd:3584 + (h + 1) * hd]
        v = z_ref[:, 4096 + h * hd:4096 + (h + 1) * hd]
        kd_ref[h] = k
        vd_ref[h] = v
        kb, v1 = k.astype(BF), _with_ones(v.astype(BF))
        q = z_ref[:, 3072 + h * hd:3072 + (h + 1) * hd] * c
        o1 = _attend(jnp.where(lane < D_HALF, q, 0.0).astype(BF), kb, v1)
        o2 = _attend(jnp.where(lane >= D_HALF, q, 0.0).astype(BF), kb, v1)
        od_ref[:, h * hd:(h + 1) * hd] = (_rms(o1 - lam * o2, gout_ref[...]) * (1.0 - lam_init)).astype(BF)


def _const(shape):
    return pl.BlockSpec(shape, lambda *_: (0,) * len(shape))


def _ctx_mixers(z, p, lam_init, prev_states):
    m = z.shape[0]
    o_spec = pl.BlockSpec((SEQ, GROUP), lambda b: (b, 0))
    o_shape = jax.ShapeDtypeStruct((m, GROUP), BF)
    n_prev = len(prev_states)
    lead = (DEPTH,) if n_prev else ()

    def st_dims(h):
        return (SEQ, B_RANK + B_ROPE) if h is None else (h, SEQ, HEAD_DIM)

    def st_spec(h, lead):
        dims = lead + st_dims(h)
        return pl.BlockSpec((None,) + dims, lambda b: (b,) + (0,) * len(dims))

    return pl.pallas_call(
        functools.partial(_ctx_mixer_kernel, lam_init, n_prev),
        grid=(BATCH,),
        in_specs=[
            pl.BlockSpec((SEQ, Z_WIDTH), lambda b: (b, 0)),
            pl.BlockSpec(memory_space=pltpu.SMEM),
            _const((1, B_RANK)), _const((B_RANK, GROUP)), _const((B_RANK, GROUP)),
            _const((1, HEAD_DIM)), _const((1, HEAD_DIM)), _const((4, D_HALF)), _const((1, HEAD_DIM)),
        ] + [st_spec(h, ()) for h in STATE_HEADS[:n_prev]],
        out_specs=[o_spec] * 4 + [st_spec(h, lead) for h in STATE_HEADS],
        out_shape=[o_shape] * 4 + [jax.ShapeDtypeStruct((BATCH,) + lead + st_dims(h), F32) for h in STATE_HEADS],
        compiler_params=_params(("arbitrary",)),
        name="ctx_mixers",
    )(z, p['a_sink'], p['b_g_kv'], p['w_uk'], p['w_uv'], p['c_gq'], p['c_gk'], p['d_lam'], p['d_g_out'],
      *prev_states)


QBLK = 256
N_QBLK = DEC_SEQ // QBLK


def _lat_a_kernel(z_ref, ck_ref, cv_ref, cos_ref, sin_ref, sink_ref, o_ref, q_s, k_s, v_s):
    hd = HEAD_DIM
    cos, sin = cos_ref[...], sin_ref[...]
    c = HEAD_DIM ** -0.5 * LOG2E
    for hk in range(2):
        k_s[hk, 0:WINDOW, :] = jnp.zeros((WINDOW, hd), BF)
        k_s[hk, WINDOW + DEC_SEQ:, :] = jnp.zeros((WINDOW, hd), BF)
        v_s[hk, 0:WINDOW, :] = jnp.zeros((WINDOW, 2 * hd), BF)
        v_s[hk, WINDOW + DEC_SEQ:, :] = jnp.zeros((WINDOW, 2 * hd), BF)
        k = _rope(z_ref[:, 512 + hk * hd:512 + (hk + 1) * hd], cos, sin, 32)
        k_s[hk, WINDOW:WINDOW + DEC_SEQ, :] = k.astype(BF)
        v_s[hk, WINDOW:WINDOW + DEC_SEQ, :] = _with_ones(z_ref[:, 768 + hk * hd:768 + (hk + 1) * hd].astype(BF))
    for h in range(4):
        q_s[h] = (_rope(z_ref[:, h * hd:(h + 1) * hd], cos, sin, 32) * c).astype(BF)

    r_io = lax.broadcasted_iota(jnp.int32, (WINDOW, 3 * WINDOW), 0)
    c_io = lax.broadcasted_iota(jnp.int32, (WINDOW, 3 * WINDOW), 1)
    for hk in range(2):
        kc = ck_ref[hk].astype(BF)
        vc1 = _with_ones(cv_ref[hk].astype(BF))
        for i in range(DEC_SEQ // WINDOW):
            r0 = i * WINDOW
            kl = k_s[hk, r0:r0 + 3 * WINDOW, :]
            vl1 = v_s[hk, r0:r0 + 3 * WINDOW, :]
            kpos = (i - 1) * WINDOW + c_io
            qpos = i * WINDOW + r_io
            ok = (jnp.abs(qpos - kpos) <= WINDOW) & (kpos >= 0) & (kpos < DEC_SEQ)
            for g in range(2):
                h = hk * 2 + g
                q = q_s[h, r0:r0 + WINDOW, :]
                sink2 = sink_ref[h] * LOG2E
                sc = _dot_nt(q, kc)
                sl = _dot_nt(q, kl)
                m = jnp.maximum(jnp.maximum(jnp.max(sc, axis=-1, keepdims=True),
                                            jnp.max(jnp.where(ok, sl, NEG), axis=-1, keepdims=True)), sink2)
                pw = jnp.where(ok, jnp.exp2(sl - m), 0.0)
                o = _dot(jnp.exp2(sc - m).astype(BF), vc1) + _dot(pw.astype(BF), vl1)
                o = o[:, :hd] / (o[:, hd:] + jnp.exp2(sink2 - m))
                o_ref[r0:r0 + WINDOW, h * hd:(h + 1) * hd] = o.astype(BF)


def _lat_b_kernel(qn_ref, qr_ref, ckv_ref, zr_ref, mla_ref, gkv_ref, wuk_ref, wuv_ref, cos_ref, sin_ref, o_ref,
                  c_s, kr_s, k_s, v_s, q_s):
    hd = HEAD_DIM
    h = pl.program_id(1)
    cos, sin = cos_ref[...], sin_ref[...]
    c = (HEAD_DIM + B_ROPE) ** -0.5 * LOG2E

    @pl.when(h == 0)
    def _():
        c_s[0:PAST_LEN, :] = mla_ref[:, 0:B_RANK].astype(BF)
        c_s[PAST_LEN:, :] = _rms(ckv_ref[...], gkv_ref[...]).astype(BF)
        kr_s[0:PAST_LEN, :] = jnp.zeros((PAST_LEN, hd), BF)
        kr_s[0:PAST_LEN, 0:B_ROPE] = mla_ref[:, B_RANK:B_RANK + B_ROPE].astype(BF)
        kr_s[PAST_LEN:, :] = _rope(zr_ref[...], cos, sin, 16).astype(BF)

    k_s[:, 0:hd] = _dot(c_s[...], wuk_ref[...]).astype(BF)
    k_s[:, hd:2 * hd] = kr_s[...]
    v_s[...] = _with_ones(_dot(c_s[...], wuv_ref[...]).astype(BF))
    q_s[:, 0:hd] = (qn_ref[...] * c).astype(BF)
    qr = _rope(qr_ref[...], cos, sin, 16) * c
    qr = jnp.where(h % 2 == 1, pltpu.roll(qr, B_ROPE, 1), qr)
    lane = lax.broadcasted_iota(jnp.int32, (DEC_SEQ, hd), 1)
    q_s[:, hd:2 * hd] = jnp.where(lane < B_ROPE, qr, 0.0).astype(BF)
    for i in range(N_QBLK):
        r0 = i * QBLK
        o_ref[r0:r0 + QBLK, :] = _attend(q_s[r0:r0 + QBLK, :], k_s[...], v_s[...]).astype(BF)


def _lat_c_kernel(q_ref, k_ref, v_ref, ck_ref, cv_ref, gq_ref, gk_ref, cos_ref, sin_ref, o_ref, k_s, v_s, q_s):
    hd = HEAD_DIM
    cos, sin = cos_ref[...], sin_ref[...]
    c = HEAD_DIM ** -0.5 * LOG2E
    k_s[0:PAST_LEN, :] = ck_ref[...].astype(BF)
    k_s[PAST_LEN:, :] = _rope(_rms(k_ref[...], gk_ref[...]), cos, sin, 32).astype(BF)
    v_s[0:PAST_LEN, :] = _with_ones(cv_ref[...].astype(BF))
    v_s[PAST_LEN:, :] = _with_ones(v_ref[...].astype(BF))
    for g in range(2):
        q = _rms(q_ref[:, g * hd:(g + 1) * hd], gq_ref[...])
        q_s[g] = (_rope(q, cos, sin, 32) * c).astype(BF)
    for g in range(2):
        for i in range(N_QBLK):
            r0 = i * QBLK
            o = _attend(q_s[g, r0:r0 + QBLK, :], k_s[...], v_s[...])
            o_ref[r0:r0 + QBLK, g * hd:(g + 1) * hd] = o.astype(BF)


def _lat_d_kernel(lam_init, q_ref, k_ref, v_ref, dk_ref, dv_ref, lam_ref, gout_ref, cos_ref, sin_ref, o_ref,
                  k_s, v_s, q1_s, q2_s):
    hd = HEAD_DIM
    cos, sin = cos_ref[...], sin_ref[...]
    c = D_HALF ** -0.5 * LOG2E
    lam = _lam(lam_ref, lam_init)
    k_s[0:PAST_LEN, :] = dk_ref[...].astype(BF)
    k_s[PAST_LEN:, :] = _rope(k_ref[...], cos, sin, 16).astype(BF)
    v_s[0:PAST_LEN, :] = _with_ones(dv_ref[...].astype(BF))
    v_s[PAST_LEN:, :] = _with_ones(v_ref[...].astype(BF))
    q = _rope(q_ref[...], cos, sin, 16) * c
    lane = lax.broadcasted_iota(jnp.int32, (DEC_SEQ, hd), 1)
    q1_s[...] = jnp.where(lane < D_HALF, q, 0.0).astype(BF)
    q2_s[...] = jnp.where(lane >= D_HALF, q, 0.0).astype(BF)
    for i in range(N_QBLK):
        r0 = i * QBLK
        o1 = _attend(q1_s[r0:r0 + QBLK, :], k_s[...], v_s[...])
        o2 = _attend(q2_s[r0:r0 + QBLK, :], k_s[...], v_s[...])
        o = _rms(o1 - lam * o2, gout_ref[...]) * (1.0 - lam_init)
        o_ref[r0:r0 + QBLK, :] = o.astype(BF)


def _lat_mixers(z, caches, l, p, tabs, lam_init):
    m = z.shape[0]
    hd = HEAD_DIM
    cos128, sin128, cos64, sin64 = tabs
    o_shape = jax.ShapeDtypeStruct((m, GROUP), BF)
    kv = lambda rows, width: pltpu.VMEM((rows, width), BF)
    tab = _const((DEC_SEQ, hd))

    o_a = pl.pallas_call(
        _lat_a_kernel, grid=(DEC_BATCH,),
        in_specs=[pl.BlockSpec((DEC_SEQ, 1024), lambda b: (b, 0)),
                  pl.BlockSpec((None, None, 2, PAST_LEN, hd), lambda b: (b, l, 0, 0, 0)),
                  pl.BlockSpec((None, None, 2, PAST_LEN, hd), lambda b: (b, l, 0, 0, 0)),
                  tab, tab, pl.BlockSpec(memory_space=pltpu.SMEM)],
        out_specs=pl.BlockSpec((DEC_SEQ, GROUP), lambda b: (b, 0)), out_shape=o_shape,
        scratch_shapes=[pltpu.VMEM((4, DEC_SEQ, hd), BF),
                        pltpu.VMEM((2, DEC_SEQ + 2 * WINDOW, hd), BF),
                        pltpu.VMEM((2, DEC_SEQ + 2 * WINDOW, 2 * hd), BF)],
        compiler_params=_params(("arbitrary",)), name="lat_mixer_a",
    )(z, caches['a_k'], caches['a_v'], cos128, sin128, p['a_sink'])

    def zcol(width, col0, per_step=1):
        return pl.BlockSpec((DEC_SEQ, width), lambda b, h: (b, col0 // width + h // per_step))

    def zfix(width, col0):
        return pl.BlockSpec((DEC_SEQ, width), lambda b, h: (b, col0 // width))

    def cache_head():
        return pl.BlockSpec((None, None, None, PAST_LEN, hd), lambda b, h: (b, l, h, 0, 0))

    cp2 = _params(("arbitrary", "arbitrary"))

    o_b = pl.pallas_call(
        _lat_b_kernel, grid=(DEC_BATCH, 4),
        in_specs=[zcol(hd, 1024), zcol(hd, 1536, per_step=2), zfix(B_RANK, 1792), zfix(LANES, Z_MAIN),
                  pl.BlockSpec((None, None, PAST_LEN, B_RANK + B_ROPE), lambda b, h: (b, l, 0, 0)),
                  _const((1, B_RANK)),
                  pl.BlockSpec((B_RANK, hd), lambda b, h: (0, h)), pl.BlockSpec((B_RANK, hd), lambda b, h: (0, h)),
                  tab, tab],
        out_specs=pl.BlockSpec((DEC_SEQ, hd), lambda b, h: (b, h)), out_shape=o_shape,
        scratch_shapes=[kv(KEYS, B_RANK), kv(KEYS, hd), kv(KEYS, 2 * hd), kv(KEYS, 2 * hd), kv(DEC_SEQ, 2 * hd)],
        compiler_params=cp2, name="lat_mixer_b",
    )(z, z, z, z, caches['mla'], p['b_g_kv'], p['w_uk'], p['w_uv'], cos64, sin64)

    o_c = pl.pallas_call(
        _lat_c_kernel, grid=(DEC_BATCH, 2),
        in_specs=[zcol(2 * hd, 2048), zcol(hd, 2560), zcol(hd, 2816), cache_head(), cache_head(),
                  _const((1, hd)), _const((1, hd)), tab, tab],
        out_specs=pl.BlockSpec((DEC_SEQ, 2 * hd), lambda b, h: (b, h)), out_shape=o_shape,
        scratch_shapes=[kv(KEYS, hd), kv(KEYS, 2 * hd), pltpu.VMEM((2, DEC_SEQ, hd), BF)],
        compiler_params=cp2, name="lat_mixer_c",
    )(z, z, z, caches['c_k'], caches['c_v'], p['c_gq'], p['c_gk'], cos128, sin128)

    o_d = pl.pallas_call(
        functools.partial(_lat_d_kernel, lam_init), grid=(DEC_BATCH, 4),
        in_specs=[zcol(hd, 3072), zcol(hd, 3584), zcol(hd, 4096), cache_head(), cache_head(),
                  _const((4, D_HALF)), _const((1, hd)), tab, tab],
        out_specs=pl.BlockSpec((DEC_SEQ, hd), lambda b, h: (b, h)), out_shape=o_shape,
        scratch_shapes=[kv(KEYS, hd), kv(KEYS, 2 * hd), kv(DEC_SEQ, hd), kv(DEC_SEQ, hd)],
        compiler_params=cp2, name="lat_mixer_d",
    )(z, z, z, caches['d_k'], caches['d_v'], p['d_lam'], p['d_g_out'], cos64, sin64)

    return o_a, o_b, o_c, o_d


def _rope_tables():
    t = jnp.arange(DEC_SEQ)
    row = (t // GRID_W).astype(F32)[:, None]
    col = (t % GRID_W).astype(F32)[:, None]

    def table(half):
        inv = ROPE_BASE ** (-jnp.arange(half, dtype=F32) / half)
        ar, ac = row * inv[None, :], col * inv[None, :]
        cos = jnp.concatenate([jnp.cos(ar), jnp.cos(ar), jnp.cos(ac), jnp.cos(ac)], axis=-1)
        sin = jnp.concatenate([-jnp.sin(ar), jnp.sin(ar), -jnp.sin(ac), jnp.sin(ac)], axis=-1)
        return cos, sin

    cos128, sin128 = table(32)
    cos64, sin64 = table(16)
    return cos128, sin128, jnp.tile(cos64, (1, 2)), jnp.tile(sin64, (1, 2))


def kernel(x_prompt, x_sample, cache_a_k, cache_a_v, cache_mla, cache_c_k, cache_c_v, cache_d_k, cache_d_v,
           c, c_ctx, w_ada, b_ada, g_attn_pre, g_attn_post, g_mlp_pre, g_mlp_post, w_in, a_sink,
           b_g_kv, b_w_uk, b_w_uv, c_gq, c_gk, d_lam, d_g_out, w_out, w_up, w_down):
    assert DEPTH == 2
    w_in_p = jnp.concatenate(
        [w_in[:, :, :2048], w_in[:, :, 2112:], w_in[:, :, 2048:2112],
         jnp.zeros((DEPTH, D_MODEL, Z_WIDTH - 4672), F32)], axis=-1).astype(BF)
    w_out_b = w_out.astype(BF)
    w_up_b = w_up.astype(BF)
    w_down_b = w_down.astype(BF)
    w_uk_b = b_w_uk.reshape(DEPTH, B_RANK, GROUP).astype(BF)
    w_uv_b = b_w_uv.reshape(DEPTH, B_RANK, GROUP).astype(BF)

    cvec = jnp.concatenate([c_ctx[None, :], c, jnp.zeros((MOD_ROWS - 1 - DEC_BATCH, D_MODEL), F32)], axis=0)
    mod = _modulation(cvec, w_ada, b_ada).reshape(DEPTH, MOD_ROWS, MOD_CHUNKS, D_MODEL)
    tabs = _rope_tables()

    tm = 512
    ctx_row = lambda i: 0
    lat_row = lambda i: 1 + (i * tm) // DEC_SEQ
    caches = {'a_k': cache_a_k, 'a_v': cache_a_v, 'mla': cache_mla, 'c_k': cache_c_k, 'c_v': cache_c_v,
              'd_k': cache_d_k, 'd_v': cache_d_v}

    xc = x_prompt.reshape(BATCH * SEQ, D_MODEL)
    xl = x_sample.reshape(DEC_BATCH * DEC_SEQ, D_MODEL)
    states = ()
    for l in range(DEPTH):
        lam_init = 0.8 - 0.6 * math.exp(-0.3 * l)
        p = {'a_sink': a_sink[l], 'b_g_kv': b_g_kv[l][None], 'w_uk': w_uk_b[l], 'w_uv': w_uv_b[l],
             'c_gq': c_gq[l][None], 'c_gk': c_gk[l][None], 'd_lam': d_lam[l], 'd_g_out': d_g_out[l][None]}
        g_pre, g_post = g_attn_pre[l][None], g_attn_post[l][None]
        g_mpre, g_mpost = g_mlp_pre[l][None], g_mlp_post[l][None]

        zc = _in_projection(xc, mod, ctx_row, l, g_pre, w_in_p, tm)
        outs = _ctx_mixers(zc, p, lam_init, states)
        states = tuple(outs[4:])
        xc = _out_projection(outs[:4], w_out_b, xc, mod, ctx_row, l, g_post, tm)
        xc = _mlp(xc, mod, ctx_row, l, g_mpre, g_mpost, w_up_b, w_down_b, tm, 1024)

        zl = _in_projection(xl, mod, lat_row, l, g_pre, w_in_p, tm)
        o_parts = _lat_mixers(zl, caches, l, p, tabs, lam_init)
        xl = _out_projection(o_parts, w_out_b, xl, mod, lat_row, l, g_post, tm)
        xl = _mlp(xl, mod, lat_row, l, g_mpre, g_mpost, w_up_b, w_down_b, tm, 1024)

    y_prompt = xc.reshape(BATCH, SEQ, D_MODEL)
    y_sample = xl.reshape(DEC_BATCH, DEC_SEQ, D_MODEL)
    return (y_prompt, y_sample) + states
```

```python
import functools
import math

import jax
import jax.numpy as jnp
from jax import lax
from jax.experimental import pallas as pl
from jax.experimental.pallas import tpu as pltpu

D_MODEL = 2048
BATCH = 32
SEQ = 256
DEPTH = 2
DEC_BATCH = 8
DEC_SEQ = 1024
PAST_LEN = 256
GRID_W = 64
HEAD_DIM = 128
WINDOW = 128
B_ROPE = 64
B_RANK = 256
D_HALF = 64
D_FF = 4 * D_MODEL
ROPE_BASE = 10000.0
EPS = 1e-6
MOD_CHUNKS = 6
MOD_ROWS = 16

GROUP = 512
Z_MAIN = 4608
Z_WIDTH = 4864
KEYS = PAST_LEN + DEC_SEQ

LANES = 128
VMEM_LIMIT = 56 * 1024 * 1024

BF = jnp.bfloat16
F32 = jnp.float32
NEG = -1e30
LOG2E = math.log2(math.e)


def _dot(a, b):
    return jnp.dot(a, b, preferred_element_type=F32)


def _dot_nt(a, b):
    return lax.dot_general(a, b, (((1,), (1,)), ((), ())), preferred_element_type=F32)


def _rms(x, g):
    return x * lax.rsqrt(jnp.mean(x * x, axis=-1, keepdims=True) + EPS) * g


def _params(sem):
    return pltpu.CompilerParams(dimension_semantics=sem, vmem_limit_bytes=VMEM_LIMIT)


def _with_ones(v):
    return jnp.concatenate([v, jnp.ones(v.shape, v.dtype)], axis=1)


def _attend(q, k, v1, sink2=None):
    s = _dot_nt(q, k)
    m = jnp.max(s, axis=-1, keepdims=True)
    if sink2 is not None:
        m = jnp.maximum(m, sink2)
    o = _dot(jnp.exp2(s - m).astype(BF), v1)
    l = o[:, HEAD_DIM:]
    if sink2 is not None:
        l = l + jnp.exp2(sink2 - m)
    return o[:, :HEAD_DIM] / l


def _rope(x, cos, sin_signed, half):
    lane = lax.broadcasted_iota(jnp.int32, (8, LANES), 1)
    from_up = pltpu.roll(lane, half, 1)[0:1, :] == (lane[0:1, :] ^ half)
    partner = jnp.where(from_up, pltpu.roll(x, half, 1), pltpu.roll(x, LANES - half, 1))
    return x * cos + partner * sin_signed


def _lam(lam_ref, lam_init):
    lp = lam_ref[...]
    a = jnp.sum(lp[0:1, :] * lp[1:2, :], axis=-1, keepdims=True)
    b = jnp.sum(lp[2:3, :] * lp[3:4, :], axis=-1, keepdims=True)
    return jnp.exp(a) - jnp.exp(b) + lam_init


def _mod_kernel(c_ref, w_ref, b_ref, o_ref):
    c = c_ref[...]
    s = c * jax.nn.sigmoid(c)
    o_ref[...] = _dot(s.astype(BF), w_ref[...].astype(BF)) + b_ref[...]


def _modulation(cvec, w_ada, b_ada):
    tn = 1024
    n = MOD_CHUNKS * D_MODEL
    return pl.pallas_call(
        _mod_kernel,
        grid=(DEPTH, n // tn),
        in_specs=[
            pl.BlockSpec((MOD_ROWS, D_MODEL), lambda l, j: (0, 0)),
            pl.BlockSpec((None, D_MODEL, tn), lambda l, j: (l, 0, j)),
            pl.BlockSpec((None, 1, tn), lambda l, j: (l, 0, j)),
        ],
        out_specs=pl.BlockSpec((None, MOD_ROWS, tn), lambda l, j: (l, 0, j)),
        out_shape=jax.ShapeDtypeStruct((DEPTH, MOD_ROWS, n), F32),
        compiler_params=_params(("arbitrary", "arbitrary")),
        name="modulation",
    )(cvec, w_ada, b_ada.reshape(DEPTH, 1, n))


SUB = 256


def _sub_tiles(rows, sub=SUB):
    return [slice(s, s + sub) for s in range(0, rows, sub)]


def _inproj_kernel(x_ref, mod_ref, g_ref, w_ref, z_ref):
    for rows in _sub_tiles(x_ref.shape[0]):
        h = _rms(x_ref[rows, :], g_ref[...]) * (1.0 + mod_ref[1:2, :]) + mod_ref[0:1, :]
        z_ref[rows, :] = _dot(h.astype(BF), w_ref[...])


def _in_projection(x, mod, mod_row, l, g, w, tm):
    m = x.shape[0]
    return pl.pallas_call(
        _inproj_kernel,
        grid=(m // tm,),
        in_specs=[
            pl.BlockSpec((tm, D_MODEL), lambda i: (i, 0)),
            pl.BlockSpec((None, None, MOD_CHUNKS, D_MODEL), lambda i: (l, mod_row(i), 0, 0)),
            pl.BlockSpec((1, D_MODEL), lambda i: (0, 0)),
            pl.BlockSpec((None, D_MODEL, Z_WIDTH), lambda i: (l, 0, 0), pipeline_mode=pl.Buffered(1)),
        ],
        out_specs=pl.BlockSpec((tm, Z_WIDTH), lambda i: (i, 0)),
        out_shape=jax.ShapeDtypeStruct((m, Z_WIDTH), F32),
        compiler_params=_params(("arbitrary",)),
        name="in_projection",
    )(x, mod, g, w)


def _outproj_kernel(oa_ref, ob_ref, oc_ref, od_ref, w_ref, x_ref, mod_ref, gpost_ref, gpre_ref, y_ref, h_ref):
    for rows in _sub_tiles(x_ref.shape[0], SUB // 2):
        acc = _dot(oa_ref[rows, :], w_ref[0:GROUP, :])
        acc += _dot(ob_ref[rows, :], w_ref[GROUP:2 * GROUP, :])
        acc += _dot(oc_ref[rows, :], w_ref[2 * GROUP:3 * GROUP, :])
        acc += _dot(od_ref[rows, :], w_ref[3 * GROUP:4 * GROUP, :])
        y = x_ref[rows, :] + mod_ref[2:3, :] * _rms(acc, gpost_ref[...])
        y_ref[rows, :] = y
        h_ref[rows, :] = (_rms(y, gpre_ref[...]) * (1.0 + mod_ref[4:5, :]) + mod_ref[3:4, :]).astype(BF)


def _out_projection(o_parts, w, x, mod, mod_row, l, gpost, gpre, tm):
    m = x.shape[0]
    o_spec = pl.BlockSpec((tm, GROUP), lambda i: (i, 0))
    row_spec = pl.BlockSpec((tm, D_MODEL), lambda i: (i, 0))
    return pl.pallas_call(
        _outproj_kernel,
        grid=(m // tm,),
        in_specs=[
            o_spec, o_spec, o_spec, o_spec,
            pl.BlockSpec((None, D_MODEL, D_MODEL), lambda i: (l, 0, 0)),
            row_spec,
            pl.BlockSpec((None, None, MOD_CHUNKS, D_MODEL), lambda i: (l, mod_row(i), 0, 0)),
            pl.BlockSpec((1, D_MODEL), lambda i: (0, 0)),
            pl.BlockSpec((1, D_MODEL), lambda i: (0, 0)),
        ],
        out_specs=[row_spec, row_spec],
        out_shape=[jax.ShapeDtypeStruct((m, D_MODEL), F32), jax.ShapeDtypeStruct((m, D_MODEL), BF)],
        compiler_params=_params(("arbitrary",)),
        name="out_projection",
    )(*o_parts, w, x, mod, gpost, gpre)


def _mlp_kernel(h_ref, x_ref, mod_ref, gpost_ref, wu_ref, wd_ref, y_ref):
    k = pl.program_id(1)
    last = pl.num_programs(1) - 1

    def ff(rows):
        u = jnp.square(jnp.maximum(_dot(h_ref[rows, :], wu_ref[...]), 0.0))
        return _dot(u.astype(BF), wd_ref[...])

    @pl.when(k == 0)
    def _():
        y_ref[...] = ff(slice(None))

    @pl.when((k > 0) & (k < last))
    def _():
        y_ref[...] += ff(slice(None))

    @pl.when(k == last)
    def _():
        for rows in _sub_tiles(x_ref.shape[0]):
            acc = y_ref[rows, :] + ff(rows)
            y_ref[rows, :] = x_ref[rows, :] + mod_ref[5:6, :] * _rms(acc, gpost_ref[...])


def _mlp(h, x, mod, mod_row, l, gpost, wu, wd, tm, tf):
    m = x.shape[0]
    assert D_FF // tf >= 2
    row_spec = pl.BlockSpec((tm, D_MODEL), lambda i, k: (i, 0))
    return pl.pallas_call(
        _mlp_kernel,
        grid=(m // tm, D_FF // tf),
        in_specs=[
            row_spec, row_spec,
            pl.BlockSpec((None, None, MOD_CHUNKS, D_MODEL), lambda i, k: (l, mod_row(i), 0, 0)),
            pl.BlockSpec((1, D_MODEL), lambda i, k: (0, 0)),
            pl.BlockSpec((None, D_MODEL, tf), lambda i, k: (l, 0, k)),
            pl.BlockSpec((None, tf, D_MODEL), lambda i, k: (l, k, 0)),
        ],
        out_specs=row_spec,
        out_shape=jax.ShapeDtypeStruct((m, D_MODEL), F32),
        compiler_params=_params(("arbitrary", "arbitrary")),
        name="mlp",
    )(h, x, mod, gpost, wu, wd)


N_STATES = 7
STATE_HEADS = (2, 2, None, 2, 2, 4, 4)


def _ctx_mixer_kernel(lam_init, n_prev, z_ref, sink_ref, gkv_ref, wuk_ref, wuv_ref, gq_ref, gk_ref, lam_ref, gout_ref,
                      *refs):
    prev = refs[:n_prev]
    oa_ref, ob_ref, oc_ref, od_ref = refs[n_prev:n_prev + 4]
    st = refs[n_prev + 4:]
    if n_prev:
        for p_ref, s_ref in zip(prev, st):
            s_ref[0] = p_ref[...]
        ka_ref, va_ref, mla_ref, kc_ref, vc_ref, kd_ref, vd_ref = [s.at[1] for s in st]
    else:
        ka_ref, va_ref, mla_ref, kc_ref, vc_ref, kd_ref, vd_ref = st
    hd = HEAD_DIM

    c = HEAD_DIM ** -0.5 * LOG2E
    for hk in range(2):
        k = z_ref[:, 512 + hk * hd:512 + (hk + 1) * hd]
        v = z_ref[:, 768 + hk * hd:768 + (hk + 1) * hd]
        ka_ref[hk] = k
        va_ref[hk] = v
        kb, v1 = k.astype(BF), _with_ones(v.astype(BF))
        for g in range(2):
            h = hk * 2 + g
            q = (z_ref[:, h * hd:(h + 1) * hd] * c).astype(BF)
            oa_ref[:, h * hd:(h + 1) * hd] = _attend(q, kb, v1, sink_ref[h] * LOG2E).astype(BF)

    c = (HEAD_DIM + B_ROPE) ** -0.5 * LOG2E
    c_lat = _rms(z_ref[:, 1792:2048], gkv_ref[...])
    kr = z_ref[:, Z_MAIN:Z_MAIN + B_ROPE]
    mla_ref[:, 0:B_RANK] = c_lat
    mla_ref[:, B_RANK:B_RANK + B_ROPE] = kr
    cb, krb = c_lat.astype(BF), kr.astype(BF)
    for h in range(4):
        kn = _dot(cb, wuk_ref[:, h * hd:(h + 1) * hd]).astype(BF)
        v1 = _with_ones(_dot(cb, wuv_ref[:, h * hd:(h + 1) * hd]).astype(BF))
        qn = (z_ref[:, 1024 + h * hd:1024 + (h + 1) * hd] * c).astype(BF)
        qr = (z_ref[:, 1536 + h * B_ROPE:1536 + (h + 1) * B_ROPE] * c).astype(BF)
        s = _dot_nt(qn, kn) + _dot_nt(qr, krb)
        m = jnp.max(s, axis=-1, keepdims=True)
        o = _dot(jnp.exp2(s - m).astype(BF), v1)
        ob_ref[:, h * hd:(h + 1) * hd] = (o[:, :hd] / o[:, hd:]).astype(BF)

    c = HEAD_DIM ** -0.5 * LOG2E
    for hk in range(2):
        k = _rms(z_ref[:, 2560 + hk * hd:2560 + (hk + 1) * hd], gk_ref[...])
        v = z_ref[:, 2816 + hk * hd:2816 + (hk + 1) * hd]
        kc_ref[hk] = k
        vc_ref[hk] = v
        kb, v1 = k.astype(BF), _with_ones(v.astype(BF))
        for g in range(2):
            h = hk * 2 + g
            q = (_rms(z_ref[:, 2048 + h * hd:2048 + (h + 1) * hd], gq_ref[...]) * c).astype(BF)
            oc_ref[:, h * hd:(h + 1) * hd] = _attend(q, kb, v1).astype(BF)

    c = D_HALF ** -0.5 * LOG2E
    lam = _lam(lam_ref, lam_init)
    lane = lax.broadcasted_iota(jnp.int32, (SEQ, hd), 1)
    for h in range(4):
        k = z_ref[:, 3584 + h * hd:3584 + (h + 1) * hd]
        v = z_ref[:, 4096 + h * hd:4096 + (h + 1) * hd]
        kd_ref[h] = k
        vd_ref[h] = v
        kb, v1 = k.astype(BF), _with_ones(v.astype(BF))
        q = z_ref[:, 3072 + h * hd:3072 + (h + 1) * hd] * c
        o1 = _attend(jnp.where(lane < D_HALF, q, 0.0).astype(BF), kb, v1)
        o2 = _attend(jnp.where(lane >= D_HALF, q, 0.0).astype(BF), kb, v1)
        od_ref[:, h * hd:(h + 1) * hd] = (_rms(o1 - lam * o2, gout_ref[...]) * (1.0 - lam_init)).astype(BF)


def _const(shape):
    return pl.BlockSpec(shape, lambda *_: (0,) * len(shape))


def _ctx_mixers(z, p, lam_init, prev_states):
    m = z.shape[0]
    o_spec = pl.BlockSpec((SEQ, GROUP), lambda b: (b, 0))
    o_shape = jax.ShapeDtypeStruct((m, GROUP), BF)
    n_prev = len(prev_states)
    lead = (DEPTH,) if n_prev else ()

    def st_dims(h):
        return (SEQ, B_RANK + B_ROPE) if h is None else (h, SEQ, HEAD_DIM)

    def st_spec(h, lead):
        dims = lead + st_dims(h)
        return pl.BlockSpec((None,) + dims, lambda b: (b,) + (0,) * len(dims))

    return pl.pallas_call(
        functools.partial(_ctx_mixer_kernel, lam_init, n_prev),
        grid=(BATCH,),
        in_specs=[
            pl.BlockSpec((SEQ, Z_WIDTH), lambda b: (b, 0)),
            pl.BlockSpec(memory_space=pltpu.SMEM),
            _const((1, B_RANK)), _const((B_RANK, GROUP)), _const((B_RANK, GROUP)),
            _const((1, HEAD_DIM)), _const((1, HEAD_DIM)), _const((4, D_HALF)), _const((1, HEAD_DIM)),
        ] + [st_spec(h, ()) for h in STATE_HEADS[:n_prev]],
        out_specs=[o_spec] * 4 + [st_spec(h, lead) for h in STATE_HEADS],
        out_shape=[o_shape] * 4 + [jax.ShapeDtypeStruct((BATCH,) + lead + st_dims(h), F32) for h in STATE_HEADS],
        compiler_params=_params(("arbitrary",)),
        name="ctx_mixers",
    )(z, p['a_sink'], p['b_g_kv'], p['w_uk'], p['w_uv'], p['c_gq'], p['c_gk'], p['d_lam'], p['d_g_out'],
      *prev_states)


QBLK = 256
N_QBLK = DEC_SEQ // QBLK
N_WBLK = DEC_SEQ // WINDOW


def _lat_a_kernel(z_ref, ck_ref, cv_ref, cos_ref, sin_ref, sink_ref, o_ref, q_s, k_s, v_s):
    hd = HEAD_DIM
    cos, sin = cos_ref[...], sin_ref[...]
    c = HEAD_DIM ** -0.5 * LOG2E
    for hk in range(2):
        k_s[hk, 0:WINDOW, :] = jnp.zeros((WINDOW, hd), BF)
        k_s[hk, WINDOW + DEC_SEQ:, :] = jnp.zeros((WINDOW, hd), BF)
        v_s[hk, 0:WINDOW, :] = jnp.zeros((WINDOW, 2 * hd), BF)
        v_s[hk, WINDOW + DEC_SEQ:, :] = jnp.zeros((WINDOW, 2 * hd), BF)
        k = _rope(z_ref[:, 512 + hk * hd:512 + (hk + 1) * hd], cos, sin, 32)
        k_s[hk, WINDOW:WINDOW + DEC_SEQ, :] = k.astype(BF)
        v_s[hk, WINDOW:WINDOW + DEC_SEQ, :] = _with_ones(z_ref[:, 768 + hk * hd:768 + (hk + 1) * hd].astype(BF))
        for g in range(2):
            h = hk * 2 + g
            q = (_rope(z_ref[:, h * hd:(h + 1) * hd], cos, sin, 32) * c).astype(BF)
            for i in range(N_WBLK):
                q_s[hk, i, g * WINDOW:(g + 1) * WINDOW, :] = q[i * WINDOW:(i + 1) * WINDOW, :]

    r_io = lax.broadcasted_iota(jnp.int32, (2 * WINDOW, 3 * WINDOW), 0) & (WINDOW - 1)
    c_io = lax.broadcasted_iota(jnp.int32, (2 * WINDOW, 3 * WINDOW), 1)
    second = lax.broadcasted_iota(jnp.int32, (2 * WINDOW, 1), 0) >= WINDOW
    for hk in range(2):
        kc = ck_ref[hk].astype(BF)
        vc1 = _with_ones(cv_ref[hk].astype(BF))
        sink2 = jnp.where(second, sink_ref[2 * hk + 1] * LOG2E, sink_ref[2 * hk] * LOG2E)
        for i in range(N_WBLK):
            r0 = i * WINDOW
            q = q_s[hk, i]
            kl = k_s[hk, r0:r0 + 3 * WINDOW, :]
            vl1 = v_s[hk, r0:r0 + 3 * WINDOW, :]
            kpos = (i - 1) * WINDOW + c_io
            qpos = i * WINDOW + r_io
            ok = (jnp.abs(qpos - kpos) <= WINDOW) & (kpos >= 0) & (kpos < DEC_SEQ)
            sc = _dot_nt(q, kc)
            sl = _dot_nt(q, kl)
            m = jnp.maximum(jnp.maximum(jnp.max(sc, axis=-1, keepdims=True),
                                        jnp.max(jnp.where(ok, sl, NEG), axis=-1, keepdims=True)), sink2)
            pw = jnp.where(ok, jnp.exp2(sl - m), 0.0)
            o = _dot(jnp.exp2(sc - m).astype(BF), vc1) + _dot(pw.astype(BF), vl1)
            o = (o[:, :hd] / (o[:, hd:] + jnp.exp2(sink2 - m))).astype(BF)
            o_ref[r0:r0 + WINDOW, 2 * hk * hd:(2 * hk + 1) * hd] = o[0:WINDOW, :]
            o_ref[r0:r0 + WINDOW, (2 * hk + 1) * hd:(2 * hk + 2) * hd] = o[WINDOW:, :]


def _lat_b_kernel(z_ref, zr_ref, mla_ref, gkv_ref, wuk_ref, wuv_ref, cos_ref, sin_ref, o_ref, c_s, kr_s, k_s, v_s, q_s):
    hd = HEAD_DIM
    cos, sin = cos_ref[...], sin_ref[...]
    c = (HEAD_DIM + B_ROPE) ** -0.5 * LOG2E
    c_s[0:PAST_LEN, :] = mla_ref[:, 0:B_RANK].astype(BF)
    c_s[PAST_LEN:, :] = _rms(z_ref[:, 768:1024], gkv_ref[...]).astype(BF)
    kr_s[0:PAST_LEN, B_ROPE:] = jnp.zeros((PAST_LEN, hd - B_ROPE), BF)
    kr_s[0:PAST_LEN, 0:B_ROPE] = mla_ref[:, B_RANK:B_RANK + B_ROPE].astype(BF)
    kr_s[PAST_LEN:, :] = _rope(zr_ref[...], cos, sin, 16).astype(BF)
    lane = lax.broadcasted_iota(jnp.int32, (DEC_SEQ, hd), 1)
    for h in range(4):
        k_s[h, :, 0:hd] = _dot(c_s[...], wuk_ref[:, h * hd:(h + 1) * hd]).astype(BF)
        k_s[h, :, hd:2 * hd] = kr_s[...]
        v_s[h] = _with_ones(_dot(c_s[...], wuv_ref[:, h * hd:(h + 1) * hd]).astype(BF))
        q_s[h, :, 0:hd] = (z_ref[:, h * hd:(h + 1) * hd] * c).astype(BF)
        qr = _rope(z_ref[:, 512 + (h // 2) * hd:512 + (h // 2 + 1) * hd], cos, sin, 16) * c
        if h % 2:
            qr = pltpu.roll(qr, B_ROPE, 1)
        q_s[h, :, hd:2 * hd] = jnp.where(lane < B_ROPE, qr, 0.0).astype(BF)
    for h in range(4):
        for i in range(N_QBLK):
            r0 = i * QBLK
            o = _attend(q_s[h, r0:r0 + QBLK, :], k_s[h], v_s[h])
            o_ref[r0:r0 + QBLK, h * hd:(h + 1) * hd] = o.astype(BF)


def _lat_c_kernel(z_ref, ck_ref, cv_ref, gq_ref, gk_ref, cos_ref, sin_ref, o_ref, k_s, v_s, q_s):
    hd = HEAD_DIM
    cos, sin = cos_ref[...], sin_ref[...]
    c = HEAD_DIM ** -0.5 * LOG2E
    for hk in range(2):
        k_s[hk, 0:PAST_LEN, :] = ck_ref[hk].astype(BF)
        k = _rms(z_ref[:, 512 + hk * hd:512 + (hk + 1) * hd], gk_ref[...])
        k_s[hk, PAST_LEN:, :] = _rope(k, cos, sin, 32).astype(BF)
        v_s[hk, 0:PAST_LEN, :] = _with_ones(cv_ref[hk].astype(BF))
        v_s[hk, PAST_LEN:, :] = _with_ones(z_ref[:, 768 + hk * hd:768 + (hk + 1) * hd].astype(BF))
    for h in range(4):
        q = _rms(z_ref[:, h * hd:(h + 1) * hd], gq_ref[...])
        q_s[h] = (_rope(q, cos, sin, 32) * c).astype(BF)
    for h in range(4):
        for i in range(N_QBLK):
            r0 = i * QBLK
            o = _attend(q_s[h, r0:r0 + QBLK, :], k_s[h // 2], v_s[h // 2])
            o_ref[r0:r0 + QBLK, h * hd:(h + 1) * hd] = o.astype(BF)


def _lat_d_kernel(lam_init, z_ref, dk_ref, dv_ref, lam_ref, gout_ref, cos_ref, sin_ref, o_ref, k_s, v_s, q1_s, q2_s):
    hd = HEAD_DIM
    cos, sin = cos_ref[...], sin_ref[...]
    c = D_HALF ** -0.5 * LOG2E
    lam = _lam(lam_ref, lam_init)
    lane = lax.broadcasted_iota(jnp.int32, (DEC_SEQ, hd), 1)
    for h in range(4):
        k_s[h, 0:PAST_LEN, :] = dk_ref[h].astype(BF)
        k_s[h, PAST_LEN:, :] = _rope(z_ref[:, 512 + h * hd:512 + (h + 1) * hd], cos, sin, 16).astype(BF)
        v_s[h, 0:PAST_LEN, :] = _with_ones(dv_ref[h].astype(BF))
        v_s[h, PAST_LEN:, :] = _with_ones(z_ref[:, 1024 + h * hd:1024 + (h + 1) * hd].astype(BF))
        q = _rope(z_ref[:, h * hd:(h + 1) * hd], cos, sin, 16) * c
        q1_s[h] = jnp.where(lane < D_HALF, q, 0.0).astype(BF)
        q2_s[h] = jnp.where(lane >= D_HALF, q, 0.0).astype(BF)
    for h in range(4):
        for i in range(N_QBLK):
            r0 = i * QBLK
            o1 = _attend(q1_s[h, r0:r0 + QBLK, :], k_s[h], v_s[h])
            o2 = _attend(q2_s[h, r0:r0 + QBLK, :], k_s[h], v_s[h])
            o = _rms(o1 - lam * o2, gout_ref[...]) * (1.0 - lam_init)
            o_ref[r0:r0 + QBLK, h * hd:(h + 1) * hd] = o.astype(BF)


def _lat_mixers(z, caches, l, p, tabs, lam_init):
    m = z.shape[0]
    hd = HEAD_DIM
    cos128, sin128, cos64, sin64 = tabs
    o_spec = pl.BlockSpec((DEC_SEQ, GROUP), lambda b: (b, 0))
    o_shape = jax.ShapeDtypeStruct((m, GROUP), BF)
    tab = _const((DEC_SEQ, hd))
    cp = _params(("arbitrary",))

    def cache_spec(h):
        return pl.BlockSpec((None, None, h, PAST_LEN, hd), lambda b: (b, l, 0, 0, 0))

    def zspec(width, col0):
        return pl.BlockSpec((DEC_SEQ, width), lambda b: (b, col0 // width))

    def scratch(*shape):
        return pltpu.VMEM(shape, BF)

    o_a = pl.pallas_call(
        _lat_a_kernel, grid=(DEC_BATCH,),
        in_specs=[zspec(1024, 0), cache_spec(2), cache_spec(2), tab, tab, pl.BlockSpec(memory_space=pltpu.SMEM)],
        out_specs=o_spec, out_shape=o_shape,
        scratch_shapes=[scratch(2, N_WBLK, 2 * WINDOW, hd), scratch(2, DEC_SEQ + 2 * WINDOW, hd),
                        scratch(2, DEC_SEQ + 2 * WINDOW, 2 * hd)],
        compiler_params=cp, name="lat_mixer_a",
    )(z, caches['a_k'], caches['a_v'], cos128, sin128, p['a_sink'])

    o_b = pl.pallas_call(
        _lat_b_kernel, grid=(DEC_BATCH,),
        in_specs=[zspec(1024, 1024), zspec(LANES, Z_MAIN),
                  pl.BlockSpec((None, None, PAST_LEN, B_RANK + B_ROPE), lambda b: (b, l, 0, 0)),
                  _const((1, B_RANK)), _const((B_RANK, GROUP)), _const((B_RANK, GROUP)), tab, tab],
        out_specs=o_spec, out_shape=o_shape,
        scratch_shapes=[scratch(KEYS, B_RANK), scratch(KEYS, hd), scratch(4, KEYS, 2 * hd), scratch(4, KEYS, 2 * hd),
                        scratch(4, DEC_SEQ, 2 * hd)],
        compiler_params=cp, name="lat_mixer_b",
    )(z, z, caches['mla'], p['b_g_kv'], p['w_uk'], p['w_uv'], cos64, sin64)

    o_c = pl.pallas_call(
        _lat_c_kernel, grid=(DEC_BATCH,),
        in_specs=[zspec(1024, 2048), cache_spec(2), cache_spec(2), _const((1, hd)), _const((1, hd)), tab, tab],
        out_specs=o_spec, out_shape=o_shape,
        scratch_shapes=[scratch(2, KEYS, hd), scratch(2, KEYS, 2 * hd), scratch(4, DEC_SEQ, hd)],
        compiler_params=cp, name="lat_mixer_c",
    )(z, caches['c_k'], caches['c_v'], p['c_gq'], p['c_gk'], cos128, sin128)

    o_d = pl.pallas_call(
        functools.partial(_lat_d_kernel, lam_init), grid=(DEC_BATCH,),
        in_specs=[zspec(1536, 3072), cache_spec(4), cache_spec(4), _const((4, D_HALF)), _const((1, hd)), tab, tab],
        out_specs=o_spec, out_shape=o_shape,
        scratch_shapes=[scratch(4, KEYS, hd), scratch(4, KEYS, 2 * hd), scratch(4, DEC_SEQ, hd),
                        scratch(4, DEC_SEQ, hd)],
        compiler_params=cp, name="lat_mixer_d",
    )(z, caches['d_k'], caches['d_v'], p['d_lam'], p['d_g_out'], cos64, sin64)

    return o_a, o_b, o_c, o_d


def _rope_tables():
    t = jnp.arange(DEC_SEQ)
    row = (t // GRID_W).astype(F32)[:, None]
    col = (t % GRID_W).astype(F32)[:, None]

    def table(half):
        inv = ROPE_BASE ** (-jnp.arange(half, dtype=F32) / half)
        ar, ac = row * inv[None, :], col * inv[None, :]
        cos = jnp.concatenate([jnp.cos(ar), jnp.cos(ar), jnp.cos(ac), jnp.cos(ac)], axis=-1)
        sin = jnp.concatenate([-jnp.sin(ar), jnp.sin(ar), -jnp.sin(ac), jnp.sin(ac)], axis=-1)
        return cos, sin

    cos128, sin128 = table(32)
    cos64, sin64 = table(16)
    return cos128, sin128, jnp.tile(cos64, (1, 2)), jnp.tile(sin64, (1, 2))


def kernel(x_prompt, x_sample, cache_a_k, cache_a_v, cache_mla, cache_c_k, cache_c_v, cache_d_k, cache_d_v,
           c, c_ctx, w_ada, b_ada, g_attn_pre, g_attn_post, g_mlp_pre, g_mlp_post, w_in, a_sink,
           b_g_kv, b_w_uk, b_w_uv, c_gq, c_gk, d_lam, d_g_out, w_out, w_up, w_down):
    assert DEPTH == 2
    w_in_b = w_in.astype(BF)
    w_in_p = jnp.concatenate(
        [w_in_b[:, :, :2048], w_in_b[:, :, 2112:], w_in_b[:, :, 2048:2112],
         jnp.zeros((DEPTH, D_MODEL, Z_WIDTH - 4672), BF)], axis=-1)
    w_out_b = w_out.astype(BF)
    w_up_b = w_up.astype(BF)
    w_down_b = w_down.astype(BF)
    w_uk_b = b_w_uk.reshape(DEPTH, B_RANK, GROUP).astype(BF)
    w_uv_b = b_w_uv.reshape(DEPTH, B_RANK, GROUP).astype(BF)

    cvec = jnp.concatenate([c_ctx[None, :], c, jnp.zeros((MOD_ROWS - 1 - DEC_BATCH, D_MODEL), F32)], axis=0)
    mod = _modulation(cvec, w_ada, b_ada).reshape(DEPTH, MOD_ROWS, MOD_CHUNKS, D_MODEL)
    tabs = _rope_tables()

    tm = 512
    tm_mlp, tf_mlp = 1024, 512
    ctx_row = lambda tm: (lambda i: 0)
    lat_row = lambda tm: (lambda i: 1 + (i * tm) // DEC_SEQ)
    caches = {'a_k': cache_a_k, 'a_v': cache_a_v, 'mla': cache_mla, 'c_k': cache_c_k, 'c_v': cache_c_v,
              'd_k': cache_d_k, 'd_v': cache_d_v}

    xc = x_prompt.reshape(BATCH * SEQ, D_MODEL)
    xl = x_sample.reshape(DEC_BATCH * DEC_SEQ, D_MODEL)
    states = ()
    for l in range(DEPTH):
        lam_init = 0.8 - 0.6 * math.exp(-0.3 * l)
        p = {'a_sink': a_sink[l], 'b_g_kv': b_g_kv[l][None], 'w_uk': w_uk_b[l], 'w_uv': w_uv_b[l],
             'c_gq': c_gq[l][None], 'c_gk': c_gk[l][None], 'd_lam': d_lam[l], 'd_g_out': d_g_out[l][None]}
        g_pre, g_post = g_attn_pre[l][None], g_attn_post[l][None]
        g_mpre, g_mpost = g_mlp_pre[l][None], g_mlp_post[l][None]

        zc = _in_projection(xc, mod, ctx_row(tm), l, g_pre, w_in_p, tm)
        outs = _ctx_mixers(zc, p, lam_init, states)
        states = tuple(outs[4:])
        xc, hc = _out_projection(outs[:4], w_out_b, xc, mod, ctx_row(tm), l, g_post, g_mpre, tm)
        xc = _mlp(hc, xc, mod, ctx_row(tm_mlp), l, g_mpost, w_up_b, w_down_b, tm_mlp, tf_mlp)

        zl = _in_projection(xl, mod, lat_row(tm), l, g_pre, w_in_p, tm)
        o_parts = _lat_mixers(zl, caches, l, p, tabs, lam_init)
        xl, hl = _out_projection(o_parts, w_out_b, xl, mod, lat_row(tm), l, g_post, g_mpre, tm)
        xl = _mlp(hl, xl, mod, lat_row(tm_mlp), l, g_mpost, w_up_b, w_down_b, tm_mlp, tf_mlp)

    y_prompt = xc.reshape(BATCH, SEQ, D_MODEL)
    y_sample = xl.reshape(DEC_BATCH, DEC_SEQ, D_MODEL)
    return (y_prompt, y_sample) + states
```

```python
import functools
import math

import jax
import jax.numpy as jnp
from jax import lax
from jax.experimental import pallas as pl
from jax.experimental.pallas import tpu as pltpu

D_MODEL = 2048
BATCH = 32
SEQ = 256
DEPTH = 2
DEC_BATCH = 8
DEC_SEQ = 1024
PAST_LEN = 256
GRID_W = 64
HEAD_DIM = 128
WINDOW = 128
B_ROPE = 64
B_RANK = 256
D_HALF = 64
D_FF = 4 * D_MODEL
ROPE_BASE = 10000.0
EPS = 1e-6
MOD_CHUNKS = 6
MOD_ROWS = 16

GROUP = 512
Z_MAIN = 4608
Z_WIDTH = 4864
KEYS = PAST_LEN + DEC_SEQ

LANES = 128
VMEM_LIMIT = 56 * 1024 * 1024

BF = jnp.bfloat16
F32 = jnp.float32
NEG = -1e30
LOG2E = math.log2(math.e)


def _dot(a, b):
    return jnp.dot(a, b, preferred_element_type=F32)


def _dot_nt(a, b):
    return lax.dot_general(a, b, (((1,), (1,)), ((), ())), preferred_element_type=F32)


def _rms(x, g):
    return x * lax.rsqrt(jnp.mean(x * x, axis=-1, keepdims=True) + EPS) * g


def _params(sem):
    return pltpu.CompilerParams(dimension_semantics=sem, vmem_limit_bytes=VMEM_LIMIT)


def _with_ones(v):
    return jnp.concatenate([v, jnp.ones(v.shape, v.dtype)], axis=1)


def _attend(q, k, v1, sink2=None):
    s = _dot_nt(q, k)
    m = jnp.max(s, axis=-1, keepdims=True)
    if sink2 is not None:
        m = jnp.maximum(m, sink2)
    o = _dot(jnp.exp2(s - m).astype(BF), v1)
    l = o[:, HEAD_DIM:]
    if sink2 is not None:
        l = l + jnp.exp2(sink2 - m)
    return o[:, :HEAD_DIM] / l


def _rope(x, cos, sin_signed, half):
    lane = lax.broadcasted_iota(jnp.int32, (8, LANES), 1)
    from_up = pltpu.roll(lane, half, 1)[0:1, :] == (lane[0:1, :] ^ half)
    partner = jnp.where(from_up, pltpu.roll(x, half, 1), pltpu.roll(x, LANES - half, 1))
    return x * cos + partner * sin_signed


def _lam(lam_ref, lam_init):
    lp = lam_ref[...]
    a = jnp.sum(lp[0:1, :] * lp[1:2, :], axis=-1, keepdims=True)
    b = jnp.sum(lp[2:3, :] * lp[3:4, :], axis=-1, keepdims=True)
    return jnp.exp(a) - jnp.exp(b) + lam_init


def _mod_kernel(c_ref, w_ref, b_ref, o_ref):
    c = c_ref[...]
    s = c * jax.nn.sigmoid(c)
    o_ref[...] = _dot(s.astype(BF), w_ref[...].astype(BF)) + b_ref[...]


def _modulation(cvec, w_ada, b_ada):
    tn = 1024
    n = MOD_CHUNKS * D_MODEL
    return pl.pallas_call(
        _mod_kernel,
        grid=(DEPTH, n // tn),
        in_specs=[
            pl.BlockSpec((MOD_ROWS, D_MODEL), lambda l, j: (0, 0)),
            pl.BlockSpec((None, D_MODEL, tn), lambda l, j: (l, 0, j)),
            pl.BlockSpec((None, 1, tn), lambda l, j: (l, 0, j)),
        ],
        out_specs=pl.BlockSpec((None, MOD_ROWS, tn), lambda l, j: (l, 0, j)),
        out_shape=jax.ShapeDtypeStruct((DEPTH, MOD_ROWS, n), F32),
        compiler_params=_params(("arbitrary", "arbitrary")),
        name="modulation",
    )(cvec, w_ada, b_ada.reshape(DEPTH, 1, n))


W_IN_COLS = 4672
KR_COL = 2048


def _regroup_kernel(w_ref, o_ref):
    o_ref[:, 0:KR_COL] = w_ref[:, 0:KR_COL].astype(BF)
    o_ref[:, KR_COL:Z_MAIN] = w_ref[:, KR_COL + B_ROPE:W_IN_COLS].astype(BF)
    o_ref[:, Z_MAIN:Z_MAIN + B_ROPE] = w_ref[:, KR_COL:KR_COL + B_ROPE].astype(BF)
    o_ref[:, Z_MAIN + B_ROPE:] = jnp.zeros((w_ref.shape[0], Z_WIDTH - Z_MAIN - B_ROPE), BF)


def _regroup_w_in(w_in):
    tr = 256
    return pl.pallas_call(
        _regroup_kernel,
        grid=(DEPTH, D_MODEL // tr),
        in_specs=[pl.BlockSpec((None, tr, W_IN_COLS), lambda l, i: (l, i, 0))],
        out_specs=pl.BlockSpec((None, tr, Z_WIDTH), lambda l, i: (l, i, 0)),
        out_shape=jax.ShapeDtypeStruct((DEPTH, D_MODEL, Z_WIDTH), BF),
        compiler_params=_params(("arbitrary", "arbitrary")),
        name="regroup_w_in",
    )(w_in)


def _also_cast(kernel, n_in):
    def wrapped(*refs):
        refs[n_in + 2][...] = refs[n_in][...].astype(BF)
        kernel(*refs[:n_in], refs[n_in + 1], *refs[n_in + 3:])
    return wrapped


SUB = 256


def _sub_tiles(rows, sub=SUB):
    return [slice(s, s + sub) for s in range(0, rows, sub)]


def _inproj_kernel(x_ref, mod_ref, g_ref, w_ref, z_ref):
    for rows in _sub_tiles(x_ref.shape[0]):
        h = _rms(x_ref[rows, :], g_ref[...]) * (1.0 + mod_ref[1:2, :]) + mod_ref[0:1, :]
        z_ref[rows, :] = _dot(h.astype(BF), w_ref[...])


def _in_projection(x, mod, mod_row, l, g, w, tm):
    m = x.shape[0]
    return pl.pallas_call(
        _inproj_kernel,
        grid=(m // tm,),
        in_specs=[
            pl.BlockSpec((tm, D_MODEL), lambda i: (i, 0)),
            pl.BlockSpec((None, None, MOD_CHUNKS, D_MODEL), lambda i: (l, mod_row(i), 0, 0)),
            pl.BlockSpec((1, D_MODEL), lambda i: (0, 0)),
            pl.BlockSpec((None, D_MODEL, Z_WIDTH), lambda i: (l, 0, 0), pipeline_mode=pl.Buffered(1)),
        ],
        out_specs=pl.BlockSpec((tm, Z_WIDTH), lambda i: (i, 0)),
        out_shape=jax.ShapeDtypeStruct((m, Z_WIDTH), F32),
        compiler_params=_params(("arbitrary",)),
        name="in_projection",
    )(x, mod, g, w)


def _outproj_kernel(oa_ref, ob_ref, oc_ref, od_ref, w_ref, x_ref, mod_ref, gpost_ref, gpre_ref, y_ref, h_ref):
    for rows in _sub_tiles(x_ref.shape[0], SUB // 2):
        o = jnp.concatenate([oa_ref[rows, :], ob_ref[rows, :], oc_ref[rows, :], od_ref[rows, :]], axis=1)
        y = x_ref[rows, :] + mod_ref[2:3, :] * _rms(_dot(o, w_ref[...]), gpost_ref[...])
        y_ref[rows, :] = y
        h_ref[rows, :] = (_rms(y, gpre_ref[...]) * (1.0 + mod_ref[4:5, :]) + mod_ref[3:4, :]).astype(BF)


def _out_projection(o_parts, w, x, mod, mod_row, l, gpost, gpre, tm):
    m = x.shape[0]
    o_spec = pl.BlockSpec((tm, GROUP), lambda i: (i, 0))
    row_spec = pl.BlockSpec((tm, D_MODEL), lambda i: (i, 0))
    return pl.pallas_call(
        _outproj_kernel,
        grid=(m // tm,),
        in_specs=[
            o_spec, o_spec, o_spec, o_spec,
            pl.BlockSpec((D_MODEL, D_MODEL), lambda i: (0, 0)),
            row_spec,
            pl.BlockSpec((None, None, MOD_CHUNKS, D_MODEL), lambda i: (l, mod_row(i), 0, 0)),
            pl.BlockSpec((1, D_MODEL), lambda i: (0, 0)),
            pl.BlockSpec((1, D_MODEL), lambda i: (0, 0)),
        ],
        out_specs=[row_spec, row_spec],
        out_shape=[jax.ShapeDtypeStruct((m, D_MODEL), F32), jax.ShapeDtypeStruct((m, D_MODEL), BF)],
        compiler_params=_params(("arbitrary",)),
        name="out_projection",
    )(*o_parts, w, x, mod, gpost, gpre)


def _mlp_kernel(h_ref, x_ref, mod_ref, gpost_ref, wu_ref, wd_ref, y_ref):
    k = pl.program_id(1)
    last = pl.num_programs(1) - 1

    def ff(rows):
        u = jnp.square(jnp.maximum(_dot(h_ref[rows, :], wu_ref[...]), 0.0))
        return _dot(u.astype(BF), wd_ref[...])

    @pl.when(k == 0)
    def _():
        y_ref[...] = ff(slice(None))

    @pl.when((k > 0) & (k < last))
    def _():
        y_ref[...] += ff(slice(None))

    @pl.when(k == last)
    def _():
        for rows in _sub_tiles(x_ref.shape[0]):
            acc = y_ref[rows, :] + ff(rows)
            y_ref[rows, :] = x_ref[rows, :] + mod_ref[5:6, :] * _rms(acc, gpost_ref[...])


def _mlp(h, x, mod, mod_row, l, gpost, wu, wd, tm, tf):
    m = x.shape[0]
    assert D_FF // tf >= 2
    row_spec = pl.BlockSpec((tm, D_MODEL), lambda i, k: (i, 0))
    return pl.pallas_call(
        _mlp_kernel,
        grid=(m // tm, D_FF // tf),
        in_specs=[
            row_spec, row_spec,
            pl.BlockSpec((None, None, MOD_CHUNKS, D_MODEL), lambda i, k: (l, mod_row(i), 0, 0)),
            pl.BlockSpec((1, D_MODEL), lambda i, k: (0, 0)),
            pl.BlockSpec((D_MODEL, tf), lambda i, k: (0, k)),
            pl.BlockSpec((tf, D_MODEL), lambda i, k: (k, 0)),
        ],
        out_specs=row_spec,
        out_shape=jax.ShapeDtypeStruct((m, D_MODEL), F32),
        compiler_params=_params(("arbitrary", "arbitrary")),
        name="mlp",
    )(h, x, mod, gpost, wu, wd)


N_STATES = 7
STATE_HEADS = (2, 2, None, 2, 2, 4, 4)


def _ctx_mixer_kernel(lam_init, n_prev, z_ref, sink_ref, gkv_ref, wuk_ref, wuv_ref, gq_ref, gk_ref, lam_ref, gout_ref,
                      *refs):
    prev = refs[:n_prev]
    oa_ref, ob_ref, oc_ref, od_ref = refs[n_prev:n_prev + 4]
    st = refs[n_prev + 4:]
    if n_prev:
        for p_ref, s_ref in zip(prev, st):
            s_ref[0] = p_ref[...]
        ka_ref, va_ref, mla_ref, kc_ref, vc_ref, kd_ref, vd_ref = [s.at[1] for s in st]
    else:
        ka_ref, va_ref, mla_ref, kc_ref, vc_ref, kd_ref, vd_ref = st
    hd = HEAD_DIM

    c = HEAD_DIM ** -0.5 * LOG2E
    for hk in range(2):
        k = z_ref[:, 512 + hk * hd:512 + (hk + 1) * hd]
        v = z_ref[:, 768 + hk * hd:768 + (hk + 1) * hd]
        ka_ref[hk] = k
        va_ref[hk] = v
        kb, v1 = k.astype(BF), _with_ones(v.astype(BF))
        for g in range(2):
            h = hk * 2 + g
            q = (z_ref[:, h * hd:(h + 1) * hd] * c).astype(BF)
            oa_ref[:, h * hd:(h + 1) * hd] = _attend(q, kb, v1, sink_ref[h] * LOG2E).astype(BF)

    c = (HEAD_DIM + B_ROPE) ** -0.5 * LOG2E
    c_lat = _rms(z_ref[:, 1792:2048], gkv_ref[...])
    kr = z_ref[:, Z_MAIN:Z_MAIN + B_ROPE]
    mla_ref[:, 0:B_RANK] = c_lat
    mla_ref[:, B_RANK:B_RANK + B_ROPE] = kr
    cb, krb = c_lat.astype(BF), kr.astype(BF)
    for h in range(4):
        kn = _dot(cb, wuk_ref[:, h * hd:(h + 1) * hd]).astype(BF)
        v1 = _with_ones(_dot(cb, wuv_ref[:, h * hd:(h + 1) * hd]).astype(BF))
        qn = (z_ref[:, 1024 + h * hd:1024 + (h + 1) * hd] * c).astype(BF)
        qr = (z_ref[:, 1536 + h * B_ROPE:1536 + (h + 1) * B_ROPE] * c).astype(BF)
        s = _dot_nt(qn, kn) + _dot_nt(qr, krb)
        m = jnp.max(s, axis=-1, keepdims=True)
        o = _dot(jnp.exp2(s - m).astype(BF), v1)
        ob_ref[:, h * hd:(h + 1) * hd] = (o[:, :hd] / o[:, hd:]).astype(BF)

    c = HEAD_DIM ** -0.5 * LOG2E
    for hk in range(2):
        k = _rms(z_ref[:, 2560 + hk * hd:2560 + (hk + 1) * hd], gk_ref[...])
        v = z_ref[:, 2816 + hk * hd:2816 + (hk + 1) * hd]
        kc_ref[hk] = k
        vc_ref[hk] = v
        kb, v1 = k.astype(BF), _with_ones(v.astype(BF))
        for g in range(2):
            h = hk * 2 + g
            q = (_rms(z_ref[:, 2048 + h * hd:2048 + (h + 1) * hd], gq_ref[...]) * c).astype(BF)
            oc_ref[:, h * hd:(h + 1) * hd] = _attend(q, kb, v1).astype(BF)

    c = D_HALF ** -0.5 * LOG2E
    lam = _lam(lam_ref, lam_init)
    lane = lax.broadcasted_iota(jnp.int32, (SEQ, hd), 1)
    for h in range(4):
        k = z_ref[:, 3584 + h * hd:3584 + (h + 1) * hd]
        v = z_ref[:, 4096 + h * hd:4096 + (h + 1) * hd]
        kd_ref[h] = k
        vd_ref[h] = v
        kb, v1 = k.astype(BF), _with_ones(v.astype(BF))
        q = z_ref[:, 3072 + h * hd:3072 + (h + 1) * hd] * c
        o1 = _attend(jnp.where(lane < D_HALF, q, 0.0).astype(BF), kb, v1)
        o2 = _attend(jnp.where(lane >= D_HALF, q, 0.0).astype(BF), kb, v1)
        od_ref[:, h * hd:(h + 1) * hd] = (_rms(o1 - lam * o2, gout_ref[...]) * (1.0 - lam_init)).astype(BF)


def _const(shape):
    return pl.BlockSpec(shape, lambda *_: (0,) * len(shape))


def _ctx_mixers(z, p, lam_init, prev_states):
    m = z.shape[0]
    o_spec = pl.BlockSpec((SEQ, GROUP), lambda b: (b, 0))
    o_shape = jax.ShapeDtypeStruct((m, GROUP), BF)
    n_prev = len(prev_states)
    lead = (DEPTH,) if n_prev else ()

    def st_dims(h):
        return (SEQ, B_RANK + B_ROPE) if h is None else (h, SEQ, HEAD_DIM)

    def st_spec(h, lead):
        dims = lead + st_dims(h)
        return pl.BlockSpec((None,) + dims, lambda b: (b,) + (0,) * len(dims))

    return pl.pallas_call(
        functools.partial(_ctx_mixer_kernel, lam_init, n_prev),
        grid=(BATCH,),
        in_specs=[
            pl.BlockSpec((SEQ, Z_WIDTH), lambda b: (b, 0)),
            pl.BlockSpec(memory_space=pltpu.SMEM),
            _const((1, B_RANK)), _const((B_RANK, GROUP)), _const((B_RANK, GROUP)),
            _const((1, HEAD_DIM)), _const((1, HEAD_DIM)), _const((4, D_HALF)), _const((1, HEAD_DIM)),
        ] + [st_spec(h, ()) for h in STATE_HEADS[:n_prev]],
        out_specs=[o_spec] * 4 + [st_spec(h, lead) for h in STATE_HEADS],
        out_shape=[o_shape] * 4 + [jax.ShapeDtypeStruct((BATCH,) + lead + st_dims(h), F32) for h in STATE_HEADS],
        compiler_params=_params(("arbitrary",)),
        name="ctx_mixers",
    )(z, p['a_sink'], p['b_g_kv'], p['w_uk'], p['w_uv'], p['c_gq'], p['c_gk'], p['d_lam'], p['d_g_out'],
      *prev_states)


QBLK = 256
N_QBLK = DEC_SEQ // QBLK
N_WBLK = DEC_SEQ // WINDOW


def _lat_a_kernel(z_ref, ck_ref, cv_ref, cos_ref, sin_ref, sink_ref, o_ref, q_s, k_s, v_s):
    hd = HEAD_DIM
    cos, sin = cos_ref[...], sin_ref[...]
    c = HEAD_DIM ** -0.5 * LOG2E
    for hk in range(2):
        k_s[hk, 0:WINDOW, :] = jnp.zeros((WINDOW, hd), BF)
        k_s[hk, WINDOW + DEC_SEQ:, :] = jnp.zeros((WINDOW, hd), BF)
        v_s[hk, 0:WINDOW, :] = jnp.zeros((WINDOW, 2 * hd), BF)
        v_s[hk, WINDOW + DEC_SEQ:, :] = jnp.zeros((WINDOW, 2 * hd), BF)
        k = _rope(z_ref[:, 512 + hk * hd:512 + (hk + 1) * hd], cos, sin, 32)
        k_s[hk, WINDOW:WINDOW + DEC_SEQ, :] = k.astype(BF)
        v_s[hk, WINDOW:WINDOW + DEC_SEQ, :] = _with_ones(z_ref[:, 768 + hk * hd:768 + (hk + 1) * hd].astype(BF))
        for g in range(2):
            h = hk * 2 + g
            q = (_rope(z_ref[:, h * hd:(h + 1) * hd], cos, sin, 32) * c).astype(BF)
            for i in range(N_WBLK):
                q_s[hk, i, g * WINDOW:(g + 1) * WINDOW, :] = q[i * WINDOW:(i + 1) * WINDOW, :]

    r_io = lax.broadcasted_iota(jnp.int32, (2 * WINDOW, 3 * WINDOW), 0) & (WINDOW - 1)
    c_io = lax.broadcasted_iota(jnp.int32, (2 * WINDOW, 3 * WINDOW), 1)
    second = lax.broadcasted_iota(jnp.int32, (2 * WINDOW, 1), 0) >= WINDOW
    for hk in range(2):
        kc = ck_ref[hk].astype(BF)
        vc1 = _with_ones(cv_ref[hk].astype(BF))
        sink2 = jnp.where(second, sink_ref[2 * hk + 1] * LOG2E, sink_ref[2 * hk] * LOG2E)
        for i in range(N_WBLK):
            r0 = i * WINDOW
            q = q_s[hk, i]
            kl = k_s[hk, r0:r0 + 3 * WINDOW, :]
            vl1 = v_s[hk, r0:r0 + 3 * WINDOW, :]
            kpos = (i - 1) * WINDOW + c_io
            qpos = i * WINDOW + r_io
            ok = (jnp.abs(qpos - kpos) <= WINDOW) & (kpos >= 0) & (kpos < DEC_SEQ)
            sc = _dot_nt(q, kc)
            sl = _dot_nt(q, kl)
            m = jnp.maximum(jnp.maximum(jnp.max(sc, axis=-1, keepdims=True),
                                        jnp.max(jnp.where(ok, sl, NEG), axis=-1, keepdims=True)), sink2)
            pw = jnp.where(ok, jnp.exp2(sl - m), 0.0)
            o = _dot(jnp.exp2(sc - m).astype(BF), vc1) + _dot(pw.astype(BF), vl1)
            o = (o[:, :hd] / (o[:, hd:] + jnp.exp2(sink2 - m))).astype(BF)
            o_ref[r0:r0 + WINDOW, 2 * hk * hd:(2 * hk + 1) * hd] = o[0:WINDOW, :]
            o_ref[r0:r0 + WINDOW, (2 * hk + 1) * hd:(2 * hk + 2) * hd] = o[WINDOW:, :]


def _lat_b_kernel(z_ref, zr_ref, mla_ref, gkv_ref, wuk_ref, wuv_ref, cos_ref, sin_ref, o_ref, c_s, kr_s, k_s, v_s, q_s):
    hd = HEAD_DIM
    cos, sin = cos_ref[...], sin_ref[...]
    c = (HEAD_DIM + B_ROPE) ** -0.5 * LOG2E
    c_s[0:PAST_LEN, :] = mla_ref[:, 0:B_RANK].astype(BF)
    c_s[PAST_LEN:, :] = _rms(z_ref[:, 768:1024], gkv_ref[...]).astype(BF)
    kr_s[0:PAST_LEN, B_ROPE:] = jnp.zeros((PAST_LEN, hd - B_ROPE), BF)
    kr_s[0:PAST_LEN, 0:B_ROPE] = mla_ref[:, B_RANK:B_RANK + B_ROPE].astype(BF)
    kr_s[PAST_LEN:, :] = _rope(zr_ref[...], cos, sin, 16).astype(BF)
    lane = lax.broadcasted_iota(jnp.int32, (DEC_SEQ, hd), 1)
    for h in range(4):
        k_s[h, :, 0:hd] = _dot(c_s[...], wuk_ref[:, h * hd:(h + 1) * hd]).astype(BF)
        k_s[h, :, hd:2 * hd] = kr_s[...]
        v_s[h] = _with_ones(_dot(c_s[...], wuv_ref[:, h * hd:(h + 1) * hd]).astype(BF))
        q_s[h, :, 0:hd] = (z_ref[:, h * hd:(h + 1) * hd] * c).astype(BF)
        qr = _rope(z_ref[:, 512 + (h // 2) * hd:512 + (h // 2 + 1) * hd], cos, sin, 16) * c
        if h % 2:
            qr = pltpu.roll(qr, B_ROPE, 1)
        q_s[h, :, hd:2 * hd] = jnp.where(lane < B_ROPE, qr, 0.0).astype(BF)
    for h in range(4):
        for i in range(N_QBLK):
            r0 = i * QBLK
            o = _attend(q_s[h, r0:r0 + QBLK, :], k_s[h], v_s[h])
            o_ref[r0:r0 + QBLK, h * hd:(h + 1) * hd] = o.astype(BF)


def _lat_c_kernel(z_ref, ck_ref, cv_ref, gq_ref, gk_ref, cos_ref, sin_ref, o_ref, k_s, v_s, q_s):
    hd = HEAD_DIM
    cos, sin = cos_ref[...], sin_ref[...]
    c = HEAD_DIM ** -0.5 * LOG2E
    for hk in range(2):
        k_s[hk, 0:PAST_LEN, :] = ck_ref[hk].astype(BF)
        k = _rms(z_ref[:, 512 + hk * hd:512 + (hk + 1) * hd], gk_ref[...])
        k_s[hk, PAST_LEN:, :] = _rope(k, cos, sin, 32).astype(BF)
        v_s[hk, 0:PAST_LEN, :] = _with_ones(cv_ref[hk].astype(BF))
        v_s[hk, PAST_LEN:, :] = _with_ones(z_ref[:, 768 + hk * hd:768 + (hk + 1) * hd].astype(BF))
    for h in range(4):
        q = _rms(z_ref[:, h * hd:(h + 1) * hd], gq_ref[...])
        q_s[h] = (_rope(q, cos, sin, 32) * c).astype(BF)
    for h in range(4):
        for i in range(N_QBLK):
            r0 = i * QBLK
            o = _attend(q_s[h, r0:r0 + QBLK, :], k_s[h // 2], v_s[h // 2])
            o_ref[r0:r0 + QBLK, h * hd:(h + 1) * hd] = o.astype(BF)


def _lat_d_kernel(lam_init, z_ref, dk_ref, dv_ref, lam_ref, gout_ref, cos_ref, sin_ref, o_ref, k_s, v_s, q1_s, q2_s):
    hd = HEAD_DIM
    cos, sin = cos_ref[...], sin_ref[...]
    c = D_HALF ** -0.5 * LOG2E
    lam = _lam(lam_ref, lam_init)
    lane = lax.broadcasted_iota(jnp.int32, (DEC_SEQ, hd), 1)
    for h in range(4):
        k_s[h, 0:PAST_LEN, :] = dk_ref[h].astype(BF)
        k_s[h, PAST_LEN:, :] = _rope(z_ref[:, 512 + h * hd:512 + (h + 1) * hd], cos, sin, 16).astype(BF)
        v_s[h, 0:PAST_LEN, :] = _with_ones(dv_ref[h].astype(BF))
        v_s[h, PAST_LEN:, :] = _with_ones(z_ref[:, 1024 + h * hd:1024 + (h + 1) * hd].astype(BF))
        q = _rope(z_ref[:, h * hd:(h + 1) * hd], cos, sin, 16) * c
        q1_s[h] = jnp.where(lane < D_HALF, q, 0.0).astype(BF)
        q2_s[h] = jnp.where(lane >= D_HALF, q, 0.0).astype(BF)
    for h in range(4):
        for i in range(N_QBLK):
            r0 = i * QBLK
            o1 = _attend(q1_s[h, r0:r0 + QBLK, :], k_s[h], v_s[h])
            o2 = _attend(q2_s[h, r0:r0 + QBLK, :], k_s[h], v_s[h])
            o = _rms(o1 - lam * o2, gout_ref[...]) * (1.0 - lam_init)
            o_ref[r0:r0 + QBLK, h * hd:(h + 1) * hd] = o.astype(BF)


def _lat_mixers(z, caches, l, p, tabs, lam_init, next_w):
    m = z.shape[0]
    hd = HEAD_DIM
    cos128, sin128, cos64, sin64 = tabs
    o_spec = pl.BlockSpec((DEC_SEQ, GROUP), lambda b: (b, 0))
    o_shape = jax.ShapeDtypeStruct((m, GROUP), BF)
    tab = _const((DEC_SEQ, hd))

    def cache_spec(h):
        return pl.BlockSpec((None, None, h, PAST_LEN, hd), lambda b: (b, l, 0, 0, 0))

    def zspec(width, col0):
        return pl.BlockSpec((DEC_SEQ, width), lambda b: (b, col0 // width))

    def scratch(*shape):
        return pltpu.VMEM(shape, BF)

    def call(kernel, name, in_specs, args, scratch_shapes, cast=None):
        out_specs, out_shape = o_spec, o_shape
        if cast is not None:
            w, in_block, in_index, out_block, out_index = cast
            kernel = _also_cast(kernel, len(in_specs))
            in_specs = in_specs + [pl.BlockSpec((None,) + in_block, in_index)]
            args = args + (w,)
            out_specs = [o_spec, pl.BlockSpec(out_block, out_index)]
            out_shape = [o_shape, jax.ShapeDtypeStruct(w.shape[1:], BF)]
        return pl.pallas_call(
            kernel, grid=(DEC_BATCH,), in_specs=in_specs, out_specs=out_specs, out_shape=out_shape,
            scratch_shapes=scratch_shapes, compiler_params=_params(("arbitrary",)), name=name)(*args)

    rows = lambda n: ((n, D_MODEL), lambda b: (l + 1, b, 0), (n, D_MODEL), lambda b: (b, 0))
    cols = lambda n: ((D_MODEL, n), lambda b: (l + 1, 0, b), (D_MODEL, n), lambda b: (0, b))
    cast_a = cast_c = cast_d = None
    if next_w is not None:
        w_out, w_down, w_up = next_w
        cast_a = (w_out,) + rows(D_MODEL // DEC_BATCH)
        cast_c = (w_down,) + rows(D_FF // DEC_BATCH)
        cast_d = (w_up,) + cols(D_FF // DEC_BATCH)

    o_a = call(_lat_a_kernel, "lat_mixer_a",
               [zspec(1024, 0), cache_spec(2), cache_spec(2), tab, tab, pl.BlockSpec(memory_space=pltpu.SMEM)],
               (z, caches['a_k'], caches['a_v'], cos128, sin128, p['a_sink']),
               [scratch(2, N_WBLK, 2 * WINDOW, hd), scratch(2, DEC_SEQ + 2 * WINDOW, hd),
                scratch(2, DEC_SEQ + 2 * WINDOW, 2 * hd)], cast_a)

    o_b = call(_lat_b_kernel, "lat_mixer_b",
               [zspec(1024, 1024), zspec(LANES, Z_MAIN),
                pl.BlockSpec((None, None, PAST_LEN, B_RANK + B_ROPE), lambda b: (b, l, 0, 0)),
                _const((1, B_RANK)), _const((B_RANK, GROUP)), _const((B_RANK, GROUP)), tab, tab],
               (z, z, caches['mla'], p['b_g_kv'], p['w_uk'], p['w_uv'], cos64, sin64),
               [scratch(KEYS, B_RANK), scratch(KEYS, hd), scratch(4, KEYS, 2 * hd), scratch(4, KEYS, 2 * hd),
                scratch(4, DEC_SEQ, 2 * hd)])

    o_c = call(_lat_c_kernel, "lat_mixer_c",
               [zspec(1024, 2048), cache_spec(2), cache_spec(2), _const((1, hd)), _const((1, hd)), tab, tab],
               (z, caches['c_k'], caches['c_v'], p['c_gq'], p['c_gk'], cos128, sin128),
               [scratch(2, KEYS, hd), scratch(2, KEYS, 2 * hd), scratch(4, DEC_SEQ, hd)], cast_c)

    o_d = call(functools.partial(_lat_d_kernel, lam_init), "lat_mixer_d",
               [zspec(1536, 3072), cache_spec(4), cache_spec(4), _const((4, D_HALF)), _const((1, hd)), tab, tab],
               (z, caches['d_k'], caches['d_v'], p['d_lam'], p['d_g_out'], cos64, sin64),
               [scratch(4, KEYS, hd), scratch(4, KEYS, 2 * hd), scratch(4, DEC_SEQ, hd), scratch(4, DEC_SEQ, hd)],
               cast_d)

    if next_w is None:
        return (o_a, o_b, o_c, o_d), None
    return (o_a[0], o_b, o_c[0], o_d[0]), (o_a[1], o_c[1], o_d[1])


def _rope_tables():
    t = jnp.arange(DEC_SEQ)
    row = (t // GRID_W).astype(F32)[:, None]
    col = (t % GRID_W).astype(F32)[:, None]

    def table(half):
        inv = ROPE_BASE ** (-jnp.arange(half, dtype=F32) / half)
        ar, ac = row * inv[None, :], col * inv[None, :]
        cos = jnp.concatenate([jnp.cos(ar), jnp.cos(ar), jnp.cos(ac), jnp.cos(ac)], axis=-1)
        sin = jnp.concatenate([-jnp.sin(ar), jnp.sin(ar), -jnp.sin(ac), jnp.sin(ac)], axis=-1)
        return cos, sin

    cos128, sin128 = table(32)
    cos64, sin64 = table(16)
    return cos128, sin128, jnp.tile(cos64, (1, 2)), jnp.tile(sin64, (1, 2))


def kernel(x_prompt, x_sample, cache_a_k, cache_a_v, cache_mla, cache_c_k, cache_c_v, cache_d_k, cache_d_v,
           c, c_ctx, w_ada, b_ada, g_attn_pre, g_attn_post, g_mlp_pre, g_mlp_post, w_in, a_sink,
           b_g_kv, b_w_uk, b_w_uv, c_gq, c_gk, d_lam, d_g_out, w_out, w_up, w_down):
    assert DEPTH == 2
    w_in_p = _regroup_w_in(w_in)
    w_out_b, w_down_b, w_up_b = w_out[0].astype(BF), w_down[0].astype(BF), w_up[0].astype(BF)
    w_uk_b = b_w_uk.reshape(DEPTH, B_RANK, GROUP).astype(BF)
    w_uv_b = b_w_uv.reshape(DEPTH, B_RANK, GROUP).astype(BF)

    cvec = jnp.concatenate([c_ctx[None, :], c, jnp.zeros((MOD_ROWS - 1 - DEC_BATCH, D_MODEL), F32)], axis=0)
    mod = _modulation(cvec, w_ada, b_ada).reshape(DEPTH, MOD_ROWS, MOD_CHUNKS, D_MODEL)
    tabs = _rope_tables()

    tm = 512
    tm_mlp, tf_mlp = 1024, 512
    ctx_row = lambda tm: (lambda i: 0)
    lat_row = lambda tm: (lambda i: 1 + (i * tm) // DEC_SEQ)
    caches = {'a_k': cache_a_k, 'a_v': cache_a_v, 'mla': cache_mla, 'c_k': cache_c_k, 'c_v': cache_c_v,
              'd_k': cache_d_k, 'd_v': cache_d_v}

    xc = x_prompt.reshape(BATCH * SEQ, D_MODEL)
    xl = x_sample.reshape(DEC_BATCH * DEC_SEQ, D_MODEL)
    states = ()
    for l in range(DEPTH):
        lam_init = 0.8 - 0.6 * math.exp(-0.3 * l)
        p = {'a_sink': a_sink[l], 'b_g_kv': b_g_kv[l][None], 'w_uk': w_uk_b[l], 'w_uv': w_uv_b[l],
             'c_gq': c_gq[l][None], 'c_gk': c_gk[l][None], 'd_lam': d_lam[l], 'd_g_out': d_g_out[l][None]}
        g_pre, g_post = g_attn_pre[l][None], g_attn_post[l][None]
        g_mpre, g_mpost = g_mlp_pre[l][None], g_mlp_post[l][None]

        zc = _in_projection(xc, mod, ctx_row(tm), l, g_pre, w_in_p, tm)
        outs = _ctx_mixers(zc, p, lam_init, states)
        states = tuple(outs[4:])
        xc, hc = _out_projection(outs[:4], w_out_b, xc, mod, ctx_row(tm), l, g_post, g_mpre, tm)
        xc = _mlp(hc, xc, mod, ctx_row(tm_mlp), l, g_mpost, w_up_b, w_down_b, tm_mlp, tf_mlp)

        zl = _in_projection(xl, mod, lat_row(tm), l, g_pre, w_in_p, tm)
        next_w = (w_out, w_down, w_up) if l + 1 < DEPTH else None
        o_parts, next_b = _lat_mixers(zl, caches, l, p, tabs, lam_init, next_w)
        xl, hl = _out_projection(o_parts, w_out_b, xl, mod, lat_row(tm), l, g_post, g_mpre, tm)
        xl = _mlp(hl, xl, mod, lat_row(tm_mlp), l, g_mpost, w_up_b, w_down_b, tm_mlp, tf_mlp)
        if next_b is not None:
            w_out_b, w_down_b, w_up_b = next_b

    y_prompt = xc.reshape(BATCH, SEQ, D_MODEL)
    y_sample = xl.reshape(DEC_BATCH, DEC_SEQ, D_MODEL)
    return (y_prompt, y_sample) + states
```

```python
import functools
import math

import jax
import jax.numpy as jnp
from jax import lax
from jax.experimental import pallas as pl
from jax.experimental.pallas import tpu as pltpu

D_MODEL = 2048
BATCH = 32
SEQ = 256
DEPTH = 2
DEC_BATCH = 8
DEC_SEQ = 1024
PAST_LEN = 256
GRID_W = 64
HEAD_DIM = 128
WINDOW = 128
B_ROPE = 64
B_RANK = 256
D_HALF = 64
D_FF = 4 * D_MODEL
ROPE_BASE = 10000.0
EPS = 1e-6
MOD_CHUNKS = 6
MOD_ROWS = 16

GROUP = 512
Z_MAIN = 4608
Z_WIDTH = 4864
KEYS = PAST_LEN + DEC_SEQ

LANES = 128
VMEM_LIMIT = 56 * 1024 * 1024

BF = jnp.bfloat16
F32 = jnp.float32
NEG = -1e30
LOG2E = math.log2(math.e)


def _dot(a, b):
    return jnp.dot(a, b, preferred_element_type=F32)


def _dot_nt(a, b):
    return lax.dot_general(a, b, (((1,), (1,)), ((), ())), preferred_element_type=F32)


def _rms(x, g):
    return x * lax.rsqrt(jnp.mean(x * x, axis=-1, keepdims=True) + EPS) * g


def _params(sem):
    return pltpu.CompilerParams(dimension_semantics=sem, vmem_limit_bytes=VMEM_LIMIT)


def _with_ones(v):
    return jnp.concatenate([v, jnp.ones(v.shape, v.dtype)], axis=1)


def _attend(q, k, v1, sink2=None):
    s = _dot_nt(q, k)
    m = jnp.max(s, axis=-1, keepdims=True)
    if sink2 is not None:
        m = jnp.maximum(m, sink2)
    o = _dot(jnp.exp2(s - m).astype(BF), v1)
    l = o[:, HEAD_DIM:]
    if sink2 is not None:
        l = l + jnp.exp2(sink2 - m)
    return o[:, :HEAD_DIM] / l


def _rope(x, cos, sin_signed, half):
    lane = lax.broadcasted_iota(jnp.int32, (8, LANES), 1)
    from_up = pltpu.roll(lane, half, 1)[0:1, :] == (lane[0:1, :] ^ half)
    partner = jnp.where(from_up, pltpu.roll(x, half, 1), pltpu.roll(x, LANES - half, 1))
    return x * cos + partner * sin_signed


def _lam(lam_ref, lam_init):
    lp = lam_ref[...]
    a = jnp.sum(lp[0:1, :] * lp[1:2, :], axis=-1, keepdims=True)
    b = jnp.sum(lp[2:3, :] * lp[3:4, :], axis=-1, keepdims=True)
    return jnp.exp(a) - jnp.exp(b) + lam_init


def _mod_kernel(c_ref, w_ref, b_ref, o_ref):
    c = c_ref[...]
    s = c * jax.nn.sigmoid(c)
    o_ref[...] = _dot(s.astype(BF), w_ref[...].astype(BF)) + b_ref[...]


def _modulation(cvec, w_ada, b_ada):
    tn = 1024
    n = MOD_CHUNKS * D_MODEL
    return pl.pallas_call(
        _mod_kernel,
        grid=(DEPTH, n // tn),
        in_specs=[
            pl.BlockSpec((MOD_ROWS, D_MODEL), lambda l, j: (0, 0)),
            pl.BlockSpec((None, D_MODEL, tn), lambda l, j: (l, 0, j)),
            pl.BlockSpec((None, 1, tn), lambda l, j: (l, 0, j)),
        ],
        out_specs=pl.BlockSpec((None, MOD_ROWS, tn), lambda l, j: (l, 0, j)),
        out_shape=jax.ShapeDtypeStruct((DEPTH, MOD_ROWS, n), F32),
        compiler_params=_params(("arbitrary", "arbitrary")),
        name="modulation",
    )(cvec, w_ada, b_ada.reshape(DEPTH, 1, n))


W_IN_COLS = 4672
KR_COL = 2048


def _regroup_kernel(w_ref, o_ref):
    o_ref[:, 0:KR_COL] = w_ref[:, 0:KR_COL].astype(BF)
    o_ref[:, KR_COL:Z_MAIN] = w_ref[:, KR_COL + B_ROPE:W_IN_COLS].astype(BF)
    o_ref[:, Z_MAIN:Z_MAIN + B_ROPE] = w_ref[:, KR_COL:KR_COL + B_ROPE].astype(BF)
    o_ref[:, Z_MAIN + B_ROPE:] = jnp.zeros((w_ref.shape[0], Z_WIDTH - Z_MAIN - B_ROPE), BF)


def _regroup_w_in(w_in):
    tr = 256
    return pl.pallas_call(
        _regroup_kernel,
        grid=(DEPTH, D_MODEL // tr),
        in_specs=[pl.BlockSpec((None, tr, W_IN_COLS), lambda l, i: (l, i, 0))],
        out_specs=pl.BlockSpec((None, tr, Z_WIDTH), lambda l, i: (l, i, 0)),
        out_shape=jax.ShapeDtypeStruct((DEPTH, D_MODEL, Z_WIDTH), BF),
        compiler_params=_params(("arbitrary", "arbitrary")),
        name="regroup_w_in",
    )(w_in)


def _also_cast(kernel, n_in):
    def wrapped(*refs):
        refs[n_in + 2][...] = refs[n_in][...].astype(BF)
        kernel(*refs[:n_in], refs[n_in + 1], *refs[n_in + 3:])
    return wrapped


SUB = 256


def _sub_tiles(rows, sub=SUB):
    return [slice(s, s + sub) for s in range(0, rows, sub)]


def _inproj_kernel(x_ref, mod_ref, g_ref, w_ref, z_ref):
    for rows in _sub_tiles(x_ref.shape[0]):
        h = _rms(x_ref[rows, :], g_ref[...]) * (1.0 + mod_ref[1:2, :]) + mod_ref[0:1, :]
        z_ref[rows, :] = _dot(h.astype(BF), w_ref[...])


def _in_projection(x, mod, mod_row, l, g, w, tm):
    m = x.shape[0]
    return pl.pallas_call(
        _inproj_kernel,
        grid=(m // tm,),
        in_specs=[
            pl.BlockSpec((tm, D_MODEL), lambda i: (i, 0)),
            pl.BlockSpec((None, None, MOD_CHUNKS, D_MODEL), lambda i: (l, mod_row(i), 0, 0)),
            pl.BlockSpec((1, D_MODEL), lambda i: (0, 0)),
            pl.BlockSpec((None, D_MODEL, Z_WIDTH), lambda i: (l, 0, 0), pipeline_mode=pl.Buffered(1)),
        ],
        out_specs=pl.BlockSpec((tm, Z_WIDTH), lambda i: (i, 0)),
        out_shape=jax.ShapeDtypeStruct((m, Z_WIDTH), F32),
        compiler_params=_params(("arbitrary",)),
        name="in_projection",
    )(x, mod, g, w)


def _outproj_kernel(oa_ref, ob_ref, oc_ref, od_ref, w_ref, x_ref, mod_ref, gpost_ref, gpre_ref, y_ref, h_ref):
    for rows in _sub_tiles(x_ref.shape[0], SUB // 2):
        o = jnp.concatenate([oa_ref[rows, :], ob_ref[rows, :], oc_ref[rows, :], od_ref[rows, :]], axis=1)
        y = x_ref[rows, :] + mod_ref[2:3, :] * _rms(_dot(o, w_ref[...]), gpost_ref[...])
        y_ref[rows, :] = y
        h_ref[rows, :] = (_rms(y, gpre_ref[...]) * (1.0 + mod_ref[4:5, :]) + mod_ref[3:4, :]).astype(BF)


def _out_projection(o_parts, w, x, mod, mod_row, l, gpost, gpre, tm):
    m = x.shape[0]
    o_spec = pl.BlockSpec((tm, GROUP), lambda i: (i, 0))
    row_spec = pl.BlockSpec((tm, D_MODEL), lambda i: (i, 0))
    return pl.pallas_call(
        _outproj_kernel,
        grid=(m // tm,),
        in_specs=[
            o_spec, o_spec, o_spec, o_spec,
            pl.BlockSpec((D_MODEL, D_MODEL), lambda i: (0, 0)),
            row_spec,
            pl.BlockSpec((None, None, MOD_CHUNKS, D_MODEL), lambda i: (l, mod_row(i), 0, 0)),
            pl.BlockSpec((1, D_MODEL), lambda i: (0, 0)),
            pl.BlockSpec((1, D_MODEL), lambda i: (0, 0)),
        ],
        out_specs=[row_spec, row_spec],
        out_shape=[jax.ShapeDtypeStruct((m, D_MODEL), F32), jax.ShapeDtypeStruct((m, D_MODEL), BF)],
        compiler_params=_params(("arbitrary",)),
        name="out_projection",
    )(*o_parts, w, x, mod, gpost, gpre)


def _mlp_kernel(h_ref, x_ref, mod_ref, gpost_ref, wu_ref, wd_ref, y_ref):
    k = pl.program_id(1)
    last = pl.num_programs(1) - 1

    def ff(rows):
        u = jnp.square(jnp.maximum(_dot(h_ref[rows, :], wu_ref[...]), 0.0))
        return _dot(u.astype(BF), wd_ref[...])

    @pl.when(k == 0)
    def _():
        y_ref[...] = ff(slice(None))

    @pl.when((k > 0) & (k < last))
    def _():
        y_ref[...] += ff(slice(None))

    @pl.when(k == last)
    def _():
        for rows in _sub_tiles(x_ref.shape[0]):
            acc = y_ref[rows, :] + ff(rows)
            y_ref[rows, :] = x_ref[rows, :] + mod_ref[5:6, :] * _rms(acc, gpost_ref[...])


def _mlp(h, x, mod, mod_row, l, gpost, wu, wd, tm, tf):
    m = x.shape[0]
    assert D_FF // tf >= 2
    row_spec = pl.BlockSpec((tm, D_MODEL), lambda i, k: (i, 0))
    return pl.pallas_call(
        _mlp_kernel,
        grid=(m // tm, D_FF // tf),
        in_specs=[
            row_spec, row_spec,
            pl.BlockSpec((None, None, MOD_CHUNKS, D_MODEL), lambda i, k: (l, mod_row(i), 0, 0)),
            pl.BlockSpec((1, D_MODEL), lambda i, k: (0, 0)),
            pl.BlockSpec((D_MODEL, tf), lambda i, k: (0, k)),
            pl.BlockSpec((tf, D_MODEL), lambda i, k: (k, 0)),
        ],
        out_specs=row_spec,
        out_shape=jax.ShapeDtypeStruct((m, D_MODEL), F32),
        compiler_params=_params(("arbitrary", "arbitrary")),
        name="mlp",
    )(h, x, mod, gpost, wu, wd)


N_STATES = 7
STATE_HEADS = (2, 2, None, 2, 2, 4, 4)
RPS = 2


def _ctx_mixer_body(lam_init, z_ref, sink_ref, gkv_ref, wuk_ref, wuv_ref, gq_ref, gk_ref, lam_ref, gout_ref,
                    o_refs, st_refs):
    oa_ref, ob_ref, oc_ref, od_ref = o_refs
    ka_ref, va_ref, mla_ref, kc_ref, vc_ref, kd_ref, vd_ref = st_refs
    hd = HEAD_DIM

    c = HEAD_DIM ** -0.5 * LOG2E
    for hk in range(2):
        k = z_ref[:, 512 + hk * hd:512 + (hk + 1) * hd]
        v = z_ref[:, 768 + hk * hd:768 + (hk + 1) * hd]
        ka_ref[hk] = k
        va_ref[hk] = v
        kb, v1 = k.astype(BF), _with_ones(v.astype(BF))
        for g in range(2):
            h = hk * 2 + g
            q = (z_ref[:, h * hd:(h + 1) * hd] * c).astype(BF)
            oa_ref[:, h * hd:(h + 1) * hd] = _attend(q, kb, v1, sink_ref[h] * LOG2E).astype(BF)

    c = (HEAD_DIM + B_ROPE) ** -0.5 * LOG2E
    c_lat = _rms(z_ref[:, 1792:2048], gkv_ref[...])
    kr = z_ref[:, Z_MAIN:Z_MAIN + B_ROPE]
    mla_ref[:, 0:B_RANK] = c_lat
    mla_ref[:, B_RANK:B_RANK + B_ROPE] = kr
    cb, krb = c_lat.astype(BF), kr.astype(BF)
    for h in range(4):
        kn = _dot(cb, wuk_ref[:, h * hd:(h + 1) * hd]).astype(BF)
        v1 = _with_ones(_dot(cb, wuv_ref[:, h * hd:(h + 1) * hd]).astype(BF))
        qn = (z_ref[:, 1024 + h * hd:1024 + (h + 1) * hd] * c).astype(BF)
        qr = (z_ref[:, 1536 + h * B_ROPE:1536 + (h + 1) * B_ROPE] * c).astype(BF)
        s = _dot_nt(qn, kn) + _dot_nt(qr, krb)
        m = jnp.max(s, axis=-1, keepdims=True)
        o = _dot(jnp.exp2(s - m).astype(BF), v1)
        ob_ref[:, h * hd:(h + 1) * hd] = (o[:, :hd] / o[:, hd:]).astype(BF)

    c = HEAD_DIM ** -0.5 * LOG2E
    for hk in range(2):
        k = _rms(z_ref[:, 2560 + hk * hd:2560 + (hk + 1) * hd], gk_ref[...])
        v = z_ref[:, 2816 + hk * hd:2816 + (hk + 1) * hd]
        kc_ref[hk] = k
        vc_ref[hk] = v
        kb, v1 = k.astype(BF), _with_ones(v.astype(BF))
        for g in range(2):
            h = hk * 2 + g
            q = (_rms(z_ref[:, 2048 + h * hd:2048 + (h + 1) * hd], gq_ref[...]) * c).astype(BF)
            oc_ref[:, h * hd:(h + 1) * hd] = _attend(q, kb, v1).astype(BF)

    c = D_HALF ** -0.5 * LOG2E
    lam = _lam(lam_ref, lam_init)
    lane = lax.broadcasted_iota(jnp.int32, (SEQ, hd), 1)
    for h in range(4):
        k = z_ref[:, 3584 + h * hd:3584 + (h + 1) * hd]
        v = z_ref[:, 4096 + h * hd:4096 + (h + 1) * hd]
        kd_ref[h] = k
        vd_ref[h] = v
        kb, v1 = k.astype(BF), _with_ones(v.astype(BF))
        q = z_ref[:, 3072 + h * hd:3072 + (h + 1) * hd] * c
        o1 = _attend(jnp.where(lane < D_HALF, q, 0.0).astype(BF), kb, v1)
        o2 = _attend(jnp.where(lane >= D_HALF, q, 0.0).astype(BF), kb, v1)
        od_ref[:, h * hd:(h + 1) * hd] = (_rms(o1 - lam * o2, gout_ref[...]) * (1.0 - lam_init)).astype(BF)


def _ctx_layer_kernel(lam_init, n_alias, x_ref, mod_ref, g_ref, w_ref, sink_ref, gkv_ref, wuk_ref, wuv_ref, gq_ref, gk_ref,
                      lam_ref, gout_ref, *refs):
    refs = refs[n_alias:]
    o_refs, st_refs, z_s = refs[:4], refs[4:4 + N_STATES], refs[4 + N_STATES]
    for r in range(RPS):
        rows = slice(r * SEQ, (r + 1) * SEQ)
        h = _rms(x_ref[rows, :], g_ref[...]) * (1.0 + mod_ref[1:2, :]) + mod_ref[0:1, :]
        z_s[rows, :] = _dot(h.astype(BF), w_ref[...])
    for r in range(RPS):
        rows = slice(r * SEQ, (r + 1) * SEQ)
        _ctx_mixer_body(lam_init, z_s.at[rows, :], sink_ref, gkv_ref, wuk_ref, wuv_ref, gq_ref, gk_ref, lam_ref, gout_ref,
                        [o.at[rows, :] for o in o_refs], [s.at[r] for s in st_refs])


def _const(shape):
    return pl.BlockSpec(shape, lambda *_: (0,) * len(shape))


def _ctx_layer(x, mod, l, g, w, p, lam_init, prev_states):
    m = x.shape[0]
    o_spec = pl.BlockSpec((RPS * SEQ, GROUP), lambda b: (b, 0))
    o_shape = jax.ShapeDtypeStruct((m, GROUP), BF)

    def st_dims(h):
        return (SEQ, B_RANK + B_ROPE) if h is None else (h, SEQ, HEAD_DIM)

    def st_spec(h):
        dims = st_dims(h)
        return pl.BlockSpec((RPS, None) + dims, lambda b: (b, l) + (0,) * len(dims))

    in_specs = [
        pl.BlockSpec((RPS * SEQ, D_MODEL), lambda b: (b, 0)),
        pl.BlockSpec((None, None, MOD_CHUNKS, D_MODEL), lambda b: (l, 0, 0, 0)),
        _const((1, D_MODEL)),
        pl.BlockSpec((None, D_MODEL, Z_WIDTH), lambda b: (l, 0, 0), pipeline_mode=pl.Buffered(1)),
        pl.BlockSpec(memory_space=pltpu.SMEM),
        _const((1, B_RANK)), _const((B_RANK, GROUP)), _const((B_RANK, GROUP)),
        _const((1, HEAD_DIM)), _const((1, HEAD_DIM)), _const((4, D_HALF)), _const((1, HEAD_DIM)),
    ]
    n_alias = len(prev_states)
    aliases = {len(in_specs) + k: 4 + k for k in range(n_alias)}
    return pl.pallas_call(
        functools.partial(_ctx_layer_kernel, lam_init, n_alias),
        grid=(BATCH // RPS,),
        in_specs=in_specs + [pl.BlockSpec(memory_space=pl.ANY)] * n_alias,
        out_specs=[o_spec] * 4 + [st_spec(h) for h in STATE_HEADS],
        out_shape=[o_shape] * 4 + [jax.ShapeDtypeStruct((BATCH, DEPTH) + st_dims(h), F32) for h in STATE_HEADS],
        scratch_shapes=[pltpu.VMEM((RPS * SEQ, Z_WIDTH), F32)],
        input_output_aliases=aliases,
        compiler_params=_params(("arbitrary",)),
        name="ctx_layer",
    )(x, mod, g, w, p['a_sink'], p['b_g_kv'], p['w_uk'], p['w_uv'], p['c_gq'], p['c_gk'], p['d_lam'], p['d_g_out'],
      *prev_states)


QBLK = 256
N_QBLK = DEC_SEQ // QBLK
N_WBLK = DEC_SEQ // WINDOW


def _lat_a_kernel(z_ref, ck_ref, cv_ref, cos_ref, sin_ref, sink_ref, o_ref, q_s, k_s, v_s):
    hd = HEAD_DIM
    cos, sin = cos_ref[...], sin_ref[...]
    c = HEAD_DIM ** -0.5 * LOG2E
    for hk in range(2):
        k_s[hk, 0:WINDOW, :] = jnp.zeros((WINDOW, hd), BF)
        k_s[hk, WINDOW + DEC_SEQ:, :] = jnp.zeros((WINDOW, hd), BF)
        v_s[hk, 0:WINDOW, :] = jnp.zeros((WINDOW, 2 * hd), BF)
        v_s[hk, WINDOW + DEC_SEQ:, :] = jnp.zeros((WINDOW, 2 * hd), BF)
        k = _rope(z_ref[:, 512 + hk * hd:512 + (hk + 1) * hd], cos, sin, 32)
        k_s[hk, WINDOW:WINDOW + DEC_SEQ, :] = k.astype(BF)
        v_s[hk, WINDOW:WINDOW + DEC_SEQ, :] = _with_ones(z_ref[:, 768 + hk * hd:768 + (hk + 1) * hd].astype(BF))
        for g in range(2):
            h = hk * 2 + g
            q = (_rope(z_ref[:, h * hd:(h + 1) * hd], cos, sin, 32) * c).astype(BF)
            for i in range(N_WBLK):
                q_s[hk, i, g * WINDOW:(g + 1) * WINDOW, :] = q[i * WINDOW:(i + 1) * WINDOW, :]

    r_io = lax.broadcasted_iota(jnp.int32, (2 * WINDOW, 3 * WINDOW), 0) & (WINDOW - 1)
    c_io = lax.broadcasted_iota(jnp.int32, (2 * WINDOW, 3 * WINDOW), 1)
    second = lax.broadcasted_iota(jnp.int32, (2 * WINDOW, 1), 0) >= WINDOW
    for hk in range(2):
        kc = ck_ref[hk].astype(BF)
        vc1 = _with_ones(cv_ref[hk].astype(BF))
        sink2 = jnp.where(second, sink_ref[2 * hk + 1] * LOG2E, sink_ref[2 * hk] * LOG2E)
        for i in range(N_WBLK):
            r0 = i * WINDOW
            q = q_s[hk, i]
            kl = k_s[hk, r0:r0 + 3 * WINDOW, :]
            vl1 = v_s[hk, r0:r0 + 3 * WINDOW, :]
            kpos = (i - 1) * WINDOW + c_io
            qpos = i * WINDOW + r_io
            ok = (jnp.abs(qpos - kpos) <= WINDOW) & (kpos >= 0) & (kpos < DEC_SEQ)
            sc = _dot_nt(q, kc)
            sl = _dot_nt(q, kl)
            m = jnp.maximum(jnp.maximum(jnp.max(sc, axis=-1, keepdims=True),
                                        jnp.max(jnp.where(ok, sl, NEG), axis=-1, keepdims=True)), sink2)
            pw = jnp.where(ok, jnp.exp2(sl - m), 0.0)
            o = _dot(jnp.exp2(sc - m).astype(BF), vc1) + _dot(pw.astype(BF), vl1)
            o = (o[:, :hd] / (o[:, hd:] + jnp.exp2(sink2 - m))).astype(BF)
            o_ref[r0:r0 + WINDOW, 2 * hk * hd:(2 * hk + 1) * hd] = o[0:WINDOW, :]
            o_ref[r0:r0 + WINDOW, (2 * hk + 1) * hd:(2 * hk + 2) * hd] = o[WINDOW:, :]


def _lat_b_kernel(z_ref, zr_ref, mla_ref, gkv_ref, wuk_ref, wuv_ref, cos_ref, sin_ref, o_ref, c_s, kr_s, k_s, v_s, q_s):
    hd = HEAD_DIM
    cos, sin = cos_ref[...], sin_ref[...]
    c = (HEAD_DIM + B_ROPE) ** -0.5 * LOG2E
    c_s[0:PAST_LEN, :] = mla_ref[:, 0:B_RANK].astype(BF)
    c_s[PAST_LEN:, :] = _rms(z_ref[:, 768:1024], gkv_ref[...]).astype(BF)
    kr_s[0:PAST_LEN, B_ROPE:] = jnp.zeros((PAST_LEN, hd - B_ROPE), BF)
    kr_s[0:PAST_LEN, 0:B_ROPE] = mla_ref[:, B_RANK:B_RANK + B_ROPE].astype(BF)
    kr_s[PAST_LEN:, :] = _rope(zr_ref[...], cos, sin, 16).astype(BF)
    lane = lax.broadcasted_iota(jnp.int32, (DEC_SEQ, hd), 1)
    for h in range(4):
        k_s[h, :, 0:hd] = _dot(c_s[...], wuk_ref[:, h * hd:(h + 1) * hd]).astype(BF)
        k_s[h, :, hd:2 * hd] = kr_s[...]
        v_s[h] = _with_ones(_dot(c_s[...], wuv_ref[:, h * hd:(h + 1) * hd]).astype(BF))
        q_s[h, :, 0:hd] = (z_ref[:, h * hd:(h + 1) * hd] * c).astype(BF)
        qr = _rope(z_ref[:, 512 + (h // 2) * hd:512 + (h // 2 + 1) * hd], cos, sin, 16) * c
        if h % 2:
            qr = pltpu.roll(qr, B_ROPE, 1)
        q_s[h, :, hd:2 * hd] = jnp.where(lane < B_ROPE, qr, 0.0).astype(BF)
    for h in range(4):
        for i in range(N_QBLK):
            r0 = i * QBLK
            o = _attend(q_s[h, r0:r0 + QBLK, :], k_s[h], v_s[h])
            o_ref[r0:r0 + QBLK, h * hd:(h + 1) * hd] = o.astype(BF)


def _lat_c_kernel(z_ref, ck_ref, cv_ref, gq_ref, gk_ref, cos_ref, sin_ref, o_ref, k_s, v_s, q_s):
    hd = HEAD_DIM
    cos, sin = cos_ref[...], sin_ref[...]
    c = HEAD_DIM ** -0.5 * LOG2E
    for hk in range(2):
        k_s[hk, 0:PAST_LEN, :] = ck_ref[hk].astype(BF)
        k = _rms(z_ref[:, 512 + hk * hd:512 + (hk + 1) * hd], gk_ref[...])
        k_s[hk, PAST_LEN:, :] = _rope(k, cos, sin, 32).astype(BF)
        v_s[hk, 0:PAST_LEN, :] = _with_ones(cv_ref[hk].astype(BF))
        v_s[hk, PAST_LEN:, :] = _with_ones(z_ref[:, 768 + hk * hd:768 + (hk + 1) * hd].astype(BF))
    for h in range(4):
        q = _rms(z_ref[:, h * hd:(h + 1) * hd], gq_ref[...])
        q_s[h] = (_rope(q, cos, sin, 32) * c).astype(BF)
    for h in range(4):
        for i in range(N_QBLK):
            r0 = i * QBLK
            o = _attend(q_s[h, r0:r0 + QBLK, :], k_s[h // 2], v_s[h // 2])
            o_ref[r0:r0 + QBLK, h * hd:(h + 1) * hd] = o.astype(BF)


def _lat_d_kernel(lam_init, z_ref, dk_ref, dv_ref, lam_ref, gout_ref, cos_ref, sin_ref, o_ref, k_s, v_s, q1_s, q2_s):
    hd = HEAD_DIM
    cos, sin = cos_ref[...], sin_ref[...]
    c = D_HALF ** -0.5 * LOG2E
    lam = _lam(lam_ref, lam_init)
    lane = lax.broadcasted_iota(jnp.int32, (DEC_SEQ, hd), 1)
    for h in range(4):
        k_s[h, 0:PAST_LEN, :] = dk_ref[h].astype(BF)
        k_s[h, PAST_LEN:, :] = _rope(z_ref[:, 512 + h * hd:512 + (h + 1) * hd], cos, sin, 16).astype(BF)
        v_s[h, 0:PAST_LEN, :] = _with_ones(dv_ref[h].astype(BF))
        v_s[h, PAST_LEN:, :] = _with_ones(z_ref[:, 1024 + h * hd:1024 + (h + 1) * hd].astype(BF))
        q = _rope(z_ref[:, h * hd:(h + 1) * hd], cos, sin, 16) * c
        q1_s[h] = jnp.where(lane < D_HALF, q, 0.0).astype(BF)
        q2_s[h] = jnp.where(lane >= D_HALF, q, 0.0).astype(BF)
    for h in range(4):
        for i in range(N_QBLK):
            r0 = i * QBLK
            o1 = _attend(q1_s[h, r0:r0 + QBLK, :], k_s[h], v_s[h])
            o2 = _attend(q2_s[h, r0:r0 + QBLK, :], k_s[h], v_s[h])
            o = _rms(o1 - lam * o2, gout_ref[...]) * (1.0 - lam_init)
            o_ref[r0:r0 + QBLK, h * hd:(h + 1) * hd] = o.astype(BF)


def _lat_mixers(z, caches, l, p, tabs, lam_init, next_w):
    m = z.shape[0]
    hd = HEAD_DIM
    cos128, sin128, cos64, sin64 = tabs
    o_spec = pl.BlockSpec((DEC_SEQ, GROUP), lambda b: (b, 0))
    o_shape = jax.ShapeDtypeStruct((m, GROUP), BF)
    tab = _const((DEC_SEQ, hd))

    def cache_spec(h):
        return pl.BlockSpec((None, None, h, PAST_LEN, hd), lambda b: (b, l, 0, 0, 0))

    def zspec(width, col0):
        return pl.BlockSpec((DEC_SEQ, width), lambda b: (b, col0 // width))

    def scratch(*shape):
        return pltpu.VMEM(shape, BF)

    def call(kernel, name, in_specs, args, scratch_shapes, cast=None):
        out_specs, out_shape = o_spec, o_shape
        if cast is not None:
            w, in_block, in_index, out_block, out_index = cast
            kernel = _also_cast(kernel, len(in_specs))
            in_specs = in_specs + [pl.BlockSpec((None,) + in_block, in_index)]
            args = args + (w,)
            out_specs = [o_spec, pl.BlockSpec(out_block, out_index)]
            out_shape = [o_shape, jax.ShapeDtypeStruct(w.shape[1:], BF)]
        return pl.pallas_call(
            kernel, grid=(DEC_BATCH,), in_specs=in_specs, out_specs=out_specs, out_shape=out_shape,
            scratch_shapes=scratch_shapes, compiler_params=_params(("arbitrary",)), name=name)(*args)

    rows = lambda n: ((n, D_MODEL), lambda b: (l + 1, b, 0), (n, D_MODEL), lambda b: (b, 0))
    cols = lambda n: ((D_MODEL, n), lambda b: (l + 1, 0, b), (D_MODEL, n), lambda b: (0, b))
    cast_a = cast_c = cast_d = None
    if next_w is not None:
        w_out, w_down, w_up = next_w
        cast_a = (w_out,) + rows(D_MODEL // DEC_BATCH)
        cast_c = (w_down,) + rows(D_FF // DEC_BATCH)
        cast_d = (w_up,) + cols(D_FF // DEC_BATCH)

    o_a = call(_lat_a_kernel, "lat_mixer_a",
               [zspec(1024, 0), cache_spec(2), cache_spec(2), tab, tab, pl.BlockSpec(memory_space=pltpu.SMEM)],
               (z, caches['a_k'], caches['a_v'], cos128, sin128, p['a_sink']),
               [scratch(2, N_WBLK, 2 * WINDOW, hd), scratch(2, DEC_SEQ + 2 * WINDOW, hd),
                scratch(2, DEC_SEQ + 2 * WINDOW, 2 * hd)], cast_a)

    o_b = call(_lat_b_kernel, "lat_mixer_b",
               [zspec(1024, 1024), zspec(LANES, Z_MAIN),
                pl.BlockSpec((None, None, PAST_LEN, B_RANK + B_ROPE), lambda b: (b, l, 0, 0)),
                _const((1, B_RANK)), _const((B_RANK, GROUP)), _const((B_RANK, GROUP)), tab, tab],
               (z, z, caches['mla'], p['b_g_kv'], p['w_uk'], p['w_uv'], cos64, sin64),
               [scratch(KEYS, B_RANK), scratch(KEYS, hd), scratch(4, KEYS, 2 * hd), scratch(4, KEYS, 2 * hd),
                scratch(4, DEC_SEQ, 2 * hd)])

    o_c = call(_lat_c_kernel, "lat_mixer_c",
               [zspec(1024, 2048), cache_spec(2), cache_spec(2), _const((1, hd)), _const((1, hd)), tab, tab],
               (z, caches['c_k'], caches['c_v'], p['c_gq'], p['c_gk'], cos128, sin128),
               [scratch(2, KEYS, hd), scratch(2, KEYS, 2 * hd), scratch(4, DEC_SEQ, hd)], cast_c)

    o_d = call(functools.partial(_lat_d_kernel, lam_init), "lat_mixer_d",
               [zspec(1536, 3072), cache_spec(4), cache_spec(4), _const((4, D_HALF)), _const((1, hd)), tab, tab],
               (z, caches['d_k'], caches['d_v'], p['d_lam'], p['d_g_out'], cos64, sin64),
               [scratch(4, KEYS, hd), scratch(4, KEYS, 2 * hd), scratch(4, DEC_SEQ, hd), scratch(4, DEC_SEQ, hd)],
               cast_d)

    if next_w is None:
        return (o_a, o_b, o_c, o_d), None
    return (o_a[0], o_b, o_c[0], o_d[0]), (o_a[1], o_c[1], o_d[1])


def _rope_tables():
    t = jnp.arange(DEC_SEQ)
    row = (t // GRID_W).astype(F32)[:, None]
    col = (t % GRID_W).astype(F32)[:, None]

    def table(half):
        inv = ROPE_BASE ** (-jnp.arange(half, dtype=F32) / half)
        ar, ac = row * inv[None, :], col * inv[None, :]
        cos = jnp.concatenate([jnp.cos(ar), jnp.cos(ar), jnp.cos(ac), jnp.cos(ac)], axis=-1)
        sin = jnp.concatenate([-jnp.sin(ar), jnp.sin(ar), -jnp.sin(ac), jnp.sin(ac)], axis=-1)
        return cos, sin

    cos128, sin128 = table(32)
    cos64, sin64 = table(16)
    return cos128, sin128, jnp.tile(cos64, (1, 2)), jnp.tile(sin64, (1, 2))


def kernel(x_prompt, x_sample, cache_a_k, cache_a_v, cache_mla, cache_c_k, cache_c_v, cache_d_k, cache_d_v,
           c, c_ctx, w_ada, b_ada, g_attn_pre, g_attn_post, g_mlp_pre, g_mlp_post, w_in, a_sink,
           b_g_kv, b_w_uk, b_w_uv, c_gq, c_gk, d_lam, d_g_out, w_out, w_up, w_down):
    w_in_p = _regroup_w_in(w_in)
    w_out_b, w_down_b, w_up_b = w_out[0].astype(BF), w_down[0].astype(BF), w_up[0].astype(BF)
    w_uk_b = b_w_uk.reshape(DEPTH, B_RANK, GROUP).astype(BF)
    w_uv_b = b_w_uv.reshape(DEPTH, B_RANK, GROUP).astype(BF)

    cvec = jnp.concatenate([c_ctx[None, :], c, jnp.zeros((MOD_ROWS - 1 - DEC_BATCH, D_MODEL), F32)], axis=0)
    mod = _modulation(cvec, w_ada, b_ada).reshape(DEPTH, MOD_ROWS, MOD_CHUNKS, D_MODEL)
    tabs = _rope_tables()

    tm = 512
    tm_mlp, tf_mlp = 1024, 512
    ctx_row = lambda tm: (lambda i: 0)
    lat_row = lambda tm: (lambda i: 1 + (i * tm) // DEC_SEQ)
    caches = {'a_k': cache_a_k, 'a_v': cache_a_v, 'mla': cache_mla, 'c_k': cache_c_k, 'c_v': cache_c_v,
              'd_k': cache_d_k, 'd_v': cache_d_v}

    xc = x_prompt.reshape(BATCH * SEQ, D_MODEL)
    xl = x_sample.reshape(DEC_BATCH * DEC_SEQ, D_MODEL)
    states = ()
    for l in range(DEPTH):
        lam_init = 0.8 - 0.6 * math.exp(-0.3 * l)
        p = {'a_sink': a_sink[l], 'b_g_kv': b_g_kv[l][None], 'w_uk': w_uk_b[l], 'w_uv': w_uv_b[l],
             'c_gq': c_gq[l][None], 'c_gk': c_gk[l][None], 'd_lam': d_lam[l], 'd_g_out': d_g_out[l][None]}
        g_pre, g_post = g_attn_pre[l][None], g_attn_post[l][None]
        g_mpre, g_mpost = g_mlp_pre[l][None], g_mlp_post[l][None]

        outs = _ctx_layer(xc, mod, l, g_pre, w_in_p, p, lam_init, states)
        states = tuple(outs[4:])
        xc, hc = _out_projection(outs[:4], w_out_b, xc, mod, ctx_row(tm), l, g_post, g_mpre, tm)
        xc = _mlp(hc, xc, mod, ctx_row(tm_mlp), l, g_mpost, w_up_b, w_down_b, tm_mlp, tf_mlp)

        zl = _in_projection(xl, mod, lat_row(tm), l, g_pre, w_in_p, tm)
        next_w = (w_out, w_down, w_up) if l + 1 < DEPTH else None
        o_parts, next_b = _lat_mixers(zl, caches, l, p, tabs, lam_init, next_w)
        xl, hl = _out_projection(o_parts, w_out_b, xl, mod, lat_row(tm), l, g_post, g_mpre, tm)
        xl = _mlp(hl, xl, mod, lat_row(tm_mlp), l, g_mpost, w_up_b, w_down_b, tm_mlp, tf_mlp)
        if next_b is not None:
            w_out_b, w_down_b, w_up_b = next_b

    y_prompt = xc.reshape(BATCH, SEQ, D_MODEL)
    y_sample = xl.reshape(DEC_BATCH, DEC_SEQ, D_MODEL)
    return (y_prompt, y_sample) + states
```

```python
import functools
import math

import jax
import jax.numpy as jnp
from jax import lax
from jax.experimental import pallas as pl
from jax.experimental.pallas import tpu as pltpu

D_MODEL = 2048
BATCH = 32
SEQ = 256
DEPTH = 2
DEC_BATCH = 8
DEC_SEQ = 1024
PAST_LEN = 256
GRID_W = 64
HEAD_DIM = 128
WINDOW = 128
B_ROPE = 64
B_RANK = 256
D_HALF = 64
D_FF = 4 * D_MODEL
ROPE_BASE = 10000.0
EPS = 1e-6
MOD_CHUNKS = 6
MOD_ROWS = 16

GROUP = 512
Z_MAIN = 4608
Z_WIDTH = 4864
KEYS = PAST_LEN + DEC_SEQ

LANES = 128
VMEM_LIMIT = 56 * 1024 * 1024

BF = jnp.bfloat16
F32 = jnp.float32
NEG = -1e30
LOG2E = math.log2(math.e)


def _dot(a, b):
    return jnp.dot(a, b, preferred_element_type=F32)


def _dot_nt(a, b):
    return lax.dot_general(a, b, (((1,), (1,)), ((), ())), preferred_element_type=F32)


def _rms(x, g):
    return x * lax.rsqrt(jnp.mean(x * x, axis=-1, keepdims=True) + EPS) * g


def _params(sem):
    return pltpu.CompilerParams(dimension_semantics=sem, vmem_limit_bytes=VMEM_LIMIT)


def _with_ones(v):
    return jnp.concatenate([v, jnp.ones(v.shape, v.dtype)], axis=1)


def _attend(q, k, v1, sink2=None):
    s = _dot_nt(q, k)
    m = jnp.max(s, axis=-1, keepdims=True)
    if sink2 is not None:
        m = jnp.maximum(m, sink2)
    o = _dot(jnp.exp2(s - m).astype(BF), v1)
    l = o[:, HEAD_DIM:]
    if sink2 is not None:
        l = l + jnp.exp2(sink2 - m)
    return o[:, :HEAD_DIM] / l


def _rope(x, cos, sin_signed, half):
    lane = lax.broadcasted_iota(jnp.int32, (8, LANES), 1)
    from_up = pltpu.roll(lane, half, 1)[0:1, :] == (lane[0:1, :] ^ half)
    partner = jnp.where(from_up, pltpu.roll(x, half, 1), pltpu.roll(x, LANES - half, 1))
    return x * cos + partner * sin_signed


def _lam(lam_ref, lam_init):
    lp = lam_ref[...]
    a = jnp.sum(lp[0:1, :] * lp[1:2, :], axis=-1, keepdims=True)
    b = jnp.sum(lp[2:3, :] * lp[3:4, :], axis=-1, keepdims=True)
    return jnp.exp(a) - jnp.exp(b) + lam_init


def _mod_kernel(c_ref, w_ref, b_ref, o_ref):
    c = c_ref[...]
    s = c * jax.nn.sigmoid(c)
    o_ref[...] = _dot(s.astype(BF), w_ref[...].astype(BF)) + b_ref[...]


def _modulation(cvec, w_ada, b_ada):
    tn = 1024
    n = MOD_CHUNKS * D_MODEL
    return pl.pallas_call(
        _mod_kernel,
        grid=(DEPTH, n // tn),
        in_specs=[
            pl.BlockSpec((MOD_ROWS, D_MODEL), lambda l, j: (0, 0)),
            pl.BlockSpec((None, D_MODEL, tn), lambda l, j: (l, 0, j)),
            pl.BlockSpec((None, 1, tn), lambda l, j: (l, 0, j)),
        ],
        out_specs=pl.BlockSpec((None, MOD_ROWS, tn), lambda l, j: (l, 0, j)),
        out_shape=jax.ShapeDtypeStruct((DEPTH, MOD_ROWS, n), F32),
        compiler_params=_params(("arbitrary", "arbitrary")),
        name="modulation",
    )(cvec, w_ada, b_ada.reshape(DEPTH, 1, n))


W_IN_COLS = 4672
KR_COL = 2048


def _regroup_kernel(w_ref, o_ref):
    o_ref[:, 0:KR_COL] = w_ref[:, 0:KR_COL]
    o_ref[:, KR_COL:Z_MAIN] = w_ref[:, KR_COL + B_ROPE:W_IN_COLS]
    o_ref[:, Z_MAIN:Z_MAIN + B_ROPE] = w_ref[:, KR_COL:KR_COL + B_ROPE]
    o_ref[:, Z_MAIN + B_ROPE:] = jnp.zeros((w_ref.shape[0], Z_WIDTH - Z_MAIN - B_ROPE), BF)


def _regroup_w_in(w_in_b):
    tr = 256
    return pl.pallas_call(
        _regroup_kernel,
        grid=(DEPTH, D_MODEL // tr),
        in_specs=[pl.BlockSpec((None, tr, W_IN_COLS), lambda l, i: (l, i, 0))],
        out_specs=pl.BlockSpec((None, tr, Z_WIDTH), lambda l, i: (l, i, 0)),
        out_shape=jax.ShapeDtypeStruct((DEPTH, D_MODEL, Z_WIDTH), BF),
        compiler_params=_params(("arbitrary", "arbitrary")),
        name="regroup_w_in",
    )(w_in_b)


def _also_cast(kernel, n_in, n_cast):
    def wrapped(*refs):
        for k in range(n_cast):
            refs[n_in + n_cast + 1 + k][...] = refs[n_in + k][...].astype(BF)
        kernel(*refs[:n_in], refs[n_in + n_cast], *refs[n_in + 2 * n_cast + 1:])
    return wrapped


SUB = 256


def _sub_tiles(rows, sub=SUB):
    return [slice(s, s + sub) for s in range(0, rows, sub)]


def _inproj_kernel(x_ref, mod_ref, g_ref, w_ref, z_ref):
    for rows in _sub_tiles(x_ref.shape[0]):
        h = _rms(x_ref[rows, :], g_ref[...]) * (1.0 + mod_ref[1:2, :]) + mod_ref[0:1, :]
        z_ref[rows, :] = _dot(h.astype(BF), w_ref[...])


def _in_projection(x, mod, mod_row, l, g, w, tm):
    m = x.shape[0]
    return pl.pallas_call(
        _inproj_kernel,
        grid=(m // tm,),
        in_specs=[
            pl.BlockSpec((tm, D_MODEL), lambda i: (i, 0)),
            pl.BlockSpec((None, None, MOD_CHUNKS, D_MODEL), lambda i: (l, mod_row(i), 0, 0)),
            pl.BlockSpec((1, D_MODEL), lambda i: (0, 0)),
            pl.BlockSpec((None, D_MODEL, Z_WIDTH), lambda i: (l, 0, 0), pipeline_mode=pl.Buffered(1)),
        ],
        out_specs=pl.BlockSpec((tm, Z_WIDTH), lambda i: (i, 0)),
        out_shape=jax.ShapeDtypeStruct((m, Z_WIDTH), F32),
        compiler_params=_params(("arbitrary",)),
        name="in_projection",
    )(x, mod, g, w)


def _outproj_kernel(oa_ref, ob_ref, oc_ref, od_ref, w_ref, x_ref, mod_ref, gpost_ref, gpre_ref, y_ref, h_ref):
    for rows in _sub_tiles(x_ref.shape[0], SUB // 2):
        o = jnp.concatenate([oa_ref[rows, :], ob_ref[rows, :], oc_ref[rows, :], od_ref[rows, :]], axis=1)
        y = x_ref[rows, :] + mod_ref[2:3, :] * _rms(_dot(o, w_ref[...]), gpost_ref[...])
        y_ref[rows, :] = y
        h_ref[rows, :] = (_rms(y, gpre_ref[...]) * (1.0 + mod_ref[4:5, :]) + mod_ref[3:4, :]).astype(BF)


def _out_projection(o_parts, w, x, mod, mod_row, l, gpost, gpre, tm):
    m = x.shape[0]
    o_spec = pl.BlockSpec((tm, GROUP), lambda i: (i, 0))
    row_spec = pl.BlockSpec((tm, D_MODEL), lambda i: (i, 0))
    return pl.pallas_call(
        _outproj_kernel,
        grid=(m // tm,),
        in_specs=[
            o_spec, o_spec, o_spec, o_spec,
            pl.BlockSpec((D_MODEL, D_MODEL), lambda i: (0, 0)),
            row_spec,
            pl.BlockSpec((None, None, MOD_CHUNKS, D_MODEL), lambda i: (l, mod_row(i), 0, 0)),
            pl.BlockSpec((1, D_MODEL), lambda i: (0, 0)),
            pl.BlockSpec((1, D_MODEL), lambda i: (0, 0)),
        ],
        out_specs=[row_spec, row_spec],
        out_shape=[jax.ShapeDtypeStruct((m, D_MODEL), F32), jax.ShapeDtypeStruct((m, D_MODEL), BF)],
        compiler_params=_params(("arbitrary",)),
        name="out_projection",
    )(*o_parts, w, x, mod, gpost, gpre)


def _mlp_kernel(h_ref, x_ref, mod_ref, gpost_ref, wu_ref, wd_ref, y_ref):
    k = pl.program_id(1)
    last = pl.num_programs(1) - 1

    def ff(rows):
        u = jnp.square(jnp.maximum(_dot(h_ref[rows, :], wu_ref[...]), 0.0))
        return _dot(u.astype(BF), wd_ref[...])

    @pl.when(k == 0)
    def _():
        y_ref[...] = ff(slice(None))

    @pl.when((k > 0) & (k < last))
    def _():
        y_ref[...] += ff(slice(None))

    @pl.when(k == last)
    def _():
        for rows in _sub_tiles(x_ref.shape[0], SUB // 2):
            acc = y_ref[rows, :] + ff(rows)
            y_ref[rows, :] = x_ref[rows, :] + mod_ref[5:6, :] * _rms(acc, gpost_ref[...])


def _mlp(h, x, mod, mod_row, l, gpost, wu, wd, tm, tf):
    m = x.shape[0]
    assert D_FF // tf >= 2
    row_spec = pl.BlockSpec((tm, D_MODEL), lambda i, k: (i, 0))
    return pl.pallas_call(
        _mlp_kernel,
        grid=(m // tm, D_FF // tf),
        in_specs=[
            row_spec, row_spec,
            pl.BlockSpec((None, None, MOD_CHUNKS, D_MODEL), lambda i, k: (l, mod_row(i), 0, 0)),
            pl.BlockSpec((1, D_MODEL), lambda i, k: (0, 0)),
            pl.BlockSpec((D_MODEL, tf), lambda i, k: (0, k)),
            pl.BlockSpec((tf, D_MODEL), lambda i, k: (k, 0)),
        ],
        out_specs=row_spec,
        out_shape=jax.ShapeDtypeStruct((m, D_MODEL), F32),
        compiler_params=_params(("arbitrary", "arbitrary")),
        name="mlp",
    )(h, x, mod, gpost, wu, wd)


N_STATES = 7
STATE_HEADS = (2, 2, None, 2, 2, 4, 4)
RPS = 2


def _ctx_mixer_body(lam_init, z_ref, sink_ref, gkv_ref, wuk_ref, wuv_ref, gq_ref, gk_ref, lam_ref, gout_ref,
                    o_refs, st_refs):
    oa_ref, ob_ref, oc_ref, od_ref = o_refs
    ka_ref, va_ref, mla_ref, kc_ref, vc_ref, kd_ref, vd_ref = st_refs
    hd = HEAD_DIM

    c = HEAD_DIM ** -0.5 * LOG2E
    for hk in range(2):
        k = z_ref[:, 512 + hk * hd:512 + (hk + 1) * hd]
        v = z_ref[:, 768 + hk * hd:768 + (hk + 1) * hd]
        ka_ref[hk] = k
        va_ref[hk] = v
        kb, v1 = k.astype(BF), _with_ones(v.astype(BF))
        for g in range(2):
            h = hk * 2 + g
            q = (z_ref[:, h * hd:(h + 1) * hd] * c).astype(BF)
            oa_ref[:, h * hd:(h + 1) * hd] = _attend(q, kb, v1, sink_ref[h] * LOG2E).astype(BF)

    c = (HEAD_DIM + B_ROPE) ** -0.5 * LOG2E
    c_lat = _rms(z_ref[:, 1792:2048], gkv_ref[...])
    kr = z_ref[:, Z_MAIN:Z_MAIN + B_ROPE]
    mla_ref[:, 0:B_RANK] = c_lat
    mla_ref[:, B_RANK:B_RANK + B_ROPE] = kr
    cb, krb = c_lat.astype(BF), kr.astype(BF)
    for h in range(4):
        kn = _dot(cb, wuk_ref[:, h * hd:(h + 1) * hd]).astype(BF)
        v1 = _with_ones(_dot(cb, wuv_ref[:, h * hd:(h + 1) * hd]).astype(BF))
        qn = (z_ref[:, 1024 + h * hd:1024 + (h + 1) * hd] * c).astype(BF)
        qr = (z_ref[:, 1536 + h * B_ROPE:1536 + (h + 1) * B_ROPE] * c).astype(BF)
        s = _dot_nt(qn, kn) + _dot_nt(qr, krb)
        m = jnp.max(s, axis=-1, keepdims=True)
        o = _dot(jnp.exp2(s - m).astype(BF), v1)
        ob_ref[:, h * hd:(h + 1) * hd] = (o[:, :hd] / o[:, hd:]).astype(BF)

    c = HEAD_DIM ** -0.5 * LOG2E
    for hk in range(2):
        k = _rms(z_ref[:, 2560 + hk * hd:2560 + (hk + 1) * hd], gk_ref[...])
        v = z_ref[:, 2816 + hk * hd:2816 + (hk + 1) * hd]
        kc_ref[hk] = k
        vc_ref[hk] = v
        kb, v1 = k.astype(BF), _with_ones(v.astype(BF))
        for g in range(2):
            h = hk * 2 + g
            q = (_rms(z_ref[:, 2048 + h * hd:2048 + (h + 1) * hd], gq_ref[...]) * c).astype(BF)
            oc_ref[:, h * hd:(h + 1) * hd] = _attend(q, kb, v1).astype(BF)

    c = D_HALF ** -0.5 * LOG2E
    lam = _lam(lam_ref, lam_init)
    lane = lax.broadcasted_iota(jnp.int32, (SEQ, hd), 1)
    for h in range(4):
        k = z_ref[:, 3584 + h * hd:3584 + (h + 1) * hd]
        v = z_ref[:, 4096 + h * hd:4096 + (h + 1) * hd]
        kd_ref[h] = k
        vd_ref[h] = v
        kb, v1 = k.astype(BF), _with_ones(v.astype(BF))
        q = z_ref[:, 3072 + h * hd:3072 + (h + 1) * hd] * c
        o1 = _attend(jnp.where(lane < D_HALF, q, 0.0).astype(BF), kb, v1)
        o2 = _attend(jnp.where(lane >= D_HALF, q, 0.0).astype(BF), kb, v1)
        od_ref[:, h * hd:(h + 1) * hd] = (_rms(o1 - lam * o2, gout_ref[...]) * (1.0 - lam_init)).astype(BF)


def _ctx_layer_kernel(lam_init, n_alias, x_ref, mod_ref, g_ref, w_ref, sink_ref, gkv_ref, wuk_ref, wuv_ref, gq_ref, gk_ref,
                      lam_ref, gout_ref, *refs):
    refs = refs[n_alias:]
    o_refs, st_refs, z_s = refs[:4], refs[4:4 + N_STATES], refs[4 + N_STATES]
    for r in range(RPS):
        rows = slice(r * SEQ, (r + 1) * SEQ)
        h = _rms(x_ref[rows, :], g_ref[...]) * (1.0 + mod_ref[1:2, :]) + mod_ref[0:1, :]
        z_s[rows, :] = _dot(h.astype(BF), w_ref[...])
    for r in range(RPS):
        rows = slice(r * SEQ, (r + 1) * SEQ)
        _ctx_mixer_body(lam_init, z_s.at[rows, :], sink_ref, gkv_ref, wuk_ref, wuv_ref, gq_ref, gk_ref, lam_ref, gout_ref,
                        [o.at[rows, :] for o in o_refs], [s.at[r] for s in st_refs])


def _const(shape):
    return pl.BlockSpec(shape, lambda *_: (0,) * len(shape))


def _ctx_layer(x, mod, l, g, w, p, lam_init, prev_states):
    m = x.shape[0]
    o_spec = pl.BlockSpec((RPS * SEQ, GROUP), lambda b: (b, 0))
    o_shape = jax.ShapeDtypeStruct((m, GROUP), BF)

    def st_dims(h):
        return (SEQ, B_RANK + B_ROPE) if h is None else (h, SEQ, HEAD_DIM)

    def st_spec(h):
        dims = st_dims(h)
        return pl.BlockSpec((RPS, None) + dims, lambda b: (b, l) + (0,) * len(dims))

    in_specs = [
        pl.BlockSpec((RPS * SEQ, D_MODEL), lambda b: (b, 0)),
        pl.BlockSpec((None, None, MOD_CHUNKS, D_MODEL), lambda b: (l, 0, 0, 0)),
        _const((1, D_MODEL)),
        pl.BlockSpec((None, D_MODEL, Z_WIDTH), lambda b: (l, 0, 0), pipeline_mode=pl.Buffered(1)),
        pl.BlockSpec(memory_space=pltpu.SMEM),
        _const((1, B_RANK)), _const((B_RANK, GROUP)), _const((B_RANK, GROUP)),
        _const((1, HEAD_DIM)), _const((1, HEAD_DIM)), _const((4, D_HALF)), _const((1, HEAD_DIM)),
    ]
    n_alias = len(prev_states)
    aliases = {len(in_specs) + k: 4 + k for k in range(n_alias)}
    return pl.pallas_call(
        functools.partial(_ctx_layer_kernel, lam_init, n_alias),
        grid=(BATCH // RPS,),
        in_specs=in_specs + [pl.BlockSpec(memory_space=pl.ANY)] * n_alias,
        out_specs=[o_spec] * 4 + [st_spec(h) for h in STATE_HEADS],
        out_shape=[o_shape] * 4 + [jax.ShapeDtypeStruct((BATCH, DEPTH) + st_dims(h), F32) for h in STATE_HEADS],
        scratch_shapes=[pltpu.VMEM((RPS * SEQ, Z_WIDTH), F32)],
        input_output_aliases=aliases,
        compiler_params=_params(("arbitrary",)),
        name="ctx_layer",
    )(x, mod, g, w, p['a_sink'], p['b_g_kv'], p['w_uk'], p['w_uv'], p['c_gq'], p['c_gk'], p['d_lam'], p['d_g_out'],
      *prev_states)


QBLK = 256
N_QBLK = DEC_SEQ // QBLK
N_WBLK = DEC_SEQ // WINDOW


def _lat_a_kernel(z_ref, ck_ref, cv_ref, cos_ref, sin_ref, sink_ref, o_ref, q_s, k_s, v_s):
    hd = HEAD_DIM
    cos, sin = cos_ref[...], sin_ref[...]
    c = HEAD_DIM ** -0.5 * LOG2E
    for hk in range(2):
        k_s[hk, 0:WINDOW, :] = jnp.zeros((WINDOW, hd), BF)
        k_s[hk, WINDOW + DEC_SEQ:, :] = jnp.zeros((WINDOW, hd), BF)
        v_s[hk, 0:WINDOW, :] = jnp.zeros((WINDOW, 2 * hd), BF)
        v_s[hk, WINDOW + DEC_SEQ:, :] = jnp.zeros((WINDOW, 2 * hd), BF)
        k = _rope(z_ref[:, 512 + hk * hd:512 + (hk + 1) * hd], cos, sin, 32)
        k_s[hk, WINDOW:WINDOW + DEC_SEQ, :] = k.astype(BF)
        v_s[hk, WINDOW:WINDOW + DEC_SEQ, :] = _with_ones(z_ref[:, 768 + hk * hd:768 + (hk + 1) * hd].astype(BF))
        for g in range(2):
            h = hk * 2 + g
            q = (_rope(z_ref[:, h * hd:(h + 1) * hd], cos, sin, 32) * c).astype(BF)
            for i in range(N_WBLK):
                q_s[hk, i, g * WINDOW:(g + 1) * WINDOW, :] = q[i * WINDOW:(i + 1) * WINDOW, :]

    r_io = lax.broadcasted_iota(jnp.int32, (2 * WINDOW, 3 * WINDOW), 0) & (WINDOW - 1)
    c_io = lax.broadcasted_iota(jnp.int32, (2 * WINDOW, 3 * WINDOW), 1)
    second = lax.broadcasted_iota(jnp.int32, (2 * WINDOW, 1), 0) >= WINDOW
    for hk in range(2):
        kc = ck_ref[hk].astype(BF)
        vc1 = _with_ones(cv_ref[hk].astype(BF))
        sink2 = jnp.where(second, sink_ref[2 * hk + 1] * LOG2E, sink_ref[2 * hk] * LOG2E)
        for i in range(N_WBLK):
            r0 = i * WINDOW
            q = q_s[hk, i]
            kl = k_s[hk, r0:r0 + 3 * WINDOW, :]
            vl1 = v_s[hk, r0:r0 + 3 * WINDOW, :]
            kpos = (i - 1) * WINDOW + c_io
            qpos = i * WINDOW + r_io
            ok = (jnp.abs(qpos - kpos) <= WINDOW) & (kpos >= 0) & (kpos < DEC_SEQ)
            sc = _dot_nt(q, kc)
            sl = _dot_nt(q, kl)
            m = jnp.maximum(jnp.maximum(jnp.max(sc, axis=-1, keepdims=True),
                                        jnp.max(jnp.where(ok, sl, NEG), axis=-1, keepdims=True)), sink2)
            pw = jnp.where(ok, jnp.exp2(sl - m), 0.0)
            o = _dot(jnp.exp2(sc - m).astype(BF), vc1) + _dot(pw.astype(BF), vl1)
            o = (o[:, :hd] / (o[:, hd:] + jnp.exp2(sink2 - m))).astype(BF)
            o_ref[r0:r0 + WINDOW, 2 * hk * hd:(2 * hk + 1) * hd] = o[0:WINDOW, :]
            o_ref[r0:r0 + WINDOW, (2 * hk + 1) * hd:(2 * hk + 2) * hd] = o[WINDOW:, :]


def _lat_b_kernel(z_ref, zr_ref, mla_ref, gkv_ref, wuk_ref, wuv_ref, cos_ref, sin_ref, o_ref, c_s, kr_s, k_s, v_s, q_s):
    hd = HEAD_DIM
    cos, sin = cos_ref[...], sin_ref[...]
    c = (HEAD_DIM + B_ROPE) ** -0.5 * LOG2E
    c_s[0:PAST_LEN, :] = mla_ref[:, 0:B_RANK].astype(BF)
    c_s[PAST_LEN:, :] = _rms(z_ref[:, 768:1024], gkv_ref[...]).astype(BF)
    kr_s[0:PAST_LEN, B_ROPE:] = jnp.zeros((PAST_LEN, hd - B_ROPE), BF)
    kr_s[0:PAST_LEN, 0:B_ROPE] = mla_ref[:, B_RANK:B_RANK + B_ROPE].astype(BF)
    kr_s[PAST_LEN:, :] = _rope(zr_ref[...], cos, sin, 16).astype(BF)
    lane = lax.broadcasted_iota(jnp.int32, (DEC_SEQ, hd), 1)
    for h in range(4):
        k_s[h, :, 0:hd] = _dot(c_s[...], wuk_ref[:, h * hd:(h + 1) * hd]).astype(BF)
        k_s[h, :, hd:2 * hd] = kr_s[...]
        v_s[h] = _with_ones(_dot(c_s[...], wuv_ref[:, h * hd:(h + 1) * hd]).astype(BF))
        q_s[h, :, 0:hd] = (z_ref[:, h * hd:(h + 1) * hd] * c).astype(BF)
        qr = _rope(z_ref[:, 512 + (h // 2) * hd:512 + (h // 2 + 1) * hd], cos, sin, 16) * c
        if h % 2:
            qr = pltpu.roll(qr, B_ROPE, 1)
        q_s[h, :, hd:2 * hd] = jnp.where(lane < B_ROPE, qr, 0.0).astype(BF)
    for h in range(4):
        for i in range(N_QBLK):
            r0 = i * QBLK
            o = _attend(q_s[h, r0:r0 + QBLK, :], k_s[h], v_s[h])
            o_ref[r0:r0 + QBLK, h * hd:(h + 1) * hd] = o.astype(BF)


def _lat_c_kernel(z_ref, ck_ref, cv_ref, gq_ref, gk_ref, cos_ref, sin_ref, o_ref, k_s, v_s, q_s):
    hd = HEAD_DIM
    cos, sin = cos_ref[...], sin_ref[...]
    c = HEAD_DIM ** -0.5 * LOG2E
    for hk in range(2):
        k_s[hk, 0:PAST_LEN, :] = ck_ref[hk].astype(BF)
        k = _rms(z_ref[:, 512 + hk * hd:512 + (hk + 1) * hd], gk_ref[...])
        k_s[hk, PAST_LEN:, :] = _rope(k, cos, sin, 32).astype(BF)
        v_s[hk, 0:PAST_LEN, :] = _with_ones(cv_ref[hk].astype(BF))
        v_s[hk, PAST_LEN:, :] = _with_ones(z_ref[:, 768 + hk * hd:768 + (hk + 1) * hd].astype(BF))
    for h in range(4):
        q = _rms(z_ref[:, h * hd:(h + 1) * hd], gq_ref[...])
        q_s[h] = (_rope(q, cos, sin, 32) * c).astype(BF)
    for h in range(4):
        for i in range(N_QBLK):
            r0 = i * QBLK
            o = _attend(q_s[h, r0:r0 + QBLK, :], k_s[h // 2], v_s[h // 2])
            o_ref[r0:r0 + QBLK, h * hd:(h + 1) * hd] = o.astype(BF)


def _lat_d_kernel(lam_init, z_ref, dk_ref, dv_ref, lam_ref, gout_ref, cos_ref, sin_ref, o_ref, k_s, v_s, q1_s, q2_s):
    hd = HEAD_DIM
    cos, sin = cos_ref[...], sin_ref[...]
    c = D_HALF ** -0.5 * LOG2E
    lam = _lam(lam_ref, lam_init)
    lane = lax.broadcasted_iota(jnp.int32, (DEC_SEQ, hd), 1)
    for h in range(4):
        k_s[h, 0:PAST_LEN, :] = dk_ref[h].astype(BF)
        k_s[h, PAST_LEN:, :] = _rope(z_ref[:, 512 + h * hd:512 + (h + 1) * hd], cos, sin, 16).astype(BF)
        v_s[h, 0:PAST_LEN, :] = _with_ones(dv_ref[h].astype(BF))
        v_s[h, PAST_LEN:, :] = _with_ones(z_ref[:, 1024 + h * hd:1024 + (h + 1) * hd].astype(BF))
        q = _rope(z_ref[:, h * hd:(h + 1) * hd], cos, sin, 16) * c
        q1_s[h] = jnp.where(lane < D_HALF, q, 0.0).astype(BF)
        q2_s[h] = jnp.where(lane >= D_HALF, q, 0.0).astype(BF)
    for h in range(4):
        for i in range(N_QBLK):
            r0 = i * QBLK
            o1 = _attend(q1_s[h, r0:r0 + QBLK, :], k_s[h], v_s[h])
            o2 = _attend(q2_s[h, r0:r0 + QBLK, :], k_s[h], v_s[h])
            o = _rms(o1 - lam * o2, gout_ref[...]) * (1.0 - lam_init)
            o_ref[r0:r0 + QBLK, h * hd:(h + 1) * hd] = o.astype(BF)


def _lat_mixers(z, caches, l, p, tabs, lam_init, casts):
    m = z.shape[0]
    hd = HEAD_DIM
    cos128, sin128, cos64, sin64 = tabs
    o_spec = pl.BlockSpec((DEC_SEQ, GROUP), lambda b: (b, 0))
    o_shape = jax.ShapeDtypeStruct((m, GROUP), BF)
    tab = _const((DEC_SEQ, hd))

    def cache_spec(h):
        return pl.BlockSpec((None, None, h, PAST_LEN, hd), lambda b: (b, l, 0, 0, 0))

    def zspec(width, col0):
        return pl.BlockSpec((DEC_SEQ, width), lambda b: (b, col0 // width))

    def scratch(*shape):
        return pltpu.VMEM(shape, BF)

    def cast_specs(w, layer, axis):
        r, c = w.shape[1:]
        if axis == 'rows':
            blk = (r // DEC_BATCH, c)
            return pl.BlockSpec((None,) + blk, lambda b: (layer, b, 0)), pl.BlockSpec(blk, lambda b: (b, 0))
        blk = (r, c // DEC_BATCH)
        return pl.BlockSpec((None,) + blk, lambda b: (layer, 0, b)), pl.BlockSpec(blk, lambda b: (0, b))

    def call(kernel, name, in_specs, args, scratch_shapes):
        todo = casts.get(name, [])
        specs = [cast_specs(*t) for t in todo]
        if todo:
            kernel = _also_cast(kernel, len(in_specs), len(todo))
        outs = pl.pallas_call(
            kernel, grid=(DEC_BATCH,),
            in_specs=in_specs + [s[0] for s in specs],
            out_specs=[o_spec] + [s[1] for s in specs],
            out_shape=[o_shape] + [jax.ShapeDtypeStruct(t[0].shape[1:], BF) for t in todo],
            scratch_shapes=scratch_shapes, compiler_params=_params(("arbitrary",)), name=name,
        )(*args, *[t[0] for t in todo])
        return outs[0], list(outs[1:])

    o_a, c_a = call(_lat_a_kernel, "lat_mixer_a",
                    [zspec(1024, 0), cache_spec(2), cache_spec(2), tab, tab, pl.BlockSpec(memory_space=pltpu.SMEM)],
                    (z, caches['a_k'], caches['a_v'], cos128, sin128, p['a_sink']),
                    [scratch(2, N_WBLK, 2 * WINDOW, hd), scratch(2, DEC_SEQ + 2 * WINDOW, hd),
                     scratch(2, DEC_SEQ + 2 * WINDOW, 2 * hd)])

    o_b, c_b = call(_lat_b_kernel, "lat_mixer_b",
                    [zspec(1024, 1024), zspec(LANES, Z_MAIN),
                     pl.BlockSpec((None, None, PAST_LEN, B_RANK + B_ROPE), lambda b: (b, l, 0, 0)),
                     _const((1, B_RANK)), _const((B_RANK, GROUP)), _const((B_RANK, GROUP)), tab, tab],
                    (z, z, caches['mla'], p['b_g_kv'], p['w_uk'], p['w_uv'], cos64, sin64),
                    [scratch(KEYS, B_RANK), scratch(KEYS, hd), scratch(4, KEYS, 2 * hd), scratch(4, KEYS, 2 * hd),
                     scratch(4, DEC_SEQ, 2 * hd)])

    o_c, c_c = call(_lat_c_kernel, "lat_mixer_c",
                    [zspec(1024, 2048), cache_spec(2), cache_spec(2), _const((1, hd)), _const((1, hd)), tab, tab],
                    (z, caches['c_k'], caches['c_v'], p['c_gq'], p['c_gk'], cos128, sin128),
                    [scratch(2, KEYS, hd), scratch(2, KEYS, 2 * hd), scratch(4, DEC_SEQ, hd)])

    o_d, c_d = call(functools.partial(_lat_d_kernel, lam_init), "lat_mixer_d",
                    [zspec(1536, 3072), cache_spec(4), cache_spec(4), _const((4, D_HALF)), _const((1, hd)), tab, tab],
                    (z, caches['d_k'], caches['d_v'], p['d_lam'], p['d_g_out'], cos64, sin64),
                    [scratch(4, KEYS, hd), scratch(4, KEYS, 2 * hd), scratch(4, DEC_SEQ, hd), scratch(4, DEC_SEQ, hd)])

    return (o_a, o_b, o_c, o_d), {"lat_mixer_a": c_a, "lat_mixer_b": c_b, "lat_mixer_c": c_c, "lat_mixer_d": c_d}


def _rope_tables():
    t = jnp.arange(DEC_SEQ)
    row = (t // GRID_W).astype(F32)[:, None]
    col = (t % GRID_W).astype(F32)[:, None]

    def table(half):
        inv = ROPE_BASE ** (-jnp.arange(half, dtype=F32) / half)
        ar, ac = row * inv[None, :], col * inv[None, :]
        cos = jnp.concatenate([jnp.cos(ar), jnp.cos(ar), jnp.cos(ac), jnp.cos(ac)], axis=-1)
        sin = jnp.concatenate([-jnp.sin(ar), jnp.sin(ar), -jnp.sin(ac), jnp.sin(ac)], axis=-1)
        return cos, sin

    cos128, sin128 = table(32)
    cos64, sin64 = table(16)
    return cos128, sin128, jnp.tile(cos64, (1, 2)), jnp.tile(sin64, (1, 2))


def kernel(x_prompt, x_sample, cache_a_k, cache_a_v, cache_mla, cache_c_k, cache_c_v, cache_d_k, cache_d_v,
           c, c_ctx, w_ada, b_ada, g_attn_pre, g_attn_post, g_mlp_pre, g_mlp_post, w_in, a_sink,
           b_g_kv, b_w_uk, b_w_uv, c_gq, c_gk, d_lam, d_g_out, w_out, w_up, w_down):
    w_in_p = _regroup_w_in(w_in.astype(BF))
    w_uk_b = b_w_uk.reshape(DEPTH, B_RANK, GROUP).astype(BF)
    w_uv_b = b_w_uv.reshape(DEPTH, B_RANK, GROUP).astype(BF)
    assert DEPTH == 2
    first_layer_casts = {"lat_mixer_a": [(w_out, 1, 'rows'), (w_down, 0, 'rows')],
                         "lat_mixer_b": [(w_up, 0, 'cols'), (w_out, 0, 'rows')],
                         "lat_mixer_c": [(w_down, 1, 'rows')],
                         "lat_mixer_d": [(w_up, 1, 'cols')]}

    cvec = jnp.concatenate([c_ctx[None, :], c, jnp.zeros((MOD_ROWS - 1 - DEC_BATCH, D_MODEL), F32)], axis=0)
    mod = _modulation(cvec, w_ada, b_ada).reshape(DEPTH, MOD_ROWS, MOD_CHUNKS, D_MODEL)
    tabs = _rope_tables()

    tm = 512
    tm_mlp, tf_mlp = 1024, 512
    ctx_row = lambda tm: (lambda i: 0)
    lat_row = lambda tm: (lambda i: 1 + (i * tm) // DEC_SEQ)
    caches = {'a_k': cache_a_k, 'a_v': cache_a_v, 'mla': cache_mla, 'c_k': cache_c_k, 'c_v': cache_c_v,
              'd_k': cache_d_k, 'd_v': cache_d_v}

    xc = x_prompt.reshape(BATCH * SEQ, D_MODEL)
    xl = x_sample.reshape(DEC_BATCH * DEC_SEQ, D_MODEL)
    states = ()
    w_out_b, w_down_b, w_up_b = {}, {}, {}
    for l in range(DEPTH):
        lam_init = 0.8 - 0.6 * math.exp(-0.3 * l)
        p = {'a_sink': a_sink[l], 'b_g_kv': b_g_kv[l][None], 'w_uk': w_uk_b[l], 'w_uv': w_uv_b[l],
             'c_gq': c_gq[l][None], 'c_gk': c_gk[l][None], 'd_lam': d_lam[l], 'd_g_out': d_g_out[l][None]}
        g_pre, g_post = g_attn_pre[l][None], g_attn_post[l][None]
        g_mpre, g_mpost = g_mlp_pre[l][None], g_mlp_post[l][None]

        zl = _in_projection(xl, mod, lat_row(tm), l, g_pre, w_in_p, tm)
        o_parts, cast = _lat_mixers(zl, caches, l, p, tabs, lam_init, first_layer_casts if l == 0 else {})
        if l == 0:
            (w_out_b[1], w_down_b[0]), (w_up_b[0], w_out_b[0]) = cast["lat_mixer_a"], cast["lat_mixer_b"]
            (w_down_b[1],), (w_up_b[1],) = cast["lat_mixer_c"], cast["lat_mixer_d"]

        outs = _ctx_layer(xc, mod, l, g_pre, w_in_p, p, lam_init, states)
        states = tuple(outs[4:])
        xc, hc = _out_projection(outs[:4], w_out_b[l], xc, mod, ctx_row(tm), l, g_post, g_mpre, tm)
        xc = _mlp(hc, xc, mod, ctx_row(tm_mlp), l, g_mpost, w_up_b[l], w_down_b[l], tm_mlp, tf_mlp)

        xl, hl = _out_projection(o_parts, w_out_b[l], xl, mod, lat_row(tm), l, g_post, g_mpre, tm)
        xl = _mlp(hl, xl, mod, lat_row(tm_mlp), l, g_mpost, w_up_b[l], w_down_b[l], tm_mlp, tf_mlp)

    y_prompt = xc.reshape(BATCH, SEQ, D_MODEL)
    y_sample = xl.reshape(DEC_BATCH, DEC_SEQ, D_MODEL)
    return (y_prompt, y_sample) + states
```

```python
import functools
import math

import jax
import jax.numpy as jnp
from jax import lax
from jax.experimental import pallas as pl
from jax.experimental.pallas import tpu as pltpu

D_MODEL = 2048
BATCH = 32
SEQ = 256
DEPTH = 2
DEC_BATCH = 8
DEC_SEQ = 1024
PAST_LEN = 256
GRID_W = 64
HEAD_DIM = 128
WINDOW = 128
B_ROPE = 64
B_RANK = 256
D_HALF = 64
D_FF = 4 * D_MODEL
ROPE_BASE = 10000.0
EPS = 1e-6
MOD_CHUNKS = 6
MOD_ROWS = 16

GROUP = 512
Z_MAIN = 4608
Z_WIDTH = 4864
KEYS = PAST_LEN + DEC_SEQ

LANES = 128
VMEM_LIMIT = 56 * 1024 * 1024

BF = jnp.bfloat16
F32 = jnp.float32
NEG = -1e30
LOG2E = math.log2(math.e)


def _dot(a, b):
    return jnp.dot(a, b, preferred_element_type=F32)


def _dot_nt(a, b):
    return lax.dot_general(a, b, (((1,), (1,)), ((), ())), preferred_element_type=F32)


def _rms(x, g):
    return x * lax.rsqrt(jnp.mean(x * x, axis=-1, keepdims=True) + EPS) * g


def _params(sem):
    return pltpu.CompilerParams(dimension_semantics=sem, vmem_limit_bytes=VMEM_LIMIT)


def _with_ones(v):
    return jnp.concatenate([v, jnp.ones(v.shape, v.dtype)], axis=1)


def _attend(q, k, v1, sink2=None):
    s = _dot_nt(q, k)
    m = jnp.max(s, axis=-1, keepdims=True)
    if sink2 is not None:
        m = jnp.maximum(m, sink2)
    o = _dot(jnp.exp2(s - m).astype(BF), v1)
    l = o[:, HEAD_DIM:]
    if sink2 is not None:
        l = l + jnp.exp2(sink2 - m)
    return o[:, :HEAD_DIM] / l


def _rope(x, cos, sin_signed, half):
    lane = lax.broadcasted_iota(jnp.int32, (8, LANES), 1)
    from_up = pltpu.roll(lane, half, 1)[0:1, :] == (lane[0:1, :] ^ half)
    partner = jnp.where(from_up, pltpu.roll(x, half, 1), pltpu.roll(x, LANES - half, 1))
    return x * cos + partner * sin_signed


def _lam(lam_ref, lam_init):
    lp = lam_ref[...]
    a = jnp.sum(lp[0:1, :] * lp[1:2, :], axis=-1, keepdims=True)
    b = jnp.sum(lp[2:3, :] * lp[3:4, :], axis=-1, keepdims=True)
    return jnp.exp(a) - jnp.exp(b) + lam_init


def _mod_kernel(c_ref, w_ref, b_ref, o_ref):
    c = c_ref[...]
    s = c * jax.nn.sigmoid(c)
    o_ref[...] = _dot(s.astype(BF), w_ref[...].astype(BF)) + b_ref[...]


def _modulation(cvec, w_ada, b_ada):
    tn = 1024
    n = MOD_CHUNKS * D_MODEL
    return pl.pallas_call(
        _mod_kernel,
        grid=(DEPTH, n // tn),
        in_specs=[
            pl.BlockSpec((MOD_ROWS, D_MODEL), lambda l, j: (0, 0)),
            pl.BlockSpec((None, D_MODEL, tn), lambda l, j: (l, 0, j)),
            pl.BlockSpec((None, 1, tn), lambda l, j: (l, 0, j)),
        ],
        out_specs=pl.BlockSpec((None, MOD_ROWS, tn), lambda l, j: (l, 0, j)),
        out_shape=jax.ShapeDtypeStruct((DEPTH, MOD_ROWS, n), F32),
        compiler_params=_params(("arbitrary", "arbitrary")),
        name="modulation",
    )(cvec, w_ada, b_ada.reshape(DEPTH, 1, n))


W_IN_COLS = 4672
KR_COL = 2048


def _regroup_kernel(w_ref, o_ref):
    o_ref[:, 0:KR_COL] = w_ref[:, 0:KR_COL]
    o_ref[:, KR_COL:Z_MAIN] = w_ref[:, KR_COL + B_ROPE:W_IN_COLS]
    o_ref[:, Z_MAIN:Z_MAIN + B_ROPE] = w_ref[:, KR_COL:KR_COL + B_ROPE]
    o_ref[:, Z_MAIN + B_ROPE:] = jnp.zeros((w_ref.shape[0], Z_WIDTH - Z_MAIN - B_ROPE), BF)


def _regroup_w_in(w_in_b):
    tr = 256
    return pl.pallas_call(
        _regroup_kernel,
        grid=(DEPTH, D_MODEL // tr),
        in_specs=[pl.BlockSpec((None, tr, Z_WIDTH), lambda l, i: (l, i, 0))],
        out_specs=pl.BlockSpec((None, tr, Z_WIDTH), lambda l, i: (l, i, 0)),
        out_shape=jax.ShapeDtypeStruct((DEPTH, D_MODEL, Z_WIDTH), BF),
        compiler_params=_params(("arbitrary", "arbitrary")),
        name="regroup_w_in",
    )(w_in_b)


def _also_cast(kernel, n_in, n_cast):
    def wrapped(*refs):
        for k in range(n_cast):
            refs[n_in + n_cast + 1 + k][...] = refs[n_in + k][...].astype(BF)
        kernel(*refs[:n_in], refs[n_in + n_cast], *refs[n_in + 2 * n_cast + 1:])
    return wrapped


SUB = 256


def _sub_tiles(rows, sub=SUB):
    return [slice(s, s + sub) for s in range(0, rows, sub)]


def _inproj_kernel(x_ref, mod_ref, g_ref, w_ref, z_ref):
    for rows in _sub_tiles(x_ref.shape[0]):
        h = _rms(x_ref[rows, :], g_ref[...]) * (1.0 + mod_ref[1:2, :]) + mod_ref[0:1, :]
        z_ref[rows, :] = _dot(h.astype(BF), w_ref[...])


def _in_projection(x, mod, mod_row, l, g, w, tm):
    m = x.shape[0]
    return pl.pallas_call(
        _inproj_kernel,
        grid=(m // tm,),
        in_specs=[
            pl.BlockSpec((tm, D_MODEL), lambda i: (i, 0)),
            pl.BlockSpec((None, None, MOD_CHUNKS, D_MODEL), lambda i: (l, mod_row(i), 0, 0)),
            pl.BlockSpec((1, D_MODEL), lambda i: (0, 0)),
            pl.BlockSpec((None, D_MODEL, Z_WIDTH), lambda i: (l, 0, 0), pipeline_mode=pl.Buffered(1)),
        ],
        out_specs=pl.BlockSpec((tm, Z_WIDTH), lambda i: (i, 0)),
        out_shape=jax.ShapeDtypeStruct((m, Z_WIDTH), F32),
        compiler_params=_params(("arbitrary",)),
        name="in_projection",
    )(x, mod, g, w)


def _outproj_kernel(oa_ref, ob_ref, oc_ref, od_ref, w_ref, x_ref, mod_ref, gpost_ref, gpre_ref, y_ref, h_ref):
    for rows in _sub_tiles(x_ref.shape[0], SUB // 2):
        o = jnp.concatenate([oa_ref[rows, :], ob_ref[rows, :], oc_ref[rows, :], od_ref[rows, :]], axis=1)
        y = x_ref[rows, :] + mod_ref[2:3, :] * _rms(_dot(o, w_ref[...]), gpost_ref[...])
        y_ref[rows, :] = y
        h_ref[rows, :] = (_rms(y, gpre_ref[...]) * (1.0 + mod_ref[4:5, :]) + mod_ref[3:4, :]).astype(BF)


def _out_projection(o_parts, w, x, mod, mod_row, l, gpost, gpre, tm):
    m = x.shape[0]
    o_spec = pl.BlockSpec((tm, GROUP), lambda i: (i, 0))
    row_spec = pl.BlockSpec((tm, D_MODEL), lambda i: (i, 0))
    return pl.pallas_call(
        _outproj_kernel,
        grid=(m // tm,),
        in_specs=[
            o_spec, o_spec, o_spec, o_spec,
            pl.BlockSpec((D_MODEL, D_MODEL), lambda i: (0, 0)),
            row_spec,
            pl.BlockSpec((None, None, MOD_CHUNKS, D_MODEL), lambda i: (l, mod_row(i), 0, 0)),
            pl.BlockSpec((1, D_MODEL), lambda i: (0, 0)),
            pl.BlockSpec((1, D_MODEL), lambda i: (0, 0)),
        ],
        out_specs=[row_spec, row_spec],
        out_shape=[jax.ShapeDtypeStruct((m, D_MODEL), F32), jax.ShapeDtypeStruct((m, D_MODEL), BF)],
        compiler_params=_params(("arbitrary",)),
        name="out_projection",
    )(*o_parts, w, x, mod, gpost, gpre)


def _mlp_kernel(h_ref, x_ref, mod_ref, gpost_ref, wu_ref, wd_ref, y_ref):
    k = pl.program_id(1)
    last = pl.num_programs(1) - 1

    def ff(rows):
        u = jnp.square(jnp.maximum(_dot(h_ref[rows, :], wu_ref[...]), 0.0))
        return _dot(u.astype(BF), wd_ref[...])

    @pl.when(k == 0)
    def _():
        y_ref[...] = ff(slice(None))

    @pl.when((k > 0) & (k < last))
    def _():
        y_ref[...] += ff(slice(None))

    @pl.when(k == last)
    def _():
        for rows in _sub_tiles(x_ref.shape[0]):
            acc = y_ref[rows, :] + ff(rows)
            y_ref[rows, :] = x_ref[rows, :] + mod_ref[5:6, :] * _rms(acc, gpost_ref[...])


def _mlp(h, x, mod, mod_row, l, gpost, wu, wd, tm, tf):
    m = x.shape[0]
    assert D_FF // tf >= 2
    row_spec = pl.BlockSpec((tm, D_MODEL), lambda i, k: (i, 0))
    return pl.pallas_call(
        _mlp_kernel,
        grid=(m // tm, D_FF // tf),
        in_specs=[
            row_spec, row_spec,
            pl.BlockSpec((None, None, MOD_CHUNKS, D_MODEL), lambda i, k: (l, mod_row(i), 0, 0)),
            pl.BlockSpec((1, D_MODEL), lambda i, k: (0, 0)),
            pl.BlockSpec((D_MODEL, tf), lambda i, k: (0, k)),
            pl.BlockSpec((tf, D_MODEL), lambda i, k: (k, 0)),
        ],
        out_specs=row_spec,
        out_shape=jax.ShapeDtypeStruct((m, D_MODEL), F32),
        compiler_params=_params(("arbitrary", "arbitrary")),
        name="mlp",
    )(h, x, mod, gpost, wu, wd)


N_STATES = 7
STATE_HEADS = (2, 2, None, 2, 2, 4, 4)
RPS = 2


def _ctx_mixer_body(lam_init, z_ref, sink_ref, gkv_ref, wuk_ref, wuv_ref, gq_ref, gk_ref, lam_ref, gout_ref,
                    o_refs, st_refs):
    oa_ref, ob_ref, oc_ref, od_ref = o_refs
    ka_ref, va_ref, mla_ref, kc_ref, vc_ref, kd_ref, vd_ref = st_refs
    hd = HEAD_DIM

    c = HEAD_DIM ** -0.5 * LOG2E
    for hk in range(2):
        k = z_ref[:, 512 + hk * hd:512 + (hk + 1) * hd]
        v = z_ref[:, 768 + hk * hd:768 + (hk + 1) * hd]
        ka_ref[hk] = k
        va_ref[hk] = v
        kb, v1 = k.astype(BF), _with_ones(v.astype(BF))
        for g in range(2):
            h = hk * 2 + g
            q = (z_ref[:, h * hd:(h + 1) * hd] * c).astype(BF)
            oa_ref[:, h * hd:(h + 1) * hd] = _attend(q, kb, v1, sink_ref[h] * LOG2E).astype(BF)

    c = (HEAD_DIM + B_ROPE) ** -0.5 * LOG2E
    c_lat = _rms(z_ref[:, 1792:2048], gkv_ref[...])
    kr = z_ref[:, Z_MAIN:Z_MAIN + B_ROPE]
    mla_ref[:, 0:B_RANK] = c_lat
    mla_ref[:, B_RANK:B_RANK + B_ROPE] = kr
    cb, krb = c_lat.astype(BF), kr.astype(BF)
    for h in range(4):
        kn = _dot(cb, wuk_ref[:, h * hd:(h + 1) * hd]).astype(BF)
        v1 = _with_ones(_dot(cb, wuv_ref[:, h * hd:(h + 1) * hd]).astype(BF))
        qn = (z_ref[:, 1024 + h * hd:1024 + (h + 1) * hd] * c).astype(BF)
        qr = (z_ref[:, 1536 + h * B_ROPE:1536 + (h + 1) * B_ROPE] * c).astype(BF)
        s = _dot_nt(qn, kn) + _dot_nt(qr, krb)
        m = jnp.max(s, axis=-1, keepdims=True)
        o = _dot(jnp.exp2(s - m).astype(BF), v1)
        ob_ref[:, h * hd:(h + 1) * hd] = (o[:, :hd] / o[:, hd:]).astype(BF)

    c = HEAD_DIM ** -0.5 * LOG2E
    for hk in range(2):
        k = _rms(z_ref[:, 2560 + hk * hd:2560 + (hk + 1) * hd], gk_ref[...])
        v = z_ref[:, 2816 + hk * hd:2816 + (hk + 1) * hd]
        kc_ref[hk] = k
        vc_ref[hk] = v
        kb, v1 = k.astype(BF), _with_ones(v.astype(BF))
        for g in range(2):
            h = hk * 2 + g
            q = (_rms(z_ref[:, 2048 + h * hd:2048 + (h + 1) * hd], gq_ref[...]) * c).astype(BF)
            oc_ref[:, h * hd:(h + 1) * hd] = _attend(q, kb, v1).astype(BF)

    c = D_HALF ** -0.5 * LOG2E
    lam = _lam(lam_ref, lam_init)
    lane = lax.broadcasted_iota(jnp.int32, (SEQ, hd), 1)
    for h in range(4):
        k = z_ref[:, 3584 + h * hd:3584 + (h + 1) * hd]
        v = z_ref[:, 4096 + h * hd:4096 + (h + 1) * hd]
        kd_ref[h] = k
        vd_ref[h] = v
        kb, v1 = k.astype(BF), _with_ones(v.astype(BF))
        q = z_ref[:, 3072 + h * hd:3072 + (h + 1) * hd] * c
        o1 = _attend(jnp.where(lane < D_HALF, q, 0.0).astype(BF), kb, v1)
        o2 = _attend(jnp.where(lane >= D_HALF, q, 0.0).astype(BF), kb, v1)
        od_ref[:, h * hd:(h + 1) * hd] = (_rms(o1 - lam * o2, gout_ref[...]) * (1.0 - lam_init)).astype(BF)


def _ctx_layer_kernel(lam_init, n_alias, x_ref, mod_ref, g_ref, w_ref, sink_ref, gkv_ref, wuk_ref, wuv_ref, gq_ref, gk_ref,
                      lam_ref, gout_ref, *refs):
    refs = refs[n_alias:]
    o_refs, st_refs, z_s = refs[:4], refs[4:4 + N_STATES], refs[4 + N_STATES]
    for r in range(RPS):
        rows = slice(r * SEQ, (r + 1) * SEQ)
        h = _rms(x_ref[rows, :], g_ref[...]) * (1.0 + mod_ref[1:2, :]) + mod_ref[0:1, :]
        z_s[rows, :] = _dot(h.astype(BF), w_ref[...])
    for r in range(RPS):
        rows = slice(r * SEQ, (r + 1) * SEQ)
        _ctx_mixer_body(lam_init, z_s.at[rows, :], sink_ref, gkv_ref, wuk_ref, wuv_ref, gq_ref, gk_ref, lam_ref, gout_ref,
                        [o.at[rows, :] for o in o_refs], [s.at[r] for s in st_refs])


def _const(shape):
    return pl.BlockSpec(shape, lambda *_: (0,) * len(shape))


def _ctx_layer(x, mod, l, g, w, p, lam_init, prev_states):
    m = x.shape[0]
    o_spec = pl.BlockSpec((RPS * SEQ, GROUP), lambda b: (b, 0))
    o_shape = jax.ShapeDtypeStruct((m, GROUP), BF)

    def st_dims(h):
        return (SEQ, B_RANK + B_ROPE) if h is None else (h, SEQ, HEAD_DIM)

    def st_spec(h):
        dims = st_dims(h)
        return pl.BlockSpec((RPS, None) + dims, lambda b: (b, l) + (0,) * len(dims))

    in_specs = [
        pl.BlockSpec((RPS * SEQ, D_MODEL), lambda b: (b, 0)),
        pl.BlockSpec((None, None, MOD_CHUNKS, D_MODEL), lambda b: (l, 0, 0, 0)),
        _const((1, D_MODEL)),
        pl.BlockSpec((None, D_MODEL, Z_WIDTH), lambda b: (l, 0, 0), pipeline_mode=pl.Buffered(1)),
        pl.BlockSpec(memory_space=pltpu.SMEM),
        _const((1, B_RANK)), _const((B_RANK, GROUP)), _const((B_RANK, GROUP)),
        _const((1, HEAD_DIM)), _const((1, HEAD_DIM)), _const((4, D_HALF)), _const((1, HEAD_DIM)),
    ]
    n_alias = len(prev_states)
    aliases = {len(in_specs) + k: 4 + k for k in range(n_alias)}
    return pl.pallas_call(
        functools.partial(_ctx_layer_kernel, lam_init, n_alias),
        grid=(BATCH // RPS,),
        in_specs=in_specs + [pl.BlockSpec(memory_space=pl.ANY)] * n_alias,
        out_specs=[o_spec] * 4 + [st_spec(h) for h in STATE_HEADS],
        out_shape=[o_shape] * 4 + [jax.ShapeDtypeStruct((BATCH, DEPTH) + st_dims(h), F32) for h in STATE_HEADS],
        scratch_shapes=[pltpu.VMEM((RPS * SEQ, Z_WIDTH), F32)],
        input_output_aliases=aliases,
        compiler_params=_params(("arbitrary",)),
        name="ctx_layer",
    )(x, mod, g, w, p['a_sink'], p['b_g_kv'], p['w_uk'], p['w_uv'], p['c_gq'], p['c_gk'], p['d_lam'], p['d_g_out'],
      *prev_states)


QBLK = 256
N_QBLK = DEC_SEQ // QBLK
N_WBLK = DEC_SEQ // WINDOW


def _lat_a_kernel(z_ref, ck_ref, cv_ref, cos_ref, sin_ref, sink_ref, o_ref, q_s, k_s, v_s):
    hd = HEAD_DIM
    cos, sin = cos_ref[...], sin_ref[...]
    c = HEAD_DIM ** -0.5 * LOG2E
    for hk in range(2):
        k_s[hk, 0:WINDOW, :] = jnp.zeros((WINDOW, hd), BF)
        k_s[hk, WINDOW + DEC_SEQ:, :] = jnp.zeros((WINDOW, hd), BF)
        v_s[hk, 0:WINDOW, :] = jnp.zeros((WINDOW, 2 * hd), BF)
        v_s[hk, WINDOW + DEC_SEQ:, :] = jnp.zeros((WINDOW, 2 * hd), BF)
        k = _rope(z_ref[:, 512 + hk * hd:512 + (hk + 1) * hd], cos, sin, 32)
        k_s[hk, WINDOW:WINDOW + DEC_SEQ, :] = k.astype(BF)
        v_s[hk, WINDOW:WINDOW + DEC_SEQ, :] = _with_ones(z_ref[:, 768 + hk * hd:768 + (hk + 1) * hd].astype(BF))
        for g in range(2):
            h = hk * 2 + g
            q = (_rope(z_ref[:, h * hd:(h + 1) * hd], cos, sin, 32) * c).astype(BF)
            for i in range(N_WBLK):
                q_s[hk, i, g * WINDOW:(g + 1) * WINDOW, :] = q[i * WINDOW:(i + 1) * WINDOW, :]

    r_io = lax.broadcasted_iota(jnp.int32, (2 * WINDOW, 3 * WINDOW), 0) & (WINDOW - 1)
    c_io = lax.broadcasted_iota(jnp.int32, (2 * WINDOW, 3 * WINDOW), 1)
    second = lax.broadcasted_iota(jnp.int32, (2 * WINDOW, 1), 0) >= WINDOW
    for hk in range(2):
        kc = ck_ref[hk].astype(BF)
        vc1 = _with_ones(cv_ref[hk].astype(BF))
        sink2 = jnp.where(second, sink_ref[2 * hk + 1] * LOG2E, sink_ref[2 * hk] * LOG2E)
        for i in range(N_WBLK):
            r0 = i * WINDOW
            q = q_s[hk, i]
            kl = k_s[hk, r0:r0 + 3 * WINDOW, :]
            vl1 = v_s[hk, r0:r0 + 3 * WINDOW, :]
            kpos = (i - 1) * WINDOW + c_io
            qpos = i * WINDOW + r_io
            ok = (jnp.abs(qpos - kpos) <= WINDOW) & (kpos >= 0) & (kpos < DEC_SEQ)
            sc = _dot_nt(q, kc)
            sl = _dot_nt(q, kl)
            m = jnp.maximum(jnp.maximum(jnp.max(sc, axis=-1, keepdims=True),
                                        jnp.max(jnp.where(ok, sl, NEG), axis=-1, keepdims=True)), sink2)
            pw = jnp.where(ok, jnp.exp2(sl - m), 0.0)
            o = _dot(jnp.exp2(sc - m).astype(BF), vc1) + _dot(pw.astype(BF), vl1)
            o = (o[:, :hd] / (o[:, hd:] + jnp.exp2(sink2 - m))).astype(BF)
            o_ref[r0:r0 + WINDOW, 2 * hk * hd:(2 * hk + 1) * hd] = o[0:WINDOW, :]
            o_ref[r0:r0 + WINDOW, (2 * hk + 1) * hd:(2 * hk + 2) * hd] = o[WINDOW:, :]


def _lat_b_kernel(z_ref, zr_ref, mla_ref, gkv_ref, wuk_ref, wuv_ref, cos_ref, sin_ref, o_ref, c_s, kr_s, k_s, v_s, q_s):
    hd = HEAD_DIM
    cos, sin = cos_ref[...], sin_ref[...]
    c = (HEAD_DIM + B_ROPE) ** -0.5 * LOG2E
    c_s[0:PAST_LEN, :] = mla_ref[:, 0:B_RANK].astype(BF)
    c_s[PAST_LEN:, :] = _rms(z_ref[:, 768:1024], gkv_ref[...]).astype(BF)
    kr_s[0:PAST_LEN, B_ROPE:] = jnp.zeros((PAST_LEN, hd - B_ROPE), BF)
    kr_s[0:PAST_LEN, 0:B_ROPE] = mla_ref[:, B_RANK:B_RANK + B_ROPE].astype(BF)
    kr_s[PAST_LEN:, :] = _rope(zr_ref[...], cos, sin, 16).astype(BF)
    lane = lax.broadcasted_iota(jnp.int32, (DEC_SEQ, hd), 1)
    for h in range(4):
        k_s[h, :, 0:hd] = _dot(c_s[...], wuk_ref[:, h * hd:(h + 1) * hd]).astype(BF)
        k_s[h, :, hd:2 * hd] = kr_s[...]
        v_s[h] = _with_ones(_dot(c_s[...], wuv_ref[:, h * hd:(h + 1) * hd]).astype(BF))
        q_s[h, :, 0:hd] = (z_ref[:, h * hd:(h + 1) * hd] * c).astype(BF)
        qr = _rope(z_ref[:, 512 + (h // 2) * hd:512 + (h // 2 + 1) * hd], cos, sin, 16) * c
        if h % 2:
            qr = pltpu.roll(qr, B_ROPE, 1)
        q_s[h, :, hd:2 * hd] = jnp.where(lane < B_ROPE, qr, 0.0).astype(BF)
    for h in range(4):
        for i in range(N_QBLK):
            r0 = i * QBLK
            o = _attend(q_s[h, r0:r0 + QBLK, :], k_s[h], v_s[h])
            o_ref[r0:r0 + QBLK, h * hd:(h + 1) * hd] = o.astype(BF)


def _lat_c_kernel(z_ref, ck_ref, cv_ref, gq_ref, gk_ref, cos_ref, sin_ref, o_ref, k_s, v_s, q_s):
    hd = HEAD_DIM
    cos, sin = cos_ref[...], sin_ref[...]
    c = HEAD_DIM ** -0.5 * LOG2E
    for hk in range(2):
        k_s[hk, 0:PAST_LEN, :] = ck_ref[hk].astype(BF)
        k = _rms(z_ref[:, 512 + hk * hd:512 + (hk + 1) * hd], gk_ref[...])
        k_s[hk, PAST_LEN:, :] = _rope(k, cos, sin, 32).astype(BF)
        v_s[hk, 0:PAST_LEN, :] = _with_ones(cv_ref[hk].astype(BF))
        v_s[hk, PAST_LEN:, :] = _with_ones(z_ref[:, 768 + hk * hd:768 + (hk + 1) * hd].astype(BF))
    for h in range(4):
        q = _rms(z_ref[:, h * hd:(h + 1) * hd], gq_ref[...])
        q_s[h] = (_rope(q, cos, sin, 32) * c).astype(BF)
    for h in range(4):
        for i in range(N_QBLK):
            r0 = i * QBLK
            o = _attend(q_s[h, r0:r0 + QBLK, :], k_s[h // 2], v_s[h // 2])
            o_ref[r0:r0 + QBLK, h * hd:(h + 1) * hd] = o.astype(BF)


def _lat_d_kernel(lam_init, z_ref, dk_ref, dv_ref, lam_ref, gout_ref, cos_ref, sin_ref, o_ref, k_s, v_s, q1_s, q2_s):
    hd = HEAD_DIM
    cos, sin = cos_ref[...], sin_ref[...]
    c = D_HALF ** -0.5 * LOG2E
    lam = _lam(lam_ref, lam_init)
    lane = lax.broadcasted_iota(jnp.int32, (DEC_SEQ, hd), 1)
    for h in range(4):
        k_s[h, 0:PAST_LEN, :] = dk_ref[h].astype(BF)
        k_s[h, PAST_LEN:, :] = _rope(z_ref[:, 512 + h * hd:512 + (h + 1) * hd], cos, sin, 16).astype(BF)
        v_s[h, 0:PAST_LEN, :] = _with_ones(dv_ref[h].astype(BF))
        v_s[h, PAST_LEN:, :] = _with_ones(z_ref[:, 1024 + h * hd:1024 + (h + 1) * hd].astype(BF))
        q = _rope(z_ref[:, h * hd:(h + 1) * hd], cos, sin, 16) * c
        q1_s[h] = jnp.where(lane < D_HALF, q, 0.0).astype(BF)
        q2_s[h] = jnp.where(lane >= D_HALF, q, 0.0).astype(BF)
    for h in range(4):
        for i in range(N_QBLK):
            r0 = i * QBLK
            o1 = _attend(q1_s[h, r0:r0 + QBLK, :], k_s[h], v_s[h])
            o2 = _attend(q2_s[h, r0:r0 + QBLK, :], k_s[h], v_s[h])
            o = _rms(o1 - lam * o2, gout_ref[...]) * (1.0 - lam_init)
            o_ref[r0:r0 + QBLK, h * hd:(h + 1) * hd] = o.astype(BF)


def _lat_mixers(z, caches, l, p, tabs, lam_init, casts):
    m = z.shape[0]
    hd = HEAD_DIM
    cos128, sin128, cos64, sin64 = tabs
    o_spec = pl.BlockSpec((DEC_SEQ, GROUP), lambda b: (b, 0))
    o_shape = jax.ShapeDtypeStruct((m, GROUP), BF)
    tab = _const((DEC_SEQ, hd))

    def cache_spec(h):
        return pl.BlockSpec((None, None, h, PAST_LEN, hd), lambda b: (b, l, 0, 0, 0))

    def zspec(width, col0):
        return pl.BlockSpec((DEC_SEQ, width), lambda b: (b, col0 // width))

    def scratch(*shape):
        return pltpu.VMEM(shape, BF)

    def cast_specs(w, layer, axis):
        r, c = w.shape[1:]
        if axis == 'rows':
            blk = (r // DEC_BATCH, c)
            return pl.BlockSpec((None,) + blk, lambda b: (layer, b, 0)), pl.BlockSpec(blk, lambda b: (b, 0))
        blk = (r, c // DEC_BATCH)
        return pl.BlockSpec((None,) + blk, lambda b: (layer, 0, b)), pl.BlockSpec(blk, lambda b: (0, b))

    def call(kernel, name, in_specs, args, scratch_shapes):
        todo = casts.get(name, [])
        specs = [cast_specs(*t) for t in todo]
        if todo:
            kernel = _also_cast(kernel, len(in_specs), len(todo))
        outs = pl.pallas_call(
            kernel, grid=(DEC_BATCH,),
            in_specs=in_specs + [s[0] for s in specs],
            out_specs=[o_spec] + [s[1] for s in specs],
            out_shape=[o_shape] + [jax.ShapeDtypeStruct(t[0].shape[1:], BF) for t in todo],
            scratch_shapes=scratch_shapes, compiler_params=_params(("arbitrary",)), name=name,
        )(*args, *[t[0] for t in todo])
        return outs[0], list(outs[1:])

    o_a, c_a = call(_lat_a_kernel, "lat_mixer_a",
                    [zspec(1024, 0), cache_spec(2), cache_spec(2), tab, tab, pl.BlockSpec(memory_space=pltpu.SMEM)],
                    (z, caches['a_k'], caches['a_v'], cos128, sin128, p['a_sink']),
                    [scratch(2, N_WBLK, 2 * WINDOW, hd), scratch(2, DEC_SEQ + 2 * WINDOW, hd),
                     scratch(2, DEC_SEQ + 2 * WINDOW, 2 * hd)])

    o_b, c_b = call(_lat_b_kernel, "lat_mixer_b",
                    [zspec(1024, 1024), zspec(LANES, Z_MAIN),
                     pl.BlockSpec((None, None, PAST_LEN, B_RANK + B_ROPE), lambda b: (b, l, 0, 0)),
                     _const((1, B_RANK)), _const((B_RANK, GROUP)), _const((B_RANK, GROUP)), tab, tab],
                    (z, z, caches['mla'], p['b_g_kv'], p['w_uk'], p['w_uv'], cos64, sin64),
                    [scratch(KEYS, B_RANK), scratch(KEYS, hd), scratch(4, KEYS, 2 * hd), scratch(4, KEYS, 2 * hd),
                     scratch(4, DEC_SEQ, 2 * hd)])

    o_c, c_c = call(_lat_c_kernel, "lat_mixer_c",
                    [zspec(1024, 2048), cache_spec(2), cache_spec(2), _const((1, hd)), _const((1, hd)), tab, tab],
                    (z, caches['c_k'], caches['c_v'], p['c_gq'], p['c_gk'], cos128, sin128),
                    [scratch(2, KEYS, hd), scratch(2, KEYS, 2 * hd), scratch(4, DEC_SEQ, hd)])

    o_d, c_d = call(functools.partial(_lat_d_kernel, lam_init), "lat_mixer_d",
                    [zspec(1536, 3072), cache_spec(4), cache_spec(4), _const((4, D_HALF)), _const((1, hd)), tab, tab],
                    (z, caches['d_k'], caches['d_v'], p['d_lam'], p['d_g_out'], cos64, sin64),
                    [scratch(4, KEYS, hd), scratch(4, KEYS, 2 * hd), scratch(4, DEC_SEQ, hd), scratch(4, DEC_SEQ, hd)])

    return (o_a, o_b, o_c, o_d), {"lat_mixer_a": c_a, "lat_mixer_b": c_b, "lat_mixer_c": c_c, "lat_mixer_d": c_d}


def _rope_tables():
    t = jnp.arange(DEC_SEQ)
    row = (t // GRID_W).astype(F32)[:, None]
    col = (t % GRID_W).astype(F32)[:, None]

    def table(half):
        inv = ROPE_BASE ** (-jnp.arange(half, dtype=F32) / half)
        ar, ac = row * inv[None, :], col * inv[None, :]
        cos = jnp.concatenate([jnp.cos(ar), jnp.cos(ar), jnp.cos(ac), jnp.cos(ac)], axis=-1)
        sin = jnp.concatenate([-jnp.sin(ar), jnp.sin(ar), -jnp.sin(ac), jnp.sin(ac)], axis=-1)
        return cos, sin

    cos128, sin128 = table(32)
    cos64, sin64 = table(16)
    return cos128, sin128, jnp.tile(cos64, (1, 2)), jnp.tile(sin64, (1, 2))


def kernel(x_prompt, x_sample, cache_a_k, cache_a_v, cache_mla, cache_c_k, cache_c_v, cache_d_k, cache_d_v,
           c, c_ctx, w_ada, b_ada, g_attn_pre, g_attn_post, g_mlp_pre, g_mlp_post, w_in, a_sink,
           b_g_kv, b_w_uk, b_w_uv, c_gq, c_gk, d_lam, d_g_out, w_out, w_up, w_down):
    w_in_p = _regroup_w_in(jnp.pad(w_in.astype(BF), ((0, 0), (0, 0), (0, Z_WIDTH - W_IN_COLS))))
    w_uk_b = b_w_uk.reshape(DEPTH, B_RANK, GROUP).astype(BF)
    w_uv_b = b_w_uv.reshape(DEPTH, B_RANK, GROUP).astype(BF)
    assert DEPTH == 2
    first_layer_casts = {"lat_mixer_a": [(w_out, 1, 'rows'), (w_down, 0, 'rows')],
                         "lat_mixer_b": [(w_up, 0, 'cols'), (w_out, 0, 'rows')],
                         "lat_mixer_c": [(w_down, 1, 'rows')],
                         "lat_mixer_d": [(w_up, 1, 'cols')]}

    cvec = jnp.concatenate([c_ctx[None, :], c, jnp.zeros((MOD_ROWS - 1 - DEC_BATCH, D_MODEL), F32)], axis=0)
    mod = _modulation(cvec, w_ada, b_ada).reshape(DEPTH, MOD_ROWS, MOD_CHUNKS, D_MODEL)
    tabs = _rope_tables()

    tm = 512
    tm_mlp, tf_mlp = 1024, 512
    ctx_row = lambda tm: (lambda i: 0)
    lat_row = lambda tm: (lambda i: 1 + (i * tm) // DEC_SEQ)
    caches = {'a_k': cache_a_k, 'a_v': cache_a_v, 'mla': cache_mla, 'c_k': cache_c_k, 'c_v': cache_c_v,
              'd_k': cache_d_k, 'd_v': cache_d_v}

    xc = x_prompt.reshape(BATCH * SEQ, D_MODEL)
    xl = x_sample.reshape(DEC_BATCH * DEC_SEQ, D_MODEL)
    states = ()
    w_out_b, w_down_b, w_up_b = {}, {}, {}
    for l in range(DEPTH):
        lam_init = 0.8 - 0.6 * math.exp(-0.3 * l)
        p = {'a_sink': a_sink[l], 'b_g_kv': b_g_kv[l][None], 'w_uk': w_uk_b[l], 'w_uv': w_uv_b[l],
             'c_gq': c_gq[l][None], 'c_gk': c_gk[l][None], 'd_lam': d_lam[l], 'd_g_out': d_g_out[l][None]}
        g_pre, g_post = g_attn_pre[l][None], g_attn_post[l][None]
        g_mpre, g_mpost = g_mlp_pre[l][None], g_mlp_post[l][None]

        zl = _in_projection(xl, mod, lat_row(tm), l, g_pre, w_in_p, tm)
        o_parts, cast = _lat_mixers(zl, caches, l, p, tabs, lam_init, first_layer_casts if l == 0 else {})
        if l == 0:
            (w_out_b[1], w_down_b[0]), (w_up_b[0], w_out_b[0]) = cast["lat_mixer_a"], cast["lat_mixer_b"]
            (w_down_b[1],), (w_up_b[1],) = cast["lat_mixer_c"], cast["lat_mixer_d"]

        outs = _ctx_layer(xc, mod, l, g_pre, w_in_p, p, lam_init, states)
        states = tuple(outs[4:])
        xc, hc = _out_projection(outs[:4], w_out_b[l], xc, mod, ctx_row(tm), l, g_post, g_mpre, tm)
        xc = _mlp(hc, xc, mod, ctx_row(tm_mlp), l, g_mpost, w_up_b[l], w_down_b[l], tm_mlp, tf_mlp)

        xl, hl = _out_projection(o_parts, w_out_b[l], xl, mod, lat_row(tm), l, g_post, g_mpre, tm)
        xl = _mlp(hl, xl, mod, lat_row(tm_mlp), l, g_mpost, w_up_b[l], w_down_b[l], tm_mlp, tf_mlp)

    y_prompt = xc.reshape(BATCH, SEQ, D_MODEL)
    y_sample = xl.reshape(DEC_BATCH, DEC_SEQ, D_MODEL)
    return (y_prompt, y_sample) + states
```

```python
import functools
import math

import jax
import jax.numpy as jnp
from jax import lax
from jax.experimental import pallas as pl
from jax.experimental.pallas import tpu as pltpu

D_MODEL = 2048
BATCH = 32
SEQ = 256
DEPTH = 2
DEC_BATCH = 8
DEC_SEQ = 1024
PAST_LEN = 256
GRID_W = 64
HEAD_DIM = 128
WINDOW = 128
B_ROPE = 64
B_RANK = 256
D_HALF = 64
D_FF = 4 * D_MODEL
ROPE_BASE = 10000.0
EPS = 1e-6
MOD_CHUNKS = 6
MOD_ROWS = 16

GROUP = 512
Z_MAIN = 4608
Z_WIDTH = 4864
KEYS = PAST_LEN + DEC_SEQ

LANES = 128
VMEM_LIMIT = 56 * 1024 * 1024
CTX_VMEM_LIMIT = 62 * 1024 * 1024

BF = jnp.bfloat16
F32 = jnp.float32
NEG = -1e30
LOG2E = math.log2(math.e)


def _dot(a, b):
    return jnp.dot(a, b, preferred_element_type=F32)


def _dot_nt(a, b):
    return lax.dot_general(a, b, (((1,), (1,)), ((), ())), preferred_element_type=F32)


def _rms(x, g):
    return x * lax.rsqrt(jnp.mean(x * x, axis=-1, keepdims=True) + EPS) * g


def _params(sem, vmem_limit=VMEM_LIMIT):
    return pltpu.CompilerParams(dimension_semantics=sem, vmem_limit_bytes=vmem_limit)


def _with_ones(v):
    return jnp.concatenate([v, jnp.ones(v.shape, v.dtype)], axis=1)


def _attend(q, k, v1, sink2=None):
    s = _dot_nt(q, k)
    m = jnp.max(s, axis=-1, keepdims=True)
    if sink2 is not None:
        m = jnp.maximum(m, sink2)
    o = _dot(jnp.exp2(s - m).astype(BF), v1)
    l = o[:, HEAD_DIM:]
    if sink2 is not None:
        l = l + jnp.exp2(sink2 - m)
    return o[:, :HEAD_DIM] / l


def _rope(x, cos, sin_signed, half):
    lane = lax.broadcasted_iota(jnp.int32, (8, LANES), 1)
    from_up = pltpu.roll(lane, half, 1)[0:1, :] == (lane[0:1, :] ^ half)
    partner = jnp.where(from_up, pltpu.roll(x, half, 1), pltpu.roll(x, LANES - half, 1))
    return x * cos + partner * sin_signed


def _lam(lam_ref, lam_init):
    lp = lam_ref[...]
    a = jnp.sum(lp[0:1, :] * lp[1:2, :], axis=-1, keepdims=True)
    b = jnp.sum(lp[2:3, :] * lp[3:4, :], axis=-1, keepdims=True)
    return jnp.exp(a) - jnp.exp(b) + lam_init


def _mod_kernel(c_ref, w_ref, b_ref, o_ref):
    c = c_ref[...]
    s = c * jax.nn.sigmoid(c)
    o_ref[...] = _dot(s.astype(BF), w_ref[...].astype(BF)) + b_ref[...]


def _modulation(cvec, w_ada, b_ada):
    tn = 1024
    n = MOD_CHUNKS * D_MODEL
    return pl.pallas_call(
        _mod_kernel,
        grid=(DEPTH, n // tn),
        in_specs=[
            pl.BlockSpec((MOD_ROWS, D_MODEL), lambda l, j: (0, 0)),
            pl.BlockSpec((None, D_MODEL, tn), lambda l, j: (l, 0, j)),
            pl.BlockSpec((None, 1, tn), lambda l, j: (l, 0, j)),
        ],
        out_specs=pl.BlockSpec((None, MOD_ROWS, tn), lambda l, j: (l, 0, j)),
        out_shape=jax.ShapeDtypeStruct((DEPTH, MOD_ROWS, n), F32),
        compiler_params=_params(("arbitrary", "arbitrary")),
        name="modulation",
    )(cvec, w_ada, b_ada.reshape(DEPTH, 1, n))


W_IN_COLS = 4672
KR_COL = 2048


def _regroup_kernel(w_ref, o_ref):
    o_ref[:, 0:KR_COL] = w_ref[:, 0:KR_COL]
    o_ref[:, KR_COL:Z_MAIN] = w_ref[:, KR_COL + B_ROPE:W_IN_COLS]
    o_ref[:, Z_MAIN:Z_MAIN + B_ROPE] = w_ref[:, KR_COL:KR_COL + B_ROPE]
    o_ref[:, Z_MAIN + B_ROPE:] = jnp.zeros((w_ref.shape[0], Z_WIDTH - Z_MAIN - B_ROPE), BF)


def _regroup_w_in(w_in_b):
    tr = 256
    return pl.pallas_call(
        _regroup_kernel,
        grid=(DEPTH, D_MODEL // tr),
        in_specs=[pl.BlockSpec((None, tr, Z_WIDTH), lambda l, i: (l, i, 0))],
        out_specs=pl.BlockSpec((None, tr, Z_WIDTH), lambda l, i: (l, i, 0)),
        out_shape=jax.ShapeDtypeStruct((DEPTH, D_MODEL, Z_WIDTH), BF),
        compiler_params=_params(("arbitrary", "arbitrary")),
        name="regroup_w_in",
    )(w_in_b)


def _also_cast(kernel, n_in, n_cast):
    def wrapped(*refs):
        for k in range(n_cast):
            refs[n_in + n_cast + 1 + k][...] = refs[n_in + k][...].astype(BF)
        kernel(*refs[:n_in], refs[n_in + n_cast], *refs[n_in + 2 * n_cast + 1:])
    return wrapped


SUB = 256


def _sub_tiles(rows, sub=SUB):
    return [slice(s, s + sub) for s in range(0, rows, sub)]


def _inproj_kernel(x_ref, mod_ref, g_ref, w_ref, z_ref):
    for rows in _sub_tiles(x_ref.shape[0]):
        h = _rms(x_ref[rows, :], g_ref[...]) * (1.0 + mod_ref[1:2, :]) + mod_ref[0:1, :]
        z_ref[rows, :] = _dot(h.astype(BF), w_ref[...])


def _in_projection(x, mod, mod_row, l, g, w, tm):
    m = x.shape[0]
    return pl.pallas_call(
        _inproj_kernel,
        grid=(m // tm,),
        in_specs=[
            pl.BlockSpec((tm, D_MODEL), lambda i: (i, 0)),
            pl.BlockSpec((None, None, MOD_CHUNKS, D_MODEL), lambda i: (l, mod_row(i), 0, 0)),
            pl.BlockSpec((1, D_MODEL), lambda i: (0, 0)),
            pl.BlockSpec((None, D_MODEL, Z_WIDTH), lambda i: (l, 0, 0), pipeline_mode=pl.Buffered(1)),
        ],
        out_specs=pl.BlockSpec((tm, Z_WIDTH), lambda i: (i, 0)),
        out_shape=jax.ShapeDtypeStruct((m, Z_WIDTH), F32),
        compiler_params=_params(("arbitrary",)),
        name="in_projection",
    )(x, mod, g, w)


def _outproj_kernel(oa_ref, ob_ref, oc_ref, od_ref, w_ref, x_ref, mod_ref, gpost_ref, gpre_ref, y_ref, h_ref):
    for rows in _sub_tiles(x_ref.shape[0], SUB // 2):
        o = jnp.concatenate([oa_ref[rows, :], ob_ref[rows, :], oc_ref[rows, :], od_ref[rows, :]], axis=1)
        y = x_ref[rows, :] + mod_ref[2:3, :] * _rms(_dot(o, w_ref[...]), gpost_ref[...])
        y_ref[rows, :] = y
        h_ref[rows, :] = (_rms(y, gpre_ref[...]) * (1.0 + mod_ref[4:5, :]) + mod_ref[3:4, :]).astype(BF)


def _out_projection(o_parts, w, x, mod, mod_row, l, gpost, gpre, tm):
    m = x.shape[0]
    o_spec = pl.BlockSpec((tm, GROUP), lambda i: (i, 0))
    row_spec = pl.BlockSpec((tm, D_MODEL), lambda i: (i, 0))
    return pl.pallas_call(
        _outproj_kernel,
        grid=(m // tm,),
        in_specs=[
            o_spec, o_spec, o_spec, o_spec,
            pl.BlockSpec((D_MODEL, D_MODEL), lambda i: (0, 0)),
            row_spec,
            pl.BlockSpec((None, None, MOD_CHUNKS, D_MODEL), lambda i: (l, mod_row(i), 0, 0)),
            pl.BlockSpec((1, D_MODEL), lambda i: (0, 0)),
            pl.BlockSpec((1, D_MODEL), lambda i: (0, 0)),
        ],
        out_specs=[row_spec, row_spec],
        out_shape=[jax.ShapeDtypeStruct((m, D_MODEL), F32), jax.ShapeDtypeStruct((m, D_MODEL), BF)],
        compiler_params=_params(("arbitrary",)),
        name="out_projection",
    )(*o_parts, w, x, mod, gpost, gpre)


def _mlp_kernel(h_ref, x_ref, mod_ref, gpost_ref, wu_ref, wd_ref, y_ref):
    k = pl.program_id(1)
    last = pl.num_programs(1) - 1

    def ff(rows):
        u = jnp.square(jnp.maximum(_dot(h_ref[rows, :], wu_ref[...]), 0.0))
        return _dot(u.astype(BF), wd_ref[...])

    @pl.when(k == 0)
    def _():
        y_ref[...] = ff(slice(None))

    @pl.when((k > 0) & (k < last))
    def _():
        y_ref[...] += ff(slice(None))

    @pl.when(k == last)
    def _():
        for rows in _sub_tiles(x_ref.shape[0]):
            acc = y_ref[rows, :] + ff(rows)
            y_ref[rows, :] = x_ref[rows, :] + mod_ref[5:6, :] * _rms(acc, gpost_ref[...])


def _mlp(h, x, mod, mod_row, l, gpost, wu, wd, tm, tf):
    m = x.shape[0]
    assert D_FF // tf >= 2
    row_spec = pl.BlockSpec((tm, D_MODEL), lambda i, k: (i, 0))
    return pl.pallas_call(
        _mlp_kernel,
        grid=(m // tm, D_FF // tf),
        in_specs=[
            row_spec, row_spec,
            pl.BlockSpec((None, None, MOD_CHUNKS, D_MODEL), lambda i, k: (l, mod_row(i), 0, 0)),
            pl.BlockSpec((1, D_MODEL), lambda i, k: (0, 0)),
            pl.BlockSpec((D_MODEL, tf), lambda i, k: (0, k)),
            pl.BlockSpec((tf, D_MODEL), lambda i, k: (k, 0)),
        ],
        out_specs=row_spec,
        out_shape=jax.ShapeDtypeStruct((m, D_MODEL), F32),
        compiler_params=_params(("arbitrary", "arbitrary")),
        name="mlp",
    )(h, x, mod, gpost, wu, wd)


N_STATES = 7
STATE_HEADS = (2, 2, None, 2, 2, 4, 4)
RPS = 2


def _ctx_mixer_body(lam_init, z_ref, sink_ref, gkv_ref, wuk_ref, wuv_ref, gq_ref, gk_ref, lam_ref, gout_ref,
                    o_refs, st_refs):
    oa_ref, ob_ref, oc_ref, od_ref = o_refs
    ka_ref, va_ref, mla_ref, kc_ref, vc_ref, kd_ref, vd_ref = st_refs
    hd = HEAD_DIM

    c = HEAD_DIM ** -0.5 * LOG2E
    for hk in range(2):
        k = z_ref[:, 512 + hk * hd:512 + (hk + 1) * hd]
        v = z_ref[:, 768 + hk * hd:768 + (hk + 1) * hd]
        ka_ref[hk] = k
        va_ref[hk] = v
        kb, v1 = k.astype(BF), _with_ones(v.astype(BF))
        for g in range(2):
            h = hk * 2 + g
            q = (z_ref[:, h * hd:(h + 1) * hd] * c).astype(BF)
            oa_ref[:, h * hd:(h + 1) * hd] = _attend(q, kb, v1, sink_ref[h] * LOG2E).astype(BF)

    c = (HEAD_DIM + B_ROPE) ** -0.5 * LOG2E
    c_lat = _rms(z_ref[:, 1792:2048], gkv_ref[...])
    kr = z_ref[:, Z_MAIN:Z_MAIN + B_ROPE]
    mla_ref[:, 0:B_RANK] = c_lat
    mla_ref[:, B_RANK:B_RANK + B_ROPE] = kr
    cb, krb = c_lat.astype(BF), kr.astype(BF)
    for h in range(4):
        kn = _dot(cb, wuk_ref[:, h * hd:(h + 1) * hd]).astype(BF)
        v1 = _with_ones(_dot(cb, wuv_ref[:, h * hd:(h + 1) * hd]).astype(BF))
        qn = (z_ref[:, 1024 + h * hd:1024 + (h + 1) * hd] * c).astype(BF)
        qr = (z_ref[:, 1536 + h * B_ROPE:1536 + (h + 1) * B_ROPE] * c).astype(BF)
        s = _dot_nt(qn, kn) + _dot_nt(qr, krb)
        m = jnp.max(s, axis=-1, keepdims=True)
        o = _dot(jnp.exp2(s - m).astype(BF), v1)
        ob_ref[:, h * hd:(h + 1) * hd] = (o[:, :hd] / o[:, hd:]).astype(BF)

    c = HEAD_DIM ** -0.5 * LOG2E
    for hk in range(2):
        k = _rms(z_ref[:, 2560 + hk * hd:2560 + (hk + 1) * hd], gk_ref[...])
        v = z_ref[:, 2816 + hk * hd:2816 + (hk + 1) * hd]
        kc_ref[hk] = k
        vc_ref[hk] = v
        kb, v1 = k.astype(BF), _with_ones(v.astype(BF))
        for g in range(2):
            h = hk * 2 + g
            q = (_rms(z_ref[:, 2048 + h * hd:2048 + (h + 1) * hd], gq_ref[...]) * c).astype(BF)
            oc_ref[:, h * hd:(h + 1) * hd] = _attend(q, kb, v1).astype(BF)

    c = D_HALF ** -0.5 * LOG2E
    lam = _lam(lam_ref, lam_init)
    lane = lax.broadcasted_iota(jnp.int32, (SEQ, hd), 1)
    for h in range(4):
        k = z_ref[:, 3584 + h * hd:3584 + (h + 1) * hd]
        v = z_ref[:, 4096 + h * hd:4096 + (h + 1) * hd]
        kd_ref[h] = k
        vd_ref[h] = v
        kb, v1 = k.astype(BF), _with_ones(v.astype(BF))
        q = z_ref[:, 3072 + h * hd:3072 + (h + 1) * hd] * c
        o1 = _attend(jnp.where(lane < D_HALF, q, 0.0).astype(BF), kb, v1)
        o2 = _attend(jnp.where(lane >= D_HALF, q, 0.0).astype(BF), kb, v1)
        od_ref[:, h * hd:(h + 1) * hd] = (_rms(o1 - lam * o2, gout_ref[...]) * (1.0 - lam_init)).astype(BF)


def _ctx_layer_kernel(lam_init, n_alias, x_ref, mod_ref, g_ref, w_ref, sink_ref, gkv_ref, wuk_ref, wuv_ref, gq_ref, gk_ref,
                      lam_ref, gout_ref, *refs):
    refs = refs[n_alias:]
    o_refs, st_refs, z_s = refs[:4], refs[4:4 + N_STATES], refs[4 + N_STATES]
    if not n_alias:
        for s in st_refs:
            s[:, 1:] = jnp.zeros((RPS, DEPTH - 1) + s.shape[2:], F32)
        st_refs = [s.at[:, 0] for s in st_refs]
    for r in range(RPS):
        rows = slice(r * SEQ, (r + 1) * SEQ)
        h = _rms(x_ref[rows, :], g_ref[...]) * (1.0 + mod_ref[1:2, :]) + mod_ref[0:1, :]
        z_s[rows, :] = _dot(h.astype(BF), w_ref[...])
    for r in range(RPS):
        rows = slice(r * SEQ, (r + 1) * SEQ)
        _ctx_mixer_body(lam_init, z_s.at[rows, :], sink_ref, gkv_ref, wuk_ref, wuv_ref, gq_ref, gk_ref, lam_ref, gout_ref,
                        [o.at[rows, :] for o in o_refs], [s.at[r] for s in st_refs])


def _const(shape):
    return pl.BlockSpec(shape, lambda *_: (0,) * len(shape))


def _ctx_layer(x, mod, l, g, w, p, lam_init, prev_states):
    m = x.shape[0]
    o_spec = pl.BlockSpec((RPS * SEQ, GROUP), lambda b: (b, 0))
    o_shape = jax.ShapeDtypeStruct((m, GROUP), BF)

    def st_dims(h):
        return (SEQ, B_RANK + B_ROPE) if h is None else (h, SEQ, HEAD_DIM)

    def st_spec(h):
        dims = st_dims(h)
        if l == 0:
            return pl.BlockSpec((RPS, DEPTH) + dims, lambda b: (b, 0) + (0,) * len(dims))
        return pl.BlockSpec((RPS, None) + dims, lambda b: (b, l) + (0,) * len(dims))

    in_specs = [
        pl.BlockSpec((RPS * SEQ, D_MODEL), lambda b: (b, 0)),
        pl.BlockSpec((None, None, MOD_CHUNKS, D_MODEL), lambda b: (l, 0, 0, 0)),
        _const((1, D_MODEL)),
        pl.BlockSpec((None, D_MODEL, Z_WIDTH), lambda b: (l, 0, 0), pipeline_mode=pl.Buffered(1)),
        pl.BlockSpec(memory_space=pltpu.SMEM),
        _const((1, B_RANK)), _const((B_RANK, GROUP)), _const((B_RANK, GROUP)),
        _const((1, HEAD_DIM)), _const((1, HEAD_DIM)), _const((4, D_HALF)), _const((1, HEAD_DIM)),
    ]
    n_alias = len(prev_states)
    aliases = {len(in_specs) + k: 4 + k for k in range(n_alias)}
    return pl.pallas_call(
        functools.partial(_ctx_layer_kernel, lam_init, n_alias),
        grid=(BATCH // RPS,),
        in_specs=in_specs + [pl.BlockSpec(memory_space=pl.ANY)] * n_alias,
        out_specs=[o_spec] * 4 + [st_spec(h) for h in STATE_HEADS],
        out_shape=[o_shape] * 4 + [jax.ShapeDtypeStruct((BATCH, DEPTH) + st_dims(h), F32) for h in STATE_HEADS],
        scratch_shapes=[pltpu.VMEM((RPS * SEQ, Z_WIDTH), F32)],
        input_output_aliases=aliases,
        compiler_params=_params(("arbitrary",), CTX_VMEM_LIMIT if l == 0 else VMEM_LIMIT),
        name="ctx_layer",
    )(x, mod, g, w, p['a_sink'], p['b_g_kv'], p['w_uk'], p['w_uv'], p['c_gq'], p['c_gk'], p['d_lam'], p['d_g_out'],
      *prev_states)


QBLK = 256
N_QBLK = DEC_SEQ // QBLK
N_WBLK = DEC_SEQ // WINDOW


def _lat_a_kernel(z_ref, ck_ref, cv_ref, cos_ref, sin_ref, sink_ref, o_ref, q_s, k_s, v_s):
    hd = HEAD_DIM
    cos, sin = cos_ref[...], sin_ref[...]
    c = HEAD_DIM ** -0.5 * LOG2E
    for hk in range(2):
        k_s[hk, 0:WINDOW, :] = jnp.zeros((WINDOW, hd), BF)
        k_s[hk, WINDOW + DEC_SEQ:, :] = jnp.zeros((WINDOW, hd), BF)
        v_s[hk, 0:WINDOW, :] = jnp.zeros((WINDOW, 2 * hd), BF)
        v_s[hk, WINDOW + DEC_SEQ:, :] = jnp.zeros((WINDOW, 2 * hd), BF)
        k = _rope(z_ref[:, 512 + hk * hd:512 + (hk + 1) * hd], cos, sin, 32)
        k_s[hk, WINDOW:WINDOW + DEC_SEQ, :] = k.astype(BF)
        v_s[hk, WINDOW:WINDOW + DEC_SEQ, :] = _with_ones(z_ref[:, 768 + hk * hd:768 + (hk + 1) * hd].astype(BF))
        for g in range(2):
            h = hk * 2 + g
            q = (_rope(z_ref[:, h * hd:(h + 1) * hd], cos, sin, 32) * c).astype(BF)
            for i in range(N_WBLK):
                q_s[hk, i, g * WINDOW:(g + 1) * WINDOW, :] = q[i * WINDOW:(i + 1) * WINDOW, :]

    r_io = lax.broadcasted_iota(jnp.int32, (2 * WINDOW, 3 * WINDOW), 0) & (WINDOW - 1)
    c_io = lax.broadcasted_iota(jnp.int32, (2 * WINDOW, 3 * WINDOW), 1)
    second = lax.broadcasted_iota(jnp.int32, (2 * WINDOW, 1), 0) >= WINDOW
    for hk in range(2):
        kc = ck_ref[hk].astype(BF)
        vc1 = _with_ones(cv_ref[hk].astype(BF))
        sink2 = jnp.where(second, sink_ref[2 * hk + 1] * LOG2E, sink_ref[2 * hk] * LOG2E)
        for i in range(N_WBLK):
            r0 = i * WINDOW
            q = q_s[hk, i]
            kl = k_s[hk, r0:r0 + 3 * WINDOW, :]
            vl1 = v_s[hk, r0:r0 + 3 * WINDOW, :]
            kpos = (i - 1) * WINDOW + c_io
            qpos = i * WINDOW + r_io
            ok = (jnp.abs(qpos - kpos) <= WINDOW) & (kpos >= 0) & (kpos < DEC_SEQ)
            sc = _dot_nt(q, kc)
            sl = _dot_nt(q, kl)
            m = jnp.maximum(jnp.maximum(jnp.max(sc, axis=-1, keepdims=True),
                                        jnp.max(jnp.where(ok, sl, NEG), axis=-1, keepdims=True)), sink2)
            pw = jnp.where(ok, jnp.exp2(sl - m), 0.0)
            o = _dot(jnp.exp2(sc - m).astype(BF), vc1) + _dot(pw.astype(BF), vl1)
            o = (o[:, :hd] / (o[:, hd:] + jnp.exp2(sink2 - m))).astype(BF)
            o_ref[r0:r0 + WINDOW, 2 * hk * hd:(2 * hk + 1) * hd] = o[0:WINDOW, :]
            o_ref[r0:r0 + WINDOW, (2 * hk + 1) * hd:(2 * hk + 2) * hd] = o[WINDOW:, :]


def _lat_b_kernel(z_ref, zr_ref, mla_ref, gkv_ref, wuk_ref, wuv_ref, cos_ref, sin_ref, o_ref, c_s, kr_s, k_s, v_s, q_s):
    hd = HEAD_DIM
    cos, sin = cos_ref[...], sin_ref[...]
    c = (HEAD_DIM + B_ROPE) ** -0.5 * LOG2E
    c_s[0:PAST_LEN, :] = mla_ref[:, 0:B_RANK].astype(BF)
    c_s[PAST_LEN:, :] = _rms(z_ref[:, 768:1024], gkv_ref[...]).astype(BF)
    kr_s[0:PAST_LEN, B_ROPE:] = jnp.zeros((PAST_LEN, hd - B_ROPE), BF)
    kr_s[0:PAST_LEN, 0:B_ROPE] = mla_ref[:, B_RANK:B_RANK + B_ROPE].astype(BF)
    kr_s[PAST_LEN:, :] = _rope(zr_ref[...], cos, sin, 16).astype(BF)
    lane = lax.broadcasted_iota(jnp.int32, (DEC_SEQ, hd), 1)
    for h in range(4):
        k_s[h, :, 0:hd] = _dot(c_s[...], wuk_ref[:, h * hd:(h + 1) * hd]).astype(BF)
        k_s[h, :, hd:2 * hd] = kr_s[...]
        v_s[h] = _with_ones(_dot(c_s[...], wuv_ref[:, h * hd:(h + 1) * hd]).astype(BF))
        q_s[h, :, 0:hd] = (z_ref[:, h * hd:(h + 1) * hd] * c).astype(BF)
        qr = _rope(z_ref[:, 512 + (h // 2) * hd:512 + (h // 2 + 1) * hd], cos, sin, 16) * c
        if h % 2:
            qr = pltpu.roll(qr, B_ROPE, 1)
        q_s[h, :, hd:2 * hd] = jnp.where(lane < B_ROPE, qr, 0.0).astype(BF)
    for h in range(4):
        for i in range(N_QBLK):
            r0 = i * QBLK
            o = _attend(q_s[h, r0:r0 + QBLK, :], k_s[h], v_s[h])
            o_ref[r0:r0 + QBLK, h * hd:(h + 1) * hd] = o.astype(BF)


def _lat_c_kernel(z_ref, ck_ref, cv_ref, gq_ref, gk_ref, cos_ref, sin_ref, o_ref, k_s, v_s, q_s):
    hd = HEAD_DIM
    cos, sin = cos_ref[...], sin_ref[...]
    c = HEAD_DIM ** -0.5 * LOG2E
    for hk in range(2):
        k_s[hk, 0:PAST_LEN, :] = ck_ref[hk].astype(BF)
        k = _rms(z_ref[:, 512 + hk * hd:512 + (hk + 1) * hd], gk_ref[...])
        k_s[hk, PAST_LEN:, :] = _rope(k, cos, sin, 32).astype(BF)
        v_s[hk, 0:PAST_LEN, :] = _with_ones(cv_ref[hk].astype(BF))
        v_s[hk, PAST_LEN:, :] = _with_ones(z_ref[:, 768 + hk * hd:768 + (hk + 1) * hd].astype(BF))
    for h in range(4):
        q = _rms(z_ref[:, h * hd:(h + 1) * hd], gq_ref[...])
        q_s[h] = (_rope(q, cos, sin, 32) * c).astype(BF)
    for h in range(4):
        for i in range(N_QBLK):
            r0 = i * QBLK
            o = _attend(q_s[h, r0:r0 + QBLK, :], k_s[h // 2], v_s[h // 2])
            o_ref[r0:r0 + QBLK, h * hd:(h + 1) * hd] = o.astype(BF)


def _lat_d_kernel(lam_init, z_ref, dk_ref, dv_ref, lam_ref, gout_ref, cos_ref, sin_ref, o_ref, k_s, v_s, q1_s, q2_s):
    hd = HEAD_DIM
    cos, sin = cos_ref[...], sin_ref[...]
    c = D_HALF ** -0.5 * LOG2E
    lam = _lam(lam_ref, lam_init)
    lane = lax.broadcasted_iota(jnp.int32, (DEC_SEQ, hd), 1)
    for h in range(4):
        k_s[h, 0:PAST_LEN, :] = dk_ref[h].astype(BF)
        k_s[h, PAST_LEN:, :] = _rope(z_ref[:, 512 + h * hd:512 + (h + 1) * hd], cos, sin, 16).astype(BF)
        v_s[h, 0:PAST_LEN, :] = _with_ones(dv_ref[h].astype(BF))
        v_s[h, PAST_LEN:, :] = _with_ones(z_ref[:, 1024 + h * hd:1024 + (h + 1) * hd].astype(BF))
        q = _rope(z_ref[:, h * hd:(h + 1) * hd], cos, sin, 16) * c
        q1_s[h] = jnp.where(lane < D_HALF, q, 0.0).astype(BF)
        q2_s[h] = jnp.where(lane >= D_HALF, q, 0.0).astype(BF)
    for h in range(4):
        for i in range(N_QBLK):
            r0 = i * QBLK
            o1 = _attend(q1_s[h, r0:r0 + QBLK, :], k_s[h], v_s[h])
            o2 = _attend(q2_s[h, r0:r0 + QBLK, :], k_s[h], v_s[h])
            o = _rms(o1 - lam * o2, gout_ref[...]) * (1.0 - lam_init)
            o_ref[r0:r0 + QBLK, h * hd:(h + 1) * hd] = o.astype(BF)


def _lat_mixers(z, caches, l, p, tabs, lam_init, casts):
    m = z.shape[0]
    hd = HEAD_DIM
    cos128, sin128, cos64, sin64 = tabs
    o_spec = pl.BlockSpec((DEC_SEQ, GROUP), lambda b: (b, 0))
    o_shape = jax.ShapeDtypeStruct((m, GROUP), BF)
    tab = _const((DEC_SEQ, hd))

    def cache_spec(h):
        return pl.BlockSpec((None, None, h, PAST_LEN, hd), lambda b: (b, l, 0, 0, 0))

    def zspec(width, col0):
        return pl.BlockSpec((DEC_SEQ, width), lambda b: (b, col0 // width))

    def scratch(*shape):
        return pltpu.VMEM(shape, BF)

    def cast_specs(w, layer, axis):
        r, c = w.shape[1:]
        if axis == 'rows':
            blk = (r // DEC_BATCH, c)
            return pl.BlockSpec((None,) + blk, lambda b: (layer, b, 0)), pl.BlockSpec(blk, lambda b: (b, 0))
        blk = (r, c // DEC_BATCH)
        return pl.BlockSpec((None,) + blk, lambda b: (layer, 0, b)), pl.BlockSpec(blk, lambda b: (0, b))

    def call(kernel, name, in_specs, args, scratch_shapes):
        todo = casts.get(name, [])
        specs = [cast_specs(*t) for t in todo]
        if todo:
            kernel = _also_cast(kernel, len(in_specs), len(todo))
        outs = pl.pallas_call(
            kernel, grid=(DEC_BATCH,),
            in_specs=in_specs + [s[0] for s in specs],
            out_specs=[o_spec] + [s[1] for s in specs],
            out_shape=[o_shape] + [jax.ShapeDtypeStruct(t[0].shape[1:], BF) for t in todo],
            scratch_shapes=scratch_shapes, compiler_params=_params(("arbitrary",)), name=name,
        )(*args, *[t[0] for t in todo])
        return outs[0], list(outs[1:])

    o_a, c_a = call(_lat_a_kernel, "lat_mixer_a",
                    [zspec(1024, 0), cache_spec(2), cache_spec(2), tab, tab, pl.BlockSpec(memory_space=pltpu.SMEM)],
                    (z, caches['a_k'], caches['a_v'], cos128, sin128, p['a_sink']),
                    [scratch(2, N_WBLK, 2 * WINDOW, hd), scratch(2, DEC_SEQ + 2 * WINDOW, hd),
                     scratch(2, DEC_SEQ + 2 * WINDOW, 2 * hd)])

    o_b, c_b = call(_lat_b_kernel, "lat_mixer_b",
                    [zspec(1024, 1024), zspec(LANES, Z_MAIN),
                     pl.BlockSpec((None, None, PAST_LEN, B_RANK + B_ROPE), lambda b: (b, l, 0, 0)),
                     _const((1, B_RANK)), _const((B_RANK, GROUP)), _const((B_RANK, GROUP)), tab, tab],
                    (z, z, caches['mla'], p['b_g_kv'], p['w_uk'], p['w_uv'], cos64, sin64),
                    [scratch(KEYS, B_RANK), scratch(KEYS, hd), scratch(4, KEYS, 2 * hd), scratch(4, KEYS, 2 * hd),
                     scratch(4, DEC_SEQ, 2 * hd)])

    o_c, c_c = call(_lat_c_kernel, "lat_mixer_c",
                    [zspec(1024, 2048), cache_spec(2), cache_spec(2), _const((1, hd)), _const((1, hd)), tab, tab],
                    (z, caches['c_k'], caches['c_v'], p['c_gq'], p['c_gk'], cos128, sin128),
                    [scratch(2, KEYS, hd), scratch(2, KEYS, 2 * hd), scratch(4, DEC_SEQ, hd)])

    o_d, c_d = call(functools.partial(_lat_d_kernel, lam_init), "lat_mixer_d",
                    [zspec(1536, 3072), cache_spec(4), cache_spec(4), _const((4, D_HALF)), _const((1, hd)), tab, tab],
                    (z, caches['d_k'], caches['d_v'], p['d_lam'], p['d_g_out'], cos64, sin64),
                    [scratch(4, KEYS, hd), scratch(4, KEYS, 2 * hd), scratch(4, DEC_SEQ, hd), scratch(4, DEC_SEQ, hd)])

    return (o_a, o_b, o_c, o_d), {"lat_mixer_a": c_a, "lat_mixer_b": c_b, "lat_mixer_c": c_c, "lat_mixer_d": c_d}


def _rope_tables():
    t = jnp.arange(DEC_SEQ)
    row = (t // GRID_W).astype(F32)[:, None]
    col = (t % GRID_W).astype(F32)[:, None]

    def table(half):
        inv = ROPE_BASE ** (-jnp.arange(half, dtype=F32) / half)
        ar, ac = row * inv[None, :], col * inv[None, :]
        cos = jnp.concatenate([jnp.cos(ar), jnp.cos(ar), jnp.cos(ac), jnp.cos(ac)], axis=-1)
        sin = jnp.concatenate([-jnp.sin(ar), jnp.sin(ar), -jnp.sin(ac), jnp.sin(ac)], axis=-1)
        return cos, sin

    cos128, sin128 = table(32)
    cos64, sin64 = table(16)
    return cos128, sin128, jnp.tile(cos64, (1, 2)), jnp.tile(sin64, (1, 2))


def kernel(x_prompt, x_sample, cache_a_k, cache_a_v, cache_mla, cache_c_k, cache_c_v, cache_d_k, cache_d_v,
           c, c_ctx, w_ada, b_ada, g_attn_pre, g_attn_post, g_mlp_pre, g_mlp_post, w_in, a_sink,
           b_g_kv, b_w_uk, b_w_uv, c_gq, c_gk, d_lam, d_g_out, w_out, w_up, w_down):
    w_in_p = _regroup_w_in(jnp.concatenate(
        [w_in.astype(BF), jnp.zeros((DEPTH, D_MODEL, Z_WIDTH - W_IN_COLS), BF)], axis=-1))
    w_uk_b = b_w_uk.reshape(DEPTH, B_RANK, GROUP).astype(BF)
    w_uv_b = b_w_uv.reshape(DEPTH, B_RANK, GROUP).astype(BF)
    assert DEPTH == 2
    first_layer_casts = {"lat_mixer_a": [(w_out, 1, 'rows'), (w_down, 0, 'rows')],
                         "lat_mixer_b": [(w_up, 0, 'cols'), (w_out, 0, 'rows')],
                         "lat_mixer_c": [(w_down, 1, 'rows')],
                         "lat_mixer_d": [(w_up, 1, 'cols')]}

    cvec = jnp.concatenate([c_ctx[None, :], c, jnp.zeros((MOD_ROWS - 1 - DEC_BATCH, D_MODEL), F32)], axis=0)
    mod = _modulation(cvec, w_ada, b_ada).reshape(DEPTH, MOD_ROWS, MOD_CHUNKS, D_MODEL)
    tabs = _rope_tables()

    tm = 512
    tm_mlp, tf_mlp = 1024, 512
    ctx_row = lambda tm: (lambda i: 0)
    lat_row = lambda tm: (lambda i: 1 + (i * tm) // DEC_SEQ)
    caches = {'a_k': cache_a_k, 'a_v': cache_a_v, 'mla': cache_mla, 'c_k': cache_c_k, 'c_v': cache_c_v,
              'd_k': cache_d_k, 'd_v': cache_d_v}

    xc = x_prompt.reshape(BATCH * SEQ, D_MODEL)
    xl = x_sample.reshape(DEC_BATCH * DEC_SEQ, D_MODEL)
    states = ()
    w_out_b, w_down_b, w_up_b = {}, {}, {}
    for l in range(DEPTH):
        lam_init = 0.8 - 0.6 * math.exp(-0.3 * l)
        p = {'a_sink': a_sink[l], 'b_g_kv': b_g_kv[l][None], 'w_uk': w_uk_b[l], 'w_uv': w_uv_b[l],
             'c_gq': c_gq[l][None], 'c_gk': c_gk[l][None], 'd_lam': d_lam[l], 'd_g_out': d_g_out[l][None]}
        g_pre, g_post = g_attn_pre[l][None], g_attn_post[l][None]
        g_mpre, g_mpost = g_mlp_pre[l][None], g_mlp_post[l][None]

        zl = _in_projection(xl, mod, lat_row(tm), l, g_pre, w_in_p, tm)
        o_parts, cast = _lat_mixers(zl, caches, l, p, tabs, lam_init, first_layer_casts if l == 0 else {})
        if l == 0:
            (w_out_b[1], w_down_b[0]), (w_up_b[0], w_out_b[0]) = cast["lat_mixer_a"], cast["lat_mixer_b"]
            (w_down_b[1],), (w_up_b[1],) = cast["lat_mixer_c"], cast["lat_mixer_d"]

        outs = _ctx_layer(xc, mod, l, g_pre, w_in_p, p, lam_init, states)
        states = tuple(outs[4:])
        xc, hc = _out_projection(outs[:4], w_out_b[l], xc, mod, ctx_row(tm), l, g_post, g_mpre, tm)
        xc = _mlp(hc, xc, mod, ctx_row(tm_mlp), l, g_mpost, w_up_b[l], w_down_b[l], tm_mlp, tf_mlp)

        xl, hl = _out_projection(o_parts, w_out_b[l], xl, mod, lat_row(tm), l, g_post, g_mpre, tm)
        xl = _mlp(hl, xl, mod, lat_row(tm_mlp), l, g_mpost, w_up_b[l], w_down_b[l], tm_mlp, tf_mlp)

    y_prompt = xc.reshape(BATCH, SEQ, D_MODEL)
    y_sample = xl.reshape(DEC_BATCH, DEC_SEQ, D_MODEL)
    return (y_prompt, y_sample) + states
```

```python
import functools
import math

import jax
import jax.numpy as jnp
from jax import lax
from jax.experimental import pallas as pl
from jax.experimental.pallas import tpu as pltpu

D_MODEL = 2048
BATCH = 32
SEQ = 256
DEPTH = 2
DEC_BATCH = 8
DEC_SEQ = 1024
PAST_LEN = 256
GRID_W = 64
HEAD_DIM = 128
WINDOW = 128
B_ROPE = 64
B_RANK = 256
D_HALF = 64
D_FF = 4 * D_MODEL
ROPE_BASE = 10000.0
EPS = 1e-6
MOD_CHUNKS = 6
MOD_ROWS = 16

GROUP = 512
Z_MAIN = 4608
Z_WIDTH = 4864
KEYS = PAST_LEN + DEC_SEQ

LANES = 128
VMEM_LIMIT = 56 * 1024 * 1024
CTX_VMEM_LIMIT = 62 * 1024 * 1024

BF = jnp.bfloat16
F32 = jnp.float32
NEG = -1e30
LOG2E = math.log2(math.e)


def _dot(a, b):
    return jnp.dot(a, b, preferred_element_type=F32)


def _dot_nt(a, b):
    return lax.dot_general(a, b, (((1,), (1,)), ((), ())), preferred_element_type=F32)


def _rms(x, g):
    return x * lax.rsqrt(jnp.mean(x * x, axis=-1, keepdims=True) + EPS) * g


def _params(sem, vmem_limit=VMEM_LIMIT):
    return pltpu.CompilerParams(dimension_semantics=sem, vmem_limit_bytes=vmem_limit)


def _with_ones(v):
    return jnp.concatenate([v, jnp.ones(v.shape, v.dtype)], axis=1)


def _attend(q, k, v1, sink2=None):
    s = _dot_nt(q, k)
    m = jnp.max(s, axis=-1, keepdims=True)
    if sink2 is not None:
        m = jnp.maximum(m, sink2)
    o = _dot(jnp.exp2(s - m).astype(BF), v1)
    l = o[:, HEAD_DIM:]
    if sink2 is not None:
        l = l + jnp.exp2(sink2 - m)
    return o[:, :HEAD_DIM] / l


def _rope(x, cos, sin_signed, half):
    lane = lax.broadcasted_iota(jnp.int32, (8, LANES), 1)
    from_up = pltpu.roll(lane, half, 1)[0:1, :] == (lane[0:1, :] ^ half)
    partner = jnp.where(from_up, pltpu.roll(x, half, 1), pltpu.roll(x, LANES - half, 1))
    return x * cos + partner * sin_signed


def _lam(lam_ref, lam_init):
    lp = lam_ref[...]
    a = jnp.sum(lp[0:1, :] * lp[1:2, :], axis=-1, keepdims=True)
    b = jnp.sum(lp[2:3, :] * lp[3:4, :], axis=-1, keepdims=True)
    return jnp.exp(a) - jnp.exp(b) + lam_init


def _mod_kernel(c_ref, w_ref, b_ref, o_ref):
    c = c_ref[...]
    s = c * jax.nn.sigmoid(c)
    o_ref[...] = _dot(s.astype(BF), w_ref[...].astype(BF)) + b_ref[...]


def _modulation(cvec, w_ada, b_ada):
    tn = 1024
    n = MOD_CHUNKS * D_MODEL
    return pl.pallas_call(
        _mod_kernel,
        grid=(DEPTH, n // tn),
        in_specs=[
            pl.BlockSpec((MOD_ROWS, D_MODEL), lambda l, j: (0, 0)),
            pl.BlockSpec((None, D_MODEL, tn), lambda l, j: (l, 0, j)),
            pl.BlockSpec((None, 1, tn), lambda l, j: (l, 0, j)),
        ],
        out_specs=pl.BlockSpec((None, MOD_ROWS, tn), lambda l, j: (l, 0, j)),
        out_shape=jax.ShapeDtypeStruct((DEPTH, MOD_ROWS, n), F32),
        compiler_params=_params(("arbitrary", "arbitrary")),
        name="modulation",
    )(cvec, w_ada, b_ada.reshape(DEPTH, 1, n))


W_IN_COLS = 4672
KR_COL = 2048


def _regroup_kernel(w_ref, o_ref):
    o_ref[:, 0:KR_COL] = w_ref[:, 0:KR_COL]
    o_ref[:, KR_COL:Z_MAIN] = w_ref[:, KR_COL + B_ROPE:W_IN_COLS]
    o_ref[:, Z_MAIN:Z_MAIN + B_ROPE] = w_ref[:, KR_COL:KR_COL + B_ROPE]
    o_ref[:, Z_MAIN + B_ROPE:] = jnp.zeros((w_ref.shape[0], Z_WIDTH - Z_MAIN - B_ROPE), BF)


def _regroup_w_in(w_in_b):
    tr = 256
    return pl.pallas_call(
        _regroup_kernel,
        grid=(DEPTH, D_MODEL // tr),
        in_specs=[pl.BlockSpec((None, tr, Z_WIDTH), lambda l, i: (l, i, 0))],
        out_specs=pl.BlockSpec((None, tr, Z_WIDTH), lambda l, i: (l, i, 0)),
        out_shape=jax.ShapeDtypeStruct((DEPTH, D_MODEL, Z_WIDTH), BF),
        compiler_params=_params(("arbitrary", "arbitrary")),
        name="regroup_w_in",
    )(w_in_b)


def _also_cast(kernel, n_in, n_cast):
    def wrapped(*refs):
        for k in range(n_cast):
            refs[n_in + n_cast + 1 + k][...] = refs[n_in + k][...].astype(BF)
        kernel(*refs[:n_in], refs[n_in + n_cast], *refs[n_in + 2 * n_cast + 1:])
    return wrapped


SUB = 256


def _sub_tiles(rows, sub=SUB):
    return [slice(s, s + sub) for s in range(0, rows, sub)]


def _inproj_kernel(x_ref, mod_ref, g_ref, w_ref, z_ref):
    for rows in _sub_tiles(x_ref.shape[0]):
        h = _rms(x_ref[rows, :], g_ref[...]) * (1.0 + mod_ref[1:2, :]) + mod_ref[0:1, :]
        z_ref[rows, :] = _dot(h.astype(BF), w_ref[...])


def _in_projection(x, mod, mod_row, l, g, w, tm):
    m = x.shape[0]
    return pl.pallas_call(
        _inproj_kernel,
        grid=(m // tm,),
        in_specs=[
            pl.BlockSpec((tm, D_MODEL), lambda i: (i, 0)),
            pl.BlockSpec((None, None, MOD_CHUNKS, D_MODEL), lambda i: (l, mod_row(i), 0, 0)),
            pl.BlockSpec((1, D_MODEL), lambda i: (0, 0)),
            pl.BlockSpec((None, D_MODEL, Z_WIDTH), lambda i: (l, 0, 0), pipeline_mode=pl.Buffered(1)),
        ],
        out_specs=pl.BlockSpec((tm, Z_WIDTH), lambda i: (i, 0)),
        out_shape=jax.ShapeDtypeStruct((m, Z_WIDTH), F32),
        compiler_params=_params(("arbitrary",)),
        name="in_projection",
    )(x, mod, g, w)


def _outproj_kernel(oa_ref, ob_ref, oc_ref, od_ref, w_ref, x_ref, mod_ref, gpost_ref, gpre_ref, y_ref, h_ref):
    for rows in _sub_tiles(x_ref.shape[0], SUB // 2):
        o = jnp.concatenate([oa_ref[rows, :], ob_ref[rows, :], oc_ref[rows, :], od_ref[rows, :]], axis=1)
        y = x_ref[rows, :] + mod_ref[2:3, :] * _rms(_dot(o, w_ref[...]), gpost_ref[...])
        y_ref[rows, :] = y
        h_ref[rows, :] = (_rms(y, gpre_ref[...]) * (1.0 + mod_ref[4:5, :]) + mod_ref[3:4, :]).astype(BF)


def _out_projection(o_parts, w, x, mod, mod_row, l, gpost, gpre, tm):
    m = x.shape[0]
    o_spec = pl.BlockSpec((tm, GROUP), lambda i: (i, 0))
    row_spec = pl.BlockSpec((tm, D_MODEL), lambda i: (i, 0))
    return pl.pallas_call(
        _outproj_kernel,
        grid=(m // tm,),
        in_specs=[
            o_spec, o_spec, o_spec, o_spec,
            pl.BlockSpec((D_MODEL, D_MODEL), lambda i: (0, 0)),
            row_spec,
            pl.BlockSpec((None, None, MOD_CHUNKS, D_MODEL), lambda i: (l, mod_row(i), 0, 0)),
            pl.BlockSpec((1, D_MODEL), lambda i: (0, 0)),
            pl.BlockSpec((1, D_MODEL), lambda i: (0, 0)),
        ],
        out_specs=[row_spec, row_spec],
        out_shape=[jax.ShapeDtypeStruct((m, D_MODEL), F32), jax.ShapeDtypeStruct((m, D_MODEL), BF)],
        compiler_params=_params(("arbitrary",)),
        name="out_projection",
    )(*o_parts, w, x, mod, gpost, gpre)


def _mlp_kernel(h_ref, x_ref, mod_ref, gpost_ref, wu_ref, wd_ref, y_ref):
    k = pl.program_id(1)
    last = pl.num_programs(1) - 1

    def ff(rows):
        u = jnp.square(jnp.maximum(_dot(h_ref[rows, :], wu_ref[...]), 0.0))
        return _dot(u.astype(BF), wd_ref[...])

    @pl.when(k == 0)
    def _():
        y_ref[...] = ff(slice(None))

    @pl.when((k > 0) & (k < last))
    def _():
        y_ref[...] += ff(slice(None))

    @pl.when(k == last)
    def _():
        for rows in _sub_tiles(x_ref.shape[0]):
            acc = y_ref[rows, :] + ff(rows)
            y_ref[rows, :] = x_ref[rows, :] + mod_ref[5:6, :] * _rms(acc, gpost_ref[...])


def _mlp(h, x, mod, mod_row, l, gpost, wu, wd, tm, tf):
    m = x.shape[0]
    assert D_FF // tf >= 2
    row_spec = pl.BlockSpec((tm, D_MODEL), lambda i, k: (i, 0))
    return pl.pallas_call(
        _mlp_kernel,
        grid=(m // tm, D_FF // tf),
        in_specs=[
            row_spec, row_spec,
            pl.BlockSpec((None, None, MOD_CHUNKS, D_MODEL), lambda i, k: (l, mod_row(i), 0, 0)),
            pl.BlockSpec((1, D_MODEL), lambda i, k: (0, 0)),
            pl.BlockSpec((D_MODEL, tf), lambda i, k: (0, k)),
            pl.BlockSpec((tf, D_MODEL), lambda i, k: (k, 0)),
        ],
        out_specs=row_spec,
        out_shape=jax.ShapeDtypeStruct((m, D_MODEL), F32),
        compiler_params=_params(("arbitrary", "arbitrary")),
        name="mlp",
    )(h, x, mod, gpost, wu, wd)


N_STATES = 7
STATE_HEADS = (2, 2, None, 2, 2, 4, 4)
RPS = 2


def _ctx_mixer_body(lam_init, z_ref, sink_ref, gkv_ref, wuk_ref, wuv_ref, gq_ref, gk_ref, lam_ref, gout_ref,
                    o_refs, st_refs):
    oa_ref, ob_ref, oc_ref, od_ref = o_refs
    ka_ref, va_ref, mla_ref, kc_ref, vc_ref, kd_ref, vd_ref = st_refs
    hd = HEAD_DIM

    c = HEAD_DIM ** -0.5 * LOG2E
    for hk in range(2):
        k = z_ref[:, 512 + hk * hd:512 + (hk + 1) * hd]
        v = z_ref[:, 768 + hk * hd:768 + (hk + 1) * hd]
        ka_ref[hk] = k
        va_ref[hk] = v
        kb, v1 = k.astype(BF), _with_ones(v.astype(BF))
        for g in range(2):
            h = hk * 2 + g
            q = (z_ref[:, h * hd:(h + 1) * hd] * c).astype(BF)
            oa_ref[:, h * hd:(h + 1) * hd] = _attend(q, kb, v1, sink_ref[h] * LOG2E).astype(BF)

    c = (HEAD_DIM + B_ROPE) ** -0.5 * LOG2E
    c_lat = _rms(z_ref[:, 1792:2048], gkv_ref[...])
    kr = z_ref[:, Z_MAIN:Z_MAIN + B_ROPE]
    mla_ref[:, 0:B_RANK] = c_lat
    mla_ref[:, B_RANK:B_RANK + B_ROPE] = kr
    cb, krb = c_lat.astype(BF), kr.astype(BF)
    for h in range(4):
        kn = _dot(cb, wuk_ref[:, h * hd:(h + 1) * hd]).astype(BF)
        v1 = _with_ones(_dot(cb, wuv_ref[:, h * hd:(h + 1) * hd]).astype(BF))
        qn = (z_ref[:, 1024 + h * hd:1024 + (h + 1) * hd] * c).astype(BF)
        qr = (z_ref[:, 1536 + h * B_ROPE:1536 + (h + 1) * B_ROPE] * c).astype(BF)
        s = _dot_nt(qn, kn) + _dot_nt(qr, krb)
        m = jnp.max(s, axis=-1, keepdims=True)
        o = _dot(jnp.exp2(s - m).astype(BF), v1)
        ob_ref[:, h * hd:(h + 1) * hd] = (o[:, :hd] / o[:, hd:]).astype(BF)

    c = HEAD_DIM ** -0.5 * LOG2E
    for hk in range(2):
        k = _rms(z_ref[:, 2560 + hk * hd:2560 + (hk + 1) * hd], gk_ref[...])
        v = z_ref[:, 2816 + hk * hd:2816 + (hk + 1) * hd]
        kc_ref[hk] = k
        vc_ref[hk] = v
        kb, v1 = k.astype(BF), _with_ones(v.astype(BF))
        for g in range(2):
            h = hk * 2 + g
            q = (_rms(z_ref[:, 2048 + h * hd:2048 + (h + 1) * hd], gq_ref[...]) * c).astype(BF)
            oc_ref[:, h * hd:(h + 1) * hd] = _attend(q, kb, v1).astype(BF)

    c = D_HALF ** -0.5 * LOG2E
    lam = _lam(lam_ref, lam_init)
    lane = lax.broadcasted_iota(jnp.int32, (SEQ, hd), 1)
    for h in range(4):
        k = z_ref[:, 3584 + h * hd:3584 + (h + 1) * hd]
        v = z_ref[:, 4096 + h * hd:4096 + (h + 1) * hd]
        kd_ref[h] = k
        vd_ref[h] = v
        kb, v1 = k.astype(BF), _with_ones(v.astype(BF))
        q = z_ref[:, 3072 + h * hd:3072 + (h + 1) * hd] * c
        o1 = _attend(jnp.where(lane < D_HALF, q, 0.0).astype(BF), kb, v1)
        o2 = _attend(jnp.where(lane >= D_HALF, q, 0.0).astype(BF), kb, v1)
        od_ref[:, h * hd:(h + 1) * hd] = (_rms(o1 - lam * o2, gout_ref[...]) * (1.0 - lam_init)).astype(BF)


def _ctx_layer_kernel(lam_init, n_alias, x_ref, mod_ref, g_ref, w_ref, sink_ref, gkv_ref, wuk_ref, wuv_ref, gq_ref, gk_ref,
                      lam_ref, gout_ref, *refs):
    refs = refs[n_alias:]
    o_refs, st_refs, z_s = refs[:4], refs[4:4 + N_STATES], refs[4 + N_STATES]
    if not n_alias:
        for s in st_refs:
            s[:, 1:] = jnp.zeros((RPS, DEPTH - 1) + s.shape[2:], F32)
        st_refs = [s.at[:, 0] for s in st_refs]
    for r in range(RPS):
        rows = slice(r * SEQ, (r + 1) * SEQ)
        h = _rms(x_ref[rows, :], g_ref[...]) * (1.0 + mod_ref[1:2, :]) + mod_ref[0:1, :]
        z_s[rows, :] = _dot(h.astype(BF), w_ref[...])
    for r in range(RPS):
        rows = slice(r * SEQ, (r + 1) * SEQ)
        _ctx_mixer_body(lam_init, z_s.at[rows, :], sink_ref, gkv_ref, wuk_ref, wuv_ref, gq_ref, gk_ref, lam_ref, gout_ref,
                        [o.at[rows, :] for o in o_refs], [s.at[r] for s in st_refs])


def _const(shape):
    return pl.BlockSpec(shape, lambda *_: (0,) * len(shape))


def _ctx_layer(x, mod, l, g, w, p, lam_init, prev_states):
    m = x.shape[0]
    o_spec = pl.BlockSpec((RPS * SEQ, GROUP), lambda b: (b, 0))
    o_shape = jax.ShapeDtypeStruct((m, GROUP), BF)

    def st_dims(h):
        return (SEQ, B_RANK + B_ROPE) if h is None else (h, SEQ, HEAD_DIM)

    def st_spec(h):
        dims = st_dims(h)
        if l == 0:
            return pl.BlockSpec((RPS, DEPTH) + dims, lambda b: (b, 0) + (0,) * len(dims))
        return pl.BlockSpec((RPS, None) + dims, lambda b: (b, l) + (0,) * len(dims))

    in_specs = [
        pl.BlockSpec((RPS * SEQ, D_MODEL), lambda b: (b, 0)),
        pl.BlockSpec((None, None, MOD_CHUNKS, D_MODEL), lambda b: (l, 0, 0, 0)),
        _const((1, D_MODEL)),
        pl.BlockSpec((None, D_MODEL, Z_WIDTH), lambda b: (l, 0, 0), pipeline_mode=pl.Buffered(1)),
        pl.BlockSpec(memory_space=pltpu.SMEM),
        _const((1, B_RANK)), _const((B_RANK, GROUP)), _const((B_RANK, GROUP)),
        _const((1, HEAD_DIM)), _const((1, HEAD_DIM)), _const((4, D_HALF)), _const((1, HEAD_DIM)),
    ]
    n_alias = len(prev_states)
    aliases = {len(in_specs) + k: 4 + k for k in range(n_alias)}
    return pl.pallas_call(
        functools.partial(_ctx_layer_kernel, lam_init, n_alias),
        grid=(BATCH // RPS,),
        in_specs=in_specs + [pl.BlockSpec(memory_space=pl.ANY)] * n_alias,
        out_specs=[o_spec] * 4 + [st_spec(h) for h in STATE_HEADS],
        out_shape=[o_shape] * 4 + [jax.ShapeDtypeStruct((BATCH, DEPTH) + st_dims(h), F32) for h in STATE_HEADS],
        scratch_shapes=[pltpu.VMEM((RPS * SEQ, Z_WIDTH), F32)],
        input_output_aliases=aliases,
        compiler_params=_params(("arbitrary",), CTX_VMEM_LIMIT if l == 0 else VMEM_LIMIT),
        name="ctx_layer",
    )(x, mod, g, w, p['a_sink'], p['b_g_kv'], p['w_uk'], p['w_uv'], p['c_gq'], p['c_gk'], p['d_lam'], p['d_g_out'],
      *prev_states)


QBLK = 256
N_QBLK = DEC_SEQ // QBLK
N_WBLK = DEC_SEQ // WINDOW


def _lat_a_kernel(z_ref, ck_ref, cv_ref, cos_ref, sin_ref, sink_ref, o_ref, q_s, k_s, v_s):
    hd = HEAD_DIM
    cos, sin = cos_ref[...], sin_ref[...]
    c = HEAD_DIM ** -0.5 * LOG2E
    r_io = lax.broadcasted_iota(jnp.int32, (2 * WINDOW, 3 * WINDOW), 0) & (WINDOW - 1)
    c_io = lax.broadcasted_iota(jnp.int32, (2 * WINDOW, 3 * WINDOW), 1)
    second = lax.broadcasted_iota(jnp.int32, (2 * WINDOW, 1), 0) >= WINDOW

    def prepare(hk):
        k_s[hk, 0:WINDOW, :] = jnp.zeros((WINDOW, hd), BF)
        k_s[hk, WINDOW + DEC_SEQ:, :] = jnp.zeros((WINDOW, hd), BF)
        v_s[hk, 0:WINDOW, :] = jnp.zeros((WINDOW, 2 * hd), BF)
        v_s[hk, WINDOW + DEC_SEQ:, :] = jnp.zeros((WINDOW, 2 * hd), BF)
        k = _rope(z_ref[:, 512 + hk * hd:512 + (hk + 1) * hd], cos, sin, 32)
        k_s[hk, WINDOW:WINDOW + DEC_SEQ, :] = k.astype(BF)
        v_s[hk, WINDOW:WINDOW + DEC_SEQ, :] = _with_ones(z_ref[:, 768 + hk * hd:768 + (hk + 1) * hd].astype(BF))
        for g in range(2):
            h = hk * 2 + g
            q = (_rope(z_ref[:, h * hd:(h + 1) * hd], cos, sin, 32) * c).astype(BF)
            for i in range(N_WBLK):
                q_s[hk, i, g * WINDOW:(g + 1) * WINDOW, :] = q[i * WINDOW:(i + 1) * WINDOW, :]

    def attend(hk, i):
        r0 = i * WINDOW
        kc = ck_ref[hk].astype(BF)
        vc1 = _with_ones(cv_ref[hk].astype(BF))
        sink2 = jnp.where(second, sink_ref[2 * hk + 1] * LOG2E, sink_ref[2 * hk] * LOG2E)
        q = q_s[hk, i]
        kl = k_s[hk, r0:r0 + 3 * WINDOW, :]
        vl1 = v_s[hk, r0:r0 + 3 * WINDOW, :]
        kpos = (i - 1) * WINDOW + c_io
        qpos = i * WINDOW + r_io
        ok = (jnp.abs(qpos - kpos) <= WINDOW) & (kpos >= 0) & (kpos < DEC_SEQ)
        sc = _dot_nt(q, kc)
        sl = _dot_nt(q, kl)
        m = jnp.maximum(jnp.maximum(jnp.max(sc, axis=-1, keepdims=True),
                                    jnp.max(jnp.where(ok, sl, NEG), axis=-1, keepdims=True)), sink2)
        pw = jnp.where(ok, jnp.exp2(sl - m), 0.0)
        o = _dot(jnp.exp2(sc - m).astype(BF), vc1) + _dot(pw.astype(BF), vl1)
        o = (o[:, :hd] / (o[:, hd:] + jnp.exp2(sink2 - m))).astype(BF)
        o_ref[r0:r0 + WINDOW, 2 * hk * hd:(2 * hk + 1) * hd] = o[0:WINDOW, :]
        o_ref[r0:r0 + WINDOW, (2 * hk + 1) * hd:(2 * hk + 2) * hd] = o[WINDOW:, :]

    prepare(0)
    attend(0, 0)
    prepare(1)
    for i in range(1, N_WBLK):
        attend(0, i)
    for i in range(N_WBLK):
        attend(1, i)


def _lat_b_kernel(z_ref, zr_ref, mla_ref, gkv_ref, wuk_ref, wuv_ref, cos_ref, sin_ref, o_ref, c_s, kr_s, k_s, v_s, q_s):
    hd = HEAD_DIM
    cos, sin = cos_ref[...], sin_ref[...]
    c = (HEAD_DIM + B_ROPE) ** -0.5 * LOG2E
    c_s[0:PAST_LEN, :] = mla_ref[:, 0:B_RANK].astype(BF)
    c_s[PAST_LEN:, :] = _rms(z_ref[:, 768:1024], gkv_ref[...]).astype(BF)
    kr_s[0:PAST_LEN, B_ROPE:] = jnp.zeros((PAST_LEN, hd - B_ROPE), BF)
    kr_s[0:PAST_LEN, 0:B_ROPE] = mla_ref[:, B_RANK:B_RANK + B_ROPE].astype(BF)
    kr_s[PAST_LEN:, :] = _rope(zr_ref[...], cos, sin, 16).astype(BF)
    lane = lax.broadcasted_iota(jnp.int32, (DEC_SEQ, hd), 1)

    def prepare(h):
        k_s[h, :, 0:hd] = _dot(c_s[...], wuk_ref[:, h * hd:(h + 1) * hd]).astype(BF)
        k_s[h, :, hd:2 * hd] = kr_s[...]
        v_s[h] = _with_ones(_dot(c_s[...], wuv_ref[:, h * hd:(h + 1) * hd]).astype(BF))
        q_s[h, :, 0:hd] = (z_ref[:, h * hd:(h + 1) * hd] * c).astype(BF)
        qr = _rope(z_ref[:, 512 + (h // 2) * hd:512 + (h // 2 + 1) * hd], cos, sin, 16) * c
        if h % 2:
            qr = pltpu.roll(qr, B_ROPE, 1)
        q_s[h, :, hd:2 * hd] = jnp.where(lane < B_ROPE, qr, 0.0).astype(BF)

    def attend(h):
        for i in range(N_QBLK):
            r0 = i * QBLK
            o = _attend(q_s[h, r0:r0 + QBLK, :], k_s[h], v_s[h])
            o_ref[r0:r0 + QBLK, h * hd:(h + 1) * hd] = o.astype(BF)

    _one_head_ahead(prepare, attend, 4)


def _one_head_ahead(prepare, attend, n):
    prepare(0)
    for h in range(n):
        if h + 1 < n:
            prepare(h + 1)
        attend(h)


def _lat_c_kernel(z_ref, ck_ref, cv_ref, gq_ref, gk_ref, cos_ref, sin_ref, o_ref, k_s, v_s, q_s):
    hd = HEAD_DIM
    cos, sin = cos_ref[...], sin_ref[...]
    c = HEAD_DIM ** -0.5 * LOG2E

    def prepare(h):
        if h % 2 == 0:
            hk = h // 2
            k_s[hk, 0:PAST_LEN, :] = ck_ref[hk].astype(BF)
            k = _rms(z_ref[:, 512 + hk * hd:512 + (hk + 1) * hd], gk_ref[...])
            k_s[hk, PAST_LEN:, :] = _rope(k, cos, sin, 32).astype(BF)
            v_s[hk, 0:PAST_LEN, :] = _with_ones(cv_ref[hk].astype(BF))
            v_s[hk, PAST_LEN:, :] = _with_ones(z_ref[:, 768 + hk * hd:768 + (hk + 1) * hd].astype(BF))
        q = _rms(z_ref[:, h * hd:(h + 1) * hd], gq_ref[...])
        q_s[h] = (_rope(q, cos, sin, 32) * c).astype(BF)

    def attend(h):
        for i in range(N_QBLK):
            r0 = i * QBLK
            o = _attend(q_s[h, r0:r0 + QBLK, :], k_s[h // 2], v_s[h // 2])
            o_ref[r0:r0 + QBLK, h * hd:(h + 1) * hd] = o.astype(BF)

    _one_head_ahead(prepare, attend, 4)


def _lat_d_kernel(lam_init, z_ref, dk_ref, dv_ref, lam_ref, gout_ref, cos_ref, sin_ref, o_ref, k_s, v_s, q1_s, q2_s):
    hd = HEAD_DIM
    cos, sin = cos_ref[...], sin_ref[...]
    c = D_HALF ** -0.5 * LOG2E
    lam = _lam(lam_ref, lam_init)
    lane = lax.broadcasted_iota(jnp.int32, (DEC_SEQ, hd), 1)

    def prepare(h):
        k_s[h, 0:PAST_LEN, :] = dk_ref[h].astype(BF)
        k_s[h, PAST_LEN:, :] = _rope(z_ref[:, 512 + h * hd:512 + (h + 1) * hd], cos, sin, 16).astype(BF)
        v_s[h, 0:PAST_LEN, :] = _with_ones(dv_ref[h].astype(BF))
        v_s[h, PAST_LEN:, :] = _with_ones(z_ref[:, 1024 + h * hd:1024 + (h + 1) * hd].astype(BF))
        q = _rope(z_ref[:, h * hd:(h + 1) * hd], cos, sin, 16) * c
        q1_s[h] = jnp.where(lane < D_HALF, q, 0.0).astype(BF)
        q2_s[h] = jnp.where(lane >= D_HALF, q, 0.0).astype(BF)

    def attend(h):
        for i in range(N_QBLK):
            r0 = i * QBLK
            o1 = _attend(q1_s[h, r0:r0 + QBLK, :], k_s[h], v_s[h])
            o2 = _attend(q2_s[h, r0:r0 + QBLK, :], k_s[h], v_s[h])
            o = _rms(o1 - lam * o2, gout_ref[...]) * (1.0 - lam_init)
            o_ref[r0:r0 + QBLK, h * hd:(h + 1) * hd] = o.astype(BF)

    _one_head_ahead(prepare, attend, 4)


def _lat_mixers(z, caches, l, p, tabs, lam_init, casts):
    m = z.shape[0]
    hd = HEAD_DIM
    cos128, sin128, cos64, sin64 = tabs
    o_spec = pl.BlockSpec((DEC_SEQ, GROUP), lambda b: (b, 0))
    o_shape = jax.ShapeDtypeStruct((m, GROUP), BF)
    tab = _const((DEC_SEQ, hd))

    def cache_spec(h):
        return pl.BlockSpec((None, None, h, PAST_LEN, hd), lambda b: (b, l, 0, 0, 0))

    def zspec(width, col0):
        return pl.BlockSpec((DEC_SEQ, width), lambda b: (b, col0 // width))

    def scratch(*shape):
        return pltpu.VMEM(shape, BF)

    def cast_specs(w, layer, axis):
        r, c = w.shape[1:]
        if axis == 'rows':
            blk = (r // DEC_BATCH, c)
            return pl.BlockSpec((None,) + blk, lambda b: (layer, b, 0)), pl.BlockSpec(blk, lambda b: (b, 0))
        blk = (r, c // DEC_BATCH)
        return pl.BlockSpec((None,) + blk, lambda b: (layer, 0, b)), pl.BlockSpec(blk, lambda b: (0, b))

    def call(kernel, name, in_specs, args, scratch_shapes):
        todo = casts.get(name, [])
        specs = [cast_specs(*t) for t in todo]
        if todo:
            kernel = _also_cast(kernel, len(in_specs), len(todo))
        outs = pl.pallas_call(
            kernel, grid=(DEC_BATCH,),
            in_specs=in_specs + [s[0] for s in specs],
            out_specs=[o_spec] + [s[1] for s in specs],
            out_shape=[o_shape] + [jax.ShapeDtypeStruct(t[0].shape[1:], BF) for t in todo],
            scratch_shapes=scratch_shapes, compiler_params=_params(("arbitrary",)), name=name,
        )(*args, *[t[0] for t in todo])
        return outs[0], list(outs[1:])

    o_a, c_a = call(_lat_a_kernel, "lat_mixer_a",
                    [zspec(1024, 0), cache_spec(2), cache_spec(2), tab, tab, pl.BlockSpec(memory_space=pltpu.SMEM)],
                    (z, caches['a_k'], caches['a_v'], cos128, sin128, p['a_sink']),
                    [scratch(2, N_WBLK, 2 * WINDOW, hd), scratch(2, DEC_SEQ + 2 * WINDOW, hd),
                     scratch(2, DEC_SEQ + 2 * WINDOW, 2 * hd)])

    o_b, c_b = call(_lat_b_kernel, "lat_mixer_b",
                    [zspec(1024, 1024), zspec(LANES, Z_MAIN),
                     pl.BlockSpec((None, None, PAST_LEN, B_RANK + B_ROPE), lambda b: (b, l, 0, 0)),
                     _const((1, B_RANK)), _const((B_RANK, GROUP)), _const((B_RANK, GROUP)), tab, tab],
                    (z, z, caches['mla'], p['b_g_kv'], p['w_uk'], p['w_uv'], cos64, sin64),
                    [scratch(KEYS, B_RANK), scratch(KEYS, hd), scratch(4, KEYS, 2 * hd), scratch(4, KEYS, 2 * hd),
                     scratch(4, DEC_SEQ, 2 * hd)])

    o_c, c_c = call(_lat_c_kernel, "lat_mixer_c",
                    [zspec(1024, 2048), cache_spec(2), cache_spec(2), _const((1, hd)), _const((1, hd)), tab, tab],
                    (z, caches['c_k'], caches['c_v'], p['c_gq'], p['c_gk'], cos128, sin128),
                    [scratch(2, KEYS, hd), scratch(2, KEYS, 2 * hd), scratch(4, DEC_SEQ, hd)])

    o_d, c_d = call(functools.partial(_lat_d_kernel, lam_init), "lat_mixer_d",
                    [zspec(1536, 3072), cache_spec(4), cache_spec(4), _const((4, D_HALF)), _const((1, hd)), tab, tab],
                    (z, caches['d_k'], caches['d_v'], p['d_lam'], p['d_g_out'], cos64, sin64),
                    [scratch(4, KEYS, hd), scratch(4, KEYS, 2 * hd), scratch(4, DEC_SEQ, hd), scratch(4, DEC_SEQ, hd)])

    return (o_a, o_b, o_c, o_d), {"lat_mixer_a": c_a, "lat_mixer_b": c_b, "lat_mixer_c": c_c, "lat_mixer_d": c_d}


def _rope_tables():
    t = jnp.arange(DEC_SEQ)
    row = (t // GRID_W).astype(F32)[:, None]
    col = (t % GRID_W).astype(F32)[:, None]

    def table(half):
        inv = ROPE_BASE ** (-jnp.arange(half, dtype=F32) / half)
        ar, ac = row * inv[None, :], col * inv[None, :]
        cos = jnp.concatenate([jnp.cos(ar), jnp.cos(ar), jnp.cos(ac), jnp.cos(ac)], axis=-1)
        sin = jnp.concatenate([-jnp.sin(ar), jnp.sin(ar), -jnp.sin(ac), jnp.sin(ac)], axis=-1)
        return cos, sin

    cos128, sin128 = table(32)
    cos64, sin64 = table(16)
    return cos128, sin128, jnp.tile(cos64, (1, 2)), jnp.tile(sin64, (1, 2))


def kernel(x_prompt, x_sample, cache_a_k, cache_a_v, cache_mla, cache_c_k, cache_c_v, cache_d_k, cache_d_v,
           c, c_ctx, w_ada, b_ada, g_attn_pre, g_attn_post, g_mlp_pre, g_mlp_post, w_in, a_sink,
           b_g_kv, b_w_uk, b_w_uv, c_gq, c_gk, d_lam, d_g_out, w_out, w_up, w_down):
    w_in_p = _regroup_w_in(jnp.concatenate(
        [w_in.astype(BF), jnp.zeros((DEPTH, D_MODEL, Z_WIDTH - W_IN_COLS), BF)], axis=-1))
    w_uk_b = b_w_uk.reshape(DEPTH, B_RANK, GROUP).astype(BF)
    w_uv_b = b_w_uv.reshape(DEPTH, B_RANK, GROUP).astype(BF)
    assert DEPTH == 2
    first_layer_casts = {"lat_mixer_a": [(w_out, 1, 'rows'), (w_down, 0, 'rows')],
                         "lat_mixer_b": [(w_up, 0, 'cols'), (w_out, 0, 'rows')],
                         "lat_mixer_c": [(w_down, 1, 'rows')],
                         "lat_mixer_d": [(w_up, 1, 'cols')]}

    cvec = jnp.concatenate([c_ctx[None, :], c, jnp.zeros((MOD_ROWS - 1 - DEC_BATCH, D_MODEL), F32)], axis=0)
    mod = _modulation(cvec, w_ada, b_ada).reshape(DEPTH, MOD_ROWS, MOD_CHUNKS, D_MODEL)
    tabs = _rope_tables()

    tm = 512
    tm_mlp, tf_mlp = 1024, 512
    ctx_row = lambda tm: (lambda i: 0)
    lat_row = lambda tm: (lambda i: 1 + (i * tm) // DEC_SEQ)
    caches = {'a_k': cache_a_k, 'a_v': cache_a_v, 'mla': cache_mla, 'c_k': cache_c_k, 'c_v': cache_c_v,
              'd_k': cache_d_k, 'd_v': cache_d_v}

    xc = x_prompt.reshape(BATCH * SEQ, D_MODEL)
    xl = x_sample.reshape(DEC_BATCH * DEC_SEQ, D_MODEL)
    states = ()
    w_out_b, w_down_b, w_up_b = {}, {}, {}
    for l in range(DEPTH):
        lam_init = 0.8 - 0.6 * math.exp(-0.3 * l)
        p = {'a_sink': a_sink[l], 'b_g_kv': b_g_kv[l][None], 'w_uk': w_uk_b[l], 'w_uv': w_uv_b[l],
             'c_gq': c_gq[l][None], 'c_gk': c_gk[l][None], 'd_lam': d_lam[l], 'd_g_out': d_g_out[l][None]}
        g_pre, g_post = g_attn_pre[l][None], g_attn_post[l][None]
        g_mpre, g_mpost = g_mlp_pre[l][None], g_mlp_post[l][None]

        zl = _in_projection(xl, mod, lat_row(tm), l, g_pre, w_in_p, tm)
        o_parts, cast = _lat_mixers(zl, caches, l, p, tabs, lam_init, first_layer_casts if l == 0 else {})
        if l == 0:
            (w_out_b[1], w_down_b[0]), (w_up_b[0], w_out_b[0]) = cast["lat_mixer_a"], cast["lat_mixer_b"]
            (w_down_b[1],), (w_up_b[1],) = cast["lat_mixer_c"], cast["lat_mixer_d"]

        outs = _ctx_layer(xc, mod, l, g_pre, w_in_p, p, lam_init, states)
        states = tuple(outs[4:])
        xc, hc = _out_projection(outs[:4], w_out_b[l], xc, mod, ctx_row(tm), l, g_post, g_mpre, tm)
        xc = _mlp(hc, xc, mod, ctx_row(tm_mlp), l, g_mpost, w_up_b[l], w_down_b[l], tm_mlp, tf_mlp)

        xl, hl = _out_projection(o_parts, w_out_b[l], xl, mod, lat_row(tm), l, g_post, g_mpre, tm)
        xl = _mlp(hl, xl, mod, lat_row(tm_mlp), l, g_mpost, w_up_b[l], w_down_b[l], tm_mlp, tf_mlp)

    y_prompt = xc.reshape(BATCH, SEQ, D_MODEL)
    y_sample = xl.reshape(DEC_BATCH, DEC_SEQ, D_MODEL)
    return (y_prompt, y_sample) + states
```

```python
import functools
import math

import jax
import jax.numpy as jnp
from jax import lax
from jax.experimental import pallas as pl
from jax.experimental.pallas import tpu as pltpu

D_MODEL = 2048
BATCH = 32
SEQ = 256
DEPTH = 2
DEC_BATCH = 8
DEC_SEQ = 1024
PAST_LEN = 256
GRID_W = 64
HEAD_DIM = 128
WINDOW = 128
B_ROPE = 64
B_RANK = 256
D_HALF = 64
D_FF = 4 * D_MODEL
ROPE_BASE = 10000.0
EPS = 1e-6
MOD_CHUNKS = 6
MOD_ROWS = 16

GROUP = 512
Z_MAIN = 4608
Z_WIDTH = 4864
KEYS = PAST_LEN + DEC_SEQ

LANES = 128
VMEM_LIMIT = 56 * 1024 * 1024
CTX_VMEM_LIMIT = 62 * 1024 * 1024

BF = jnp.bfloat16
F32 = jnp.float32
NEG = -1e30
LOG2E = math.log2(math.e)


def _dot(a, b):
    return jnp.dot(a, b, preferred_element_type=F32)


def _dot_nt(a, b):
    return lax.dot_general(a, b, (((1,), (1,)), ((), ())), preferred_element_type=F32)


def _rms(x, g):
    return x * lax.rsqrt(jnp.mean(x * x, axis=-1, keepdims=True) + EPS) * g


def _params(sem, vmem_limit=VMEM_LIMIT):
    return pltpu.CompilerParams(dimension_semantics=sem, vmem_limit_bytes=vmem_limit)


def _with_ones(v):
    return jnp.concatenate([v, jnp.ones(v.shape, v.dtype)], axis=1)


def _attend(q, k, v1, sink2=None):
    s = _dot_nt(q, k)
    m = jnp.max(s, axis=-1, keepdims=True)
    if sink2 is not None:
        m = jnp.maximum(m, sink2)
    o = _dot(jnp.exp2(s - m).astype(BF), v1)
    l = o[:, HEAD_DIM:]
    if sink2 is not None:
        l = l + jnp.exp2(sink2 - m)
    return o[:, :HEAD_DIM] / l


def _rope(x, cos, sin_signed, half):
    lane = lax.broadcasted_iota(jnp.int32, (8, LANES), 1)
    from_up = pltpu.roll(lane, half, 1)[0:1, :] == (lane[0:1, :] ^ half)
    partner = jnp.where(from_up, pltpu.roll(x, half, 1), pltpu.roll(x, LANES - half, 1))
    return x * cos + partner * sin_signed


def _lam(lam_ref, lam_init):
    lp = lam_ref[...]
    a = jnp.sum(lp[0:1, :] * lp[1:2, :], axis=-1, keepdims=True)
    b = jnp.sum(lp[2:3, :] * lp[3:4, :], axis=-1, keepdims=True)
    return jnp.exp(a) - jnp.exp(b) + lam_init


def _mod_kernel(c_ref, w_ref, b_ref, o_ref):
    c = c_ref[...]
    s = c * jax.nn.sigmoid(c)
    o_ref[...] = _dot(s.astype(BF), w_ref[...].astype(BF)) + b_ref[...]


def _modulation(cvec, w_ada, b_ada):
    tn = 1024
    n = MOD_CHUNKS * D_MODEL
    return pl.pallas_call(
        _mod_kernel,
        grid=(DEPTH, n // tn),
        in_specs=[
            pl.BlockSpec((MOD_ROWS, D_MODEL), lambda l, j: (0, 0)),
            pl.BlockSpec((None, D_MODEL, tn), lambda l, j: (l, 0, j)),
            pl.BlockSpec((None, 1, tn), lambda l, j: (l, 0, j)),
        ],
        out_specs=pl.BlockSpec((None, MOD_ROWS, tn), lambda l, j: (l, 0, j)),
        out_shape=jax.ShapeDtypeStruct((DEPTH, MOD_ROWS, n), F32),
        compiler_params=_params(("arbitrary", "arbitrary")),
        name="modulation",
    )(cvec, w_ada, b_ada.reshape(DEPTH, 1, n))


W_IN_COLS = 4672
KR_COL = 2048


def _regroup_kernel(w_ref, o_ref):
    o_ref[:, 0:KR_COL] = w_ref[:, 0:KR_COL]
    o_ref[:, KR_COL:Z_MAIN] = w_ref[:, KR_COL + B_ROPE:W_IN_COLS]
    o_ref[:, Z_MAIN:Z_MAIN + B_ROPE] = w_ref[:, KR_COL:KR_COL + B_ROPE]
    o_ref[:, Z_MAIN + B_ROPE:] = jnp.zeros((w_ref.shape[0], Z_WIDTH - Z_MAIN - B_ROPE), BF)


def _regroup_w_in(w_in_b):
    tr = 256
    return pl.pallas_call(
        _regroup_kernel,
        grid=(DEPTH, D_MODEL // tr),
        in_specs=[pl.BlockSpec((None, tr, Z_WIDTH), lambda l, i: (l, i, 0))],
        out_specs=pl.BlockSpec((None, tr, Z_WIDTH), lambda l, i: (l, i, 0)),
        out_shape=jax.ShapeDtypeStruct((DEPTH, D_MODEL, Z_WIDTH), BF),
        compiler_params=_params(("arbitrary", "arbitrary")),
        name="regroup_w_in",
    )(w_in_b)


def _also_cast(kernel, n_in, n_cast):
    def wrapped(*refs):
        for k in range(n_cast):
            refs[n_in + n_cast + 1 + k][...] = refs[n_in + k][...].astype(BF)
        kernel(*refs[:n_in], refs[n_in + n_cast], *refs[n_in + 2 * n_cast + 1:])
    return wrapped


SUB = 256


def _sub_tiles(rows, sub=SUB):
    return [slice(s, s + sub) for s in range(0, rows, sub)]


def _inproj_kernel(x_ref, mod_ref, g_ref, w_ref, z_ref):
    for rows in _sub_tiles(x_ref.shape[0]):
        h = _rms(x_ref[rows, :], g_ref[...]) * (1.0 + mod_ref[1:2, :]) + mod_ref[0:1, :]
        z_ref[rows, :] = _dot(h.astype(BF), w_ref[...])


def _in_projection(x, mod, mod_row, l, g, w, tm):
    m = x.shape[0]
    return pl.pallas_call(
        _inproj_kernel,
        grid=(m // tm,),
        in_specs=[
            pl.BlockSpec((tm, D_MODEL), lambda i: (i, 0)),
            pl.BlockSpec((None, None, MOD_CHUNKS, D_MODEL), lambda i: (l, mod_row(i), 0, 0)),
            pl.BlockSpec((1, D_MODEL), lambda i: (0, 0)),
            pl.BlockSpec((None, D_MODEL, Z_WIDTH), lambda i: (l, 0, 0), pipeline_mode=pl.Buffered(1)),
        ],
        out_specs=pl.BlockSpec((tm, Z_WIDTH), lambda i: (i, 0)),
        out_shape=jax.ShapeDtypeStruct((m, Z_WIDTH), F32),
        compiler_params=_params(("arbitrary",)),
        name="in_projection",
    )(x, mod, g, w)


def _outproj_kernel(oa_ref, ob_ref, oc_ref, od_ref, w_ref, x_ref, mod_ref, gpost_ref, gpre_ref, y_ref, h_ref):
    for rows in _sub_tiles(x_ref.shape[0], SUB // 2):
        o = jnp.concatenate([oa_ref[rows, :], ob_ref[rows, :], oc_ref[rows, :], od_ref[rows, :]], axis=1)
        y = x_ref[rows, :] + mod_ref[2:3, :] * _rms(_dot(o, w_ref[...]), gpost_ref[...])
        y_ref[rows, :] = y
        h_ref[rows, :] = (_rms(y, gpre_ref[...]) * (1.0 + mod_ref[4:5, :]) + mod_ref[3:4, :]).astype(BF)


def _out_projection(o_parts, w, x, mod, mod_row, l, gpost, gpre, tm):
    m = x.shape[0]
    o_spec = pl.BlockSpec((tm, GROUP), lambda i: (i, 0))
    row_spec = pl.BlockSpec((tm, D_MODEL), lambda i: (i, 0))
    return pl.pallas_call(
        _outproj_kernel,
        grid=(m // tm,),
        in_specs=[
            o_spec, o_spec, o_spec, o_spec,
            pl.BlockSpec((D_MODEL, D_MODEL), lambda i: (0, 0), pipeline_mode=pl.Buffered(1)),
            row_spec,
            pl.BlockSpec((None, None, MOD_CHUNKS, D_MODEL), lambda i: (l, mod_row(i), 0, 0)),
            pl.BlockSpec((1, D_MODEL), lambda i: (0, 0)),
            pl.BlockSpec((1, D_MODEL), lambda i: (0, 0)),
        ],
        out_specs=[row_spec, row_spec],
        out_shape=[jax.ShapeDtypeStruct((m, D_MODEL), F32), jax.ShapeDtypeStruct((m, D_MODEL), BF)],
        compiler_params=_params(("arbitrary",), CTX_VMEM_LIMIT),
        name="out_projection",
    )(*o_parts, w, x, mod, gpost, gpre)


def _mlp_kernel(h_ref, x_ref, mod_ref, gpost_ref, wu_ref, wd_ref, y_ref):
    k = pl.program_id(1)
    last = pl.num_programs(1) - 1

    def ff(rows):
        u = jnp.square(jnp.maximum(_dot(h_ref[rows, :], wu_ref[...]), 0.0))
        return _dot(u.astype(BF), wd_ref[...])

    @pl.when(k == 0)
    def _():
        y_ref[...] = ff(slice(None))

    @pl.when((k > 0) & (k < last))
    def _():
        y_ref[...] += ff(slice(None))

    @pl.when(k == last)
    def _():
        for rows in _sub_tiles(x_ref.shape[0]):
            acc = y_ref[rows, :] + ff(rows)
            y_ref[rows, :] = x_ref[rows, :] + mod_ref[5:6, :] * _rms(acc, gpost_ref[...])


def _mlp(h, x, mod, mod_row, l, gpost, wu, wd, tm, tf):
    m = x.shape[0]
    assert D_FF // tf >= 2
    row_spec = pl.BlockSpec((tm, D_MODEL), lambda i, k: (i, 0))
    return pl.pallas_call(
        _mlp_kernel,
        grid=(m // tm, D_FF // tf),
        in_specs=[
            row_spec, row_spec,
            pl.BlockSpec((None, None, MOD_CHUNKS, D_MODEL), lambda i, k: (l, mod_row(i), 0, 0)),
            pl.BlockSpec((1, D_MODEL), lambda i, k: (0, 0)),
            pl.BlockSpec((D_MODEL, tf), lambda i, k: (0, k)),
            pl.BlockSpec((tf, D_MODEL), lambda i, k: (k, 0)),
        ],
        out_specs=row_spec,
        out_shape=jax.ShapeDtypeStruct((m, D_MODEL), F32),
        compiler_params=_params(("arbitrary", "arbitrary")),
        name="mlp",
    )(h, x, mod, gpost, wu, wd)


N_STATES = 7
STATE_HEADS = (2, 2, None, 2, 2, 4, 4)
RPS = 2


def _ctx_mixer_body(lam_init, z_ref, sink_ref, gkv_ref, wuk_ref, wuv_ref, gq_ref, gk_ref, lam_ref, gout_ref,
                    o_refs, st_refs):
    oa_ref, ob_ref, oc_ref, od_ref = o_refs
    ka_ref, va_ref, mla_ref, kc_ref, vc_ref, kd_ref, vd_ref = st_refs
    hd = HEAD_DIM

    c = HEAD_DIM ** -0.5 * LOG2E
    for hk in range(2):
        k = z_ref[:, 512 + hk * hd:512 + (hk + 1) * hd]
        v = z_ref[:, 768 + hk * hd:768 + (hk + 1) * hd]
        ka_ref[hk] = k
        va_ref[hk] = v
        kb, v1 = k.astype(BF), _with_ones(v.astype(BF))
        for g in range(2):
            h = hk * 2 + g
            q = (z_ref[:, h * hd:(h + 1) * hd] * c).astype(BF)
            oa_ref[:, h * hd:(h + 1) * hd] = _attend(q, kb, v1, sink_ref[h] * LOG2E).astype(BF)

    c = (HEAD_DIM + B_ROPE) ** -0.5 * LOG2E
    c_lat = _rms(z_ref[:, 1792:2048], gkv_ref[...])
    kr = z_ref[:, Z_MAIN:Z_MAIN + B_ROPE]
    mla_ref[:, 0:B_RANK] = c_lat
    mla_ref[:, B_RANK:B_RANK + B_ROPE] = kr
    cb, krb = c_lat.astype(BF), kr.astype(BF)
    for h in range(4):
        kn = _dot(cb, wuk_ref[:, h * hd:(h + 1) * hd]).astype(BF)
        v1 = _with_ones(_dot(cb, wuv_ref[:, h * hd:(h + 1) * hd]).astype(BF))
        qn = (z_ref[:, 1024 + h * hd:1024 + (h + 1) * hd] * c).astype(BF)
        qr = (z_ref[:, 1536 + h * B_ROPE:1536 + (h + 1) * B_ROPE] * c).astype(BF)
        s = _dot_nt(qn, kn) + _dot_nt(qr, krb)
        m = jnp.max(s, axis=-1, keepdims=True)
        o = _dot(jnp.exp2(s - m).astype(BF), v1)
        ob_ref[:, h * hd:(h + 1) * hd] = (o[:, :hd] / o[:, hd:]).astype(BF)

    c = HEAD_DIM ** -0.5 * LOG2E
    for hk in range(2):
        k = _rms(z_ref[:, 2560 + hk * hd:2560 + (hk + 1) * hd], gk_ref[...])
        v = z_ref[:, 2816 + hk * hd:2816 + (hk + 1) * hd]
        kc_ref[hk] = k
        vc_ref[hk] = v
        kb, v1 = k.astype(BF), _with_ones(v.astype(BF))
        for g in range(2):
            h = hk * 2 + g
            q = (_rms(z_ref[:, 2048 + h * hd:2048 + (h + 1) * hd], gq_ref[...]) * c).astype(BF)
            oc_ref[:, h * hd:(h + 1) * hd] = _attend(q, kb, v1).astype(BF)

    c = D_HALF ** -0.5 * LOG2E
    lam = _lam(lam_ref, lam_init)
    lane = lax.broadcasted_iota(jnp.int32, (SEQ, hd), 1)
    for h in range(4):
        k = z_ref[:, 3584 + h * hd:3584 + (h + 1) * hd]
        v = z_ref[:, 4096 + h * hd:4096 + (h + 1) * hd]
        kd_ref[h] = k
        vd_ref[h] = v
        kb, v1 = k.astype(BF), _with_ones(v.astype(BF))
        q = z_ref[:, 3072 + h * hd:3072 + (h + 1) * hd] * c
        o1 = _attend(jnp.where(lane < D_HALF, q, 0.0).astype(BF), kb, v1)
        o2 = _attend(jnp.where(lane >= D_HALF, q, 0.0).astype(BF), kb, v1)
        od_ref[:, h * hd:(h + 1) * hd] = (_rms(o1 - lam * o2, gout_ref[...]) * (1.0 - lam_init)).astype(BF)


def _ctx_layer_kernel(lam_init, n_alias, x_ref, mod_ref, g_ref, w_ref, sink_ref, gkv_ref, wuk_ref, wuv_ref, gq_ref, gk_ref,
                      lam_ref, gout_ref, *refs):
    refs = refs[n_alias:]
    o_refs, st_refs, z_s = refs[:4], refs[4:4 + N_STATES], refs[4 + N_STATES]
    if not n_alias:
        for s in st_refs:
            s[:, 1:] = jnp.zeros((RPS, DEPTH - 1) + s.shape[2:], F32)
        st_refs = [s.at[:, 0] for s in st_refs]
    for r in range(RPS):
        rows = slice(r * SEQ, (r + 1) * SEQ)
        h = _rms(x_ref[rows, :], g_ref[...]) * (1.0 + mod_ref[1:2, :]) + mod_ref[0:1, :]
        z_s[rows, :] = _dot(h.astype(BF), w_ref[...])
    for r in range(RPS):
        rows = slice(r * SEQ, (r + 1) * SEQ)
        _ctx_mixer_body(lam_init, z_s.at[rows, :], sink_ref, gkv_ref, wuk_ref, wuv_ref, gq_ref, gk_ref, lam_ref, gout_ref,
                        [o.at[rows, :] for o in o_refs], [s.at[r] for s in st_refs])


def _const(shape):
    return pl.BlockSpec(shape, lambda *_: (0,) * len(shape))


def _ctx_layer(x, mod, l, g, w, p, lam_init, prev_states):
    m = x.shape[0]
    o_spec = pl.BlockSpec((RPS * SEQ, GROUP), lambda b: (b, 0))
    o_shape = jax.ShapeDtypeStruct((m, GROUP), BF)

    def st_dims(h):
        return (SEQ, B_RANK + B_ROPE) if h is None else (h, SEQ, HEAD_DIM)

    def st_spec(h):
        dims = st_dims(h)
        if l == 0:
            return pl.BlockSpec((RPS, DEPTH) + dims, lambda b: (b, 0) + (0,) * len(dims))
        return pl.BlockSpec((RPS, None) + dims, lambda b: (b, l) + (0,) * len(dims))

    in_specs = [
        pl.BlockSpec((RPS * SEQ, D_MODEL), lambda b: (b, 0)),
        pl.BlockSpec((None, None, MOD_CHUNKS, D_MODEL), lambda b: (l, 0, 0, 0)),
        _const((1, D_MODEL)),
        pl.BlockSpec((None, D_MODEL, Z_WIDTH), lambda b: (l, 0, 0), pipeline_mode=pl.Buffered(1)),
        pl.BlockSpec(memory_space=pltpu.SMEM),
        _const((1, B_RANK)), _const((B_RANK, GROUP)), _const((B_RANK, GROUP)),
        _const((1, HEAD_DIM)), _const((1, HEAD_DIM)), _const((4, D_HALF)), _const((1, HEAD_DIM)),
    ]
    n_alias = len(prev_states)
    aliases = {len(in_specs) + k: 4 + k for k in range(n_alias)}
    return pl.pallas_call(
        functools.partial(_ctx_layer_kernel, lam_init, n_alias),
        grid=(BATCH // RPS,),
        in_specs=in_specs + [pl.BlockSpec(memory_space=pl.ANY)] * n_alias,
        out_specs=[o_spec] * 4 + [st_spec(h) for h in STATE_HEADS],
        out_shape=[o_shape] * 4 + [jax.ShapeDtypeStruct((BATCH, DEPTH) + st_dims(h), F32) for h in STATE_HEADS],
        scratch_shapes=[pltpu.VMEM((RPS * SEQ, Z_WIDTH), F32)],
        input_output_aliases=aliases,
        compiler_params=_params(("arbitrary",), CTX_VMEM_LIMIT if l == 0 else VMEM_LIMIT),
        name="ctx_layer",
    )(x, mod, g, w, p['a_sink'], p['b_g_kv'], p['w_uk'], p['w_uv'], p['c_gq'], p['c_gk'], p['d_lam'], p['d_g_out'],
      *prev_states)


QBLK = 256
N_QBLK = DEC_SEQ // QBLK
N_WBLK = DEC_SEQ // WINDOW


def _lat_a_kernel(z_ref, ck_ref, cv_ref, cos_ref, sin_ref, sink_ref, o_ref, q_s, k_s, v_s):
    hd = HEAD_DIM
    cos, sin = cos_ref[...], sin_ref[...]
    c = HEAD_DIM ** -0.5 * LOG2E
    r_io = lax.broadcasted_iota(jnp.int32, (2 * WINDOW, 3 * WINDOW), 0) & (WINDOW - 1)
    c_io = lax.broadcasted_iota(jnp.int32, (2 * WINDOW, 3 * WINDOW), 1)
    second = lax.broadcasted_iota(jnp.int32, (2 * WINDOW, 1), 0) >= WINDOW

    def prepare(hk):
        k_s[hk, 0:WINDOW, :] = jnp.zeros((WINDOW, hd), BF)
        k_s[hk, WINDOW + DEC_SEQ:, :] = jnp.zeros((WINDOW, hd), BF)
        v_s[hk, 0:WINDOW, :] = jnp.zeros((WINDOW, 2 * hd), BF)
        v_s[hk, WINDOW + DEC_SEQ:, :] = jnp.zeros((WINDOW, 2 * hd), BF)
        k = _rope(z_ref[:, 512 + hk * hd:512 + (hk + 1) * hd], cos, sin, 32)
        k_s[hk, WINDOW:WINDOW + DEC_SEQ, :] = k.astype(BF)
        v_s[hk, WINDOW:WINDOW + DEC_SEQ, :] = _with_ones(z_ref[:, 768 + hk * hd:768 + (hk + 1) * hd].astype(BF))
        for g in range(2):
            h = hk * 2 + g
            q = (_rope(z_ref[:, h * hd:(h + 1) * hd], cos, sin, 32) * c).astype(BF)
            for i in range(N_WBLK):
                q_s[hk, i, g * WINDOW:(g + 1) * WINDOW, :] = q[i * WINDOW:(i + 1) * WINDOW, :]

    def attend(hk, i):
        r0 = i * WINDOW
        kc = ck_ref[hk].astype(BF)
        vc1 = _with_ones(cv_ref[hk].astype(BF))
        sink2 = jnp.where(second, sink_ref[2 * hk + 1] * LOG2E, sink_ref[2 * hk] * LOG2E)
        q = q_s[hk, i]
        kl = k_s[hk, r0:r0 + 3 * WINDOW, :]
        vl1 = v_s[hk, r0:r0 + 3 * WINDOW, :]
        kpos = (i - 1) * WINDOW + c_io
        qpos = i * WINDOW + r_io
        ok = (jnp.abs(qpos - kpos) <= WINDOW) & (kpos >= 0) & (kpos < DEC_SEQ)
        sc = _dot_nt(q, kc)
        sl = _dot_nt(q, kl)
        m = jnp.maximum(jnp.maximum(jnp.max(sc, axis=-1, keepdims=True),
                                    jnp.max(jnp.where(ok, sl, NEG), axis=-1, keepdims=True)), sink2)
        pw = jnp.where(ok, jnp.exp2(sl - m), 0.0)
        o = _dot(jnp.exp2(sc - m).astype(BF), vc1) + _dot(pw.astype(BF), vl1)
        o = (o[:, :hd] / (o[:, hd:] + jnp.exp2(sink2 - m))).astype(BF)
        o_ref[r0:r0 + WINDOW, 2 * hk * hd:(2 * hk + 1) * hd] = o[0:WINDOW, :]
        o_ref[r0:r0 + WINDOW, (2 * hk + 1) * hd:(2 * hk + 2) * hd] = o[WINDOW:, :]

    prepare(0)
    attend(0, 0)
    prepare(1)
    for i in range(1, N_WBLK):
        attend(0, i)
    for i in range(N_WBLK):
        attend(1, i)


def _lat_b_kernel(z_ref, zr_ref, mla_ref, gkv_ref, wuk_ref, wuv_ref, cos_ref, sin_ref, o_ref, c_s, kr_s, k_s, v_s, q_s):
    hd = HEAD_DIM
    cos, sin = cos_ref[...], sin_ref[...]
    c = (HEAD_DIM + B_ROPE) ** -0.5 * LOG2E
    c_s[0:PAST_LEN, :] = mla_ref[:, 0:B_RANK].astype(BF)
    c_s[PAST_LEN:, :] = _rms(z_ref[:, 768:1024], gkv_ref[...]).astype(BF)
    kr_s[0:PAST_LEN, B_ROPE:] = jnp.zeros((PAST_LEN, hd - B_ROPE), BF)
    kr_s[0:PAST_LEN, 0:B_ROPE] = mla_ref[:, B_RANK:B_RANK + B_ROPE].astype(BF)
    kr_s[PAST_LEN:, :] = _rope(zr_ref[...], cos, sin, 16).astype(BF)
    lane = lax.broadcasted_iota(jnp.int32, (DEC_SEQ, hd), 1)

    def prepare(h):
        k_s[h, :, 0:hd] = _dot(c_s[...], wuk_ref[:, h * hd:(h + 1) * hd]).astype(BF)
        k_s[h, :, hd:2 * hd] = kr_s[...]
        v_s[h] = _with_ones(_dot(c_s[...], wuv_ref[:, h * hd:(h + 1) * hd]).astype(BF))
        q_s[h, :, 0:hd] = (z_ref[:, h * hd:(h + 1) * hd] * c).astype(BF)
        qr = _rope(z_ref[:, 512 + (h // 2) * hd:512 + (h // 2 + 1) * hd], cos, sin, 16) * c
        if h % 2:
            qr = pltpu.roll(qr, B_ROPE, 1)
        q_s[h, :, hd:2 * hd] = jnp.where(lane < B_ROPE, qr, 0.0).astype(BF)

    def attend(h):
        for i in range(N_QBLK):
            r0 = i * QBLK
            o = _attend(q_s[h, r0:r0 + QBLK, :], k_s[h], v_s[h])
            o_ref[r0:r0 + QBLK, h * hd:(h + 1) * hd] = o.astype(BF)

    _one_head_ahead(prepare, attend, 4)


def _one_head_ahead(prepare, attend, n):
    prepare(0)
    for h in range(n):
        if h + 1 < n:
            prepare(h + 1)
        attend(h)


def _lat_c_kernel(z_ref, ck_ref, cv_ref, gq_ref, gk_ref, cos_ref, sin_ref, o_ref, k_s, v_s, q_s):
    hd = HEAD_DIM
    cos, sin = cos_ref[...], sin_ref[...]
    c = HEAD_DIM ** -0.5 * LOG2E

    def prepare(h):
        if h % 2 == 0:
            hk = h // 2
            k_s[hk, 0:PAST_LEN, :] = ck_ref[hk].astype(BF)
            k = _rms(z_ref[:, 512 + hk * hd:512 + (hk + 1) * hd], gk_ref[...])
            k_s[hk, PAST_LEN:, :] = _rope(k, cos, sin, 32).astype(BF)
            v_s[hk, 0:PAST_LEN, :] = _with_ones(cv_ref[hk].astype(BF))
            v_s[hk, PAST_LEN:, :] = _with_ones(z_ref[:, 768 + hk * hd:768 + (hk + 1) * hd].astype(BF))
        q = _rms(z_ref[:, h * hd:(h + 1) * hd], gq_ref[...])
        q_s[h] = (_rope(q, cos, sin, 32) * c).astype(BF)

    def attend(h):
        for i in range(N_QBLK):
            r0 = i * QBLK
            o = _attend(q_s[h, r0:r0 + QBLK, :], k_s[h // 2], v_s[h // 2])
            o_ref[r0:r0 + QBLK, h * hd:(h + 1) * hd] = o.astype(BF)

    _one_head_ahead(prepare, attend, 4)


def _lat_d_kernel(lam_init, z_ref, dk_ref, dv_ref, lam_ref, gout_ref, cos_ref, sin_ref, o_ref, k_s, v_s, q1_s, q2_s):
    hd = HEAD_DIM
    cos, sin = cos_ref[...], sin_ref[...]
    c = D_HALF ** -0.5 * LOG2E
    lam = _lam(lam_ref, lam_init)
    lane = lax.broadcasted_iota(jnp.int32, (DEC_SEQ, hd), 1)

    def prepare(h):
        k_s[h, 0:PAST_LEN, :] = dk_ref[h].astype(BF)
        k_s[h, PAST_LEN:, :] = _rope(z_ref[:, 512 + h * hd:512 + (h + 1) * hd], cos, sin, 16).astype(BF)
        v_s[h, 0:PAST_LEN, :] = _with_ones(dv_ref[h].astype(BF))
        v_s[h, PAST_LEN:, :] = _with_ones(z_ref[:, 1024 + h * hd:1024 + (h + 1) * hd].astype(BF))
        q = _rope(z_ref[:, h * hd:(h + 1) * hd], cos, sin, 16) * c
        q1_s[h] = jnp.where(lane < D_HALF, q, 0.0).astype(BF)
        q2_s[h] = jnp.where(lane >= D_HALF, q, 0.0).astype(BF)

    def attend(h):
        for i in range(N_QBLK):
            r0 = i * QBLK
            o1 = _attend(q1_s[h, r0:r0 + QBLK, :], k_s[h], v_s[h])
            o2 = _attend(q2_s[h, r0:r0 + QBLK, :], k_s[h], v_s[h])
            o = _rms(o1 - lam * o2, gout_ref[...]) * (1.0 - lam_init)
            o_ref[r0:r0 + QBLK, h * hd:(h + 1) * hd] = o.astype(BF)

    _one_head_ahead(prepare, attend, 4)


def _lat_mixers(z, caches, l, p, tabs, lam_init, casts):
    m = z.shape[0]
    hd = HEAD_DIM
    cos128, sin128, cos64, sin64 = tabs
    o_spec = pl.BlockSpec((DEC_SEQ, GROUP), lambda b: (b, 0))
    o_shape = jax.ShapeDtypeStruct((m, GROUP), BF)
    tab = _const((DEC_SEQ, hd))

    def cache_spec(h):
        return pl.BlockSpec((None, None, h, PAST_LEN, hd), lambda b: (b, l, 0, 0, 0))

    def zspec(width, col0):
        return pl.BlockSpec((DEC_SEQ, width), lambda b: (b, col0 // width))

    def scratch(*shape):
        return pltpu.VMEM(shape, BF)

    def cast_specs(w, layer, axis):
        r, c = w.shape[1:]
        if axis == 'rows':
            blk = (r // DEC_BATCH, c)
            return pl.BlockSpec((None,) + blk, lambda b: (layer, b, 0)), pl.BlockSpec(blk, lambda b: (b, 0))
        blk = (r, c // DEC_BATCH)
        return pl.BlockSpec((None,) + blk, lambda b: (layer, 0, b)), pl.BlockSpec(blk, lambda b: (0, b))

    def call(kernel, name, in_specs, args, scratch_shapes):
        todo = casts.get(name, [])
        specs = [cast_specs(*t) for t in todo]
        if todo:
            kernel = _also_cast(kernel, len(in_specs), len(todo))
        outs = pl.pallas_call(
            kernel, grid=(DEC_BATCH,),
            in_specs=in_specs + [s[0] for s in specs],
            out_specs=[o_spec] + [s[1] for s in specs],
            out_shape=[o_shape] + [jax.ShapeDtypeStruct(t[0].shape[1:], BF) for t in todo],
            scratch_shapes=scratch_shapes, compiler_params=_params(("arbitrary",)), name=name,
        )(*args, *[t[0] for t in todo])
        return outs[0], list(outs[1:])

    o_a, c_a = call(_lat_a_kernel, "lat_mixer_a",
                    [zspec(1024, 0), cache_spec(2), cache_spec(2), tab, tab, pl.BlockSpec(memory_space=pltpu.SMEM)],
                    (z, caches['a_k'], caches['a_v'], cos128, sin128, p['a_sink']),
                    [scratch(2, N_WBLK, 2 * WINDOW, hd), scratch(2, DEC_SEQ + 2 * WINDOW, hd),
                     scratch(2, DEC_SEQ + 2 * WINDOW, 2 * hd)])

    o_b, c_b = call(_lat_b_kernel, "lat_mixer_b",
                    [zspec(1024, 1024), zspec(LANES, Z_MAIN),
                     pl.BlockSpec((None, None, PAST_LEN, B_RANK + B_ROPE), lambda b: (b, l, 0, 0)),
                     _const((1, B_RANK)), _const((B_RANK, GROUP)), _const((B_RANK, GROUP)), tab, tab],
                    (z, z, caches['mla'], p['b_g_kv'], p['w_uk'], p['w_uv'], cos64, sin64),
                    [scratch(KEYS, B_RANK), scratch(KEYS, hd), scratch(4, KEYS, 2 * hd), scratch(4, KEYS, 2 * hd),
                     scratch(4, DEC_SEQ, 2 * hd)])

    o_c, c_c = call(_lat_c_kernel, "lat_mixer_c",
                    [zspec(1024, 2048), cache_spec(2), cache_spec(2), _const((1, hd)), _const((1, hd)), tab, tab],
                    (z, caches['c_k'], caches['c_v'], p['c_gq'], p['c_gk'], cos128, sin128),
                    [scratch(2, KEYS, hd), scratch(2, KEYS, 2 * hd), scratch(4, DEC_SEQ, hd)])

    o_d, c_d = call(functools.partial(_lat_d_kernel, lam_init), "lat_mixer_d",
                    [zspec(1536, 3072), cache_spec(4), cache_spec(4), _const((4, D_HALF)), _const((1, hd)), tab, tab],
                    (z, caches['d_k'], caches['d_v'], p['d_lam'], p['d_g_out'], cos64, sin64),
                    [scratch(4, KEYS, hd), scratch(4, KEYS, 2 * hd), scratch(4, DEC_SEQ, hd), scratch(4, DEC_SEQ, hd)])

    return (o_a, o_b, o_c, o_d), {"lat_mixer_a": c_a, "lat_mixer_b": c_b, "lat_mixer_c": c_c, "lat_mixer_d": c_d}


def _rope_tables():
    t = jnp.arange(DEC_SEQ)
    row = (t // GRID_W).astype(F32)[:, None]
    col = (t % GRID_W).astype(F32)[:, None]

    def table(half):
        inv = ROPE_BASE ** (-jnp.arange(half, dtype=F32) / half)
        ar, ac = row * inv[None, :], col * inv[None, :]
        cos = jnp.concatenate([jnp.cos(ar), jnp.cos(ar), jnp.cos(ac), jnp.cos(ac)], axis=-1)
        sin = jnp.concatenate([-jnp.sin(ar), jnp.sin(ar), -jnp.sin(ac), jnp.sin(ac)], axis=-1)
        return cos, sin

    cos128, sin128 = table(32)
    cos64, sin64 = table(16)
    return cos128, sin128, jnp.tile(cos64, (1, 2)), jnp.tile(sin64, (1, 2))


def kernel(x_prompt, x_sample, cache_a_k, cache_a_v, cache_mla, cache_c_k, cache_c_v, cache_d_k, cache_d_v,
           c, c_ctx, w_ada, b_ada, g_attn_pre, g_attn_post, g_mlp_pre, g_mlp_post, w_in, a_sink,
           b_g_kv, b_w_uk, b_w_uv, c_gq, c_gk, d_lam, d_g_out, w_out, w_up, w_down):
    w_in_p = _regroup_w_in(jnp.concatenate(
        [w_in.astype(BF), jnp.zeros((DEPTH, D_MODEL, Z_WIDTH - W_IN_COLS), BF)], axis=-1))
    w_uk_b = b_w_uk.reshape(DEPTH, B_RANK, GROUP).astype(BF)
    w_uv_b = b_w_uv.reshape(DEPTH, B_RANK, GROUP).astype(BF)
    assert DEPTH == 2
    first_layer_casts = {"lat_mixer_a": [(w_out, 1, 'rows'), (w_down, 0, 'rows')],
                         "lat_mixer_b": [(w_up, 0, 'cols'), (w_out, 0, 'rows')],
                         "lat_mixer_c": [(w_down, 1, 'rows')],
                         "lat_mixer_d": [(w_up, 1, 'cols')]}

    cvec = jnp.concatenate([c_ctx[None, :], c, jnp.zeros((MOD_ROWS - 1 - DEC_BATCH, D_MODEL), F32)], axis=0)
    mod = _modulation(cvec, w_ada, b_ada).reshape(DEPTH, MOD_ROWS, MOD_CHUNKS, D_MODEL)
    tabs = _rope_tables()

    tm = 512
    tm_out = 1024
    tm_mlp, tf_mlp = 1024, 512
    ctx_row = lambda tm: (lambda i: 0)
    lat_row = lambda tm: (lambda i: 1 + (i * tm) // DEC_SEQ)
    caches = {'a_k': cache_a_k, 'a_v': cache_a_v, 'mla': cache_mla, 'c_k': cache_c_k, 'c_v': cache_c_v,
              'd_k': cache_d_k, 'd_v': cache_d_v}

    xc = x_prompt.reshape(BATCH * SEQ, D_MODEL)
    xl = x_sample.reshape(DEC_BATCH * DEC_SEQ, D_MODEL)
    states = ()
    w_out_b, w_down_b, w_up_b = {}, {}, {}
    for l in range(DEPTH):
        lam_init = 0.8 - 0.6 * math.exp(-0.3 * l)
        p = {'a_sink': a_sink[l], 'b_g_kv': b_g_kv[l][None], 'w_uk': w_uk_b[l], 'w_uv': w_uv_b[l],
             'c_gq': c_gq[l][None], 'c_gk': c_gk[l][None], 'd_lam': d_lam[l], 'd_g_out': d_g_out[l][None]}
        g_pre, g_post = g_attn_pre[l][None], g_attn_post[l][None]
        g_mpre, g_mpost = g_mlp_pre[l][None], g_mlp_post[l][None]

        zl = _in_projection(xl, mod, lat_row(tm), l, g_pre, w_in_p, tm)
        o_parts, cast = _lat_mixers(zl, caches, l, p, tabs, lam_init, first_layer_casts if l == 0 else {})
        if l == 0:
            (w_out_b[1], w_down_b[0]), (w_up_b[0], w_out_b[0]) = cast["lat_mixer_a"], cast["lat_mixer_b"]
            (w_down_b[1],), (w_up_b[1],) = cast["lat_mixer_c"], cast["lat_mixer_d"]

        outs = _ctx_layer(xc, mod, l, g_pre, w_in_p, p, lam_init, states)
        states = tuple(outs[4:])
        xc, hc = _out_projection(outs[:4], w_out_b[l], xc, mod, ctx_row(tm_out), l, g_post, g_mpre, tm_out)
        xc = _mlp(hc, xc, mod, ctx_row(tm_mlp), l, g_mpost, w_up_b[l], w_down_b[l], tm_mlp, tf_mlp)

        xl, hl = _out_projection(o_parts, w_out_b[l], xl, mod, lat_row(tm_out), l, g_post, g_mpre, tm_out)
        xl = _mlp(hl, xl, mod, lat_row(tm_mlp), l, g_mpost, w_up_b[l], w_down_b[l], tm_mlp, tf_mlp)

    y_prompt = xc.reshape(BATCH, SEQ, D_MODEL)
    y_sample = xl.reshape(DEC_BATCH, DEC_SEQ, D_MODEL)
    return (y_prompt, y_sample) + states
```

```python
import functools
import math

import jax
import jax.numpy as jnp
from jax import lax
from jax.experimental import pallas as pl
from jax.experimental.pallas import tpu as pltpu

D_MODEL = 2048
BATCH = 32
SEQ = 256
DEPTH = 2
DEC_BATCH = 8
DEC_SEQ = 1024
PAST_LEN = 256
GRID_W = 64
HEAD_DIM = 128
WINDOW = 128
B_ROPE = 64
B_RANK = 256
D_HALF = 64
D_FF = 4 * D_MODEL
ROPE_BASE = 10000.0
EPS = 1e-6
MOD_CHUNKS = 6
MOD_ROWS = 16

GROUP = 512
Z_MAIN = 4608
Z_WIDTH = 4864
KEYS = PAST_LEN + DEC_SEQ

LANES = 128
VMEM_LIMIT = 56 * 1024 * 1024
CTX_VMEM_LIMIT = 62 * 1024 * 1024

BF = jnp.bfloat16
F32 = jnp.float32
NEG = -1e30
LOG2E = math.log2(math.e)


def _dot(a, b):
    return jnp.dot(a, b, preferred_element_type=F32)


def _dot_nt(a, b):
    return lax.dot_general(a, b, (((1,), (1,)), ((), ())), preferred_element_type=F32)


def _rms(x, g):
    return x * lax.rsqrt(jnp.mean(x * x, axis=-1, keepdims=True) + EPS) * g


def _params(sem, vmem_limit=VMEM_LIMIT):
    return pltpu.CompilerParams(dimension_semantics=sem, vmem_limit_bytes=vmem_limit)


def _with_ones(v):
    return jnp.concatenate([v, jnp.ones(v.shape, v.dtype)], axis=1)


def _attend(q, k, v1, sink2=None):
    s = _dot_nt(q, k)
    m = jnp.max(s, axis=-1, keepdims=True)
    if sink2 is not None:
        m = jnp.maximum(m, sink2)
    o = _dot(jnp.exp2(s - m).astype(BF), v1)
    l = o[:, HEAD_DIM:]
    if sink2 is not None:
        l = l + jnp.exp2(sink2 - m)
    return o[:, :HEAD_DIM] / l


def _rope(x, cos, sin_signed, half):
    lane = lax.broadcasted_iota(jnp.int32, (8, LANES), 1)
    from_up = pltpu.roll(lane, half, 1)[0:1, :] == (lane[0:1, :] ^ half)
    partner = jnp.where(from_up, pltpu.roll(x, half, 1), pltpu.roll(x, LANES - half, 1))
    return x * cos + partner * sin_signed


def _lam(lam_ref, lam_init):
    lp = lam_ref[...]
    a = jnp.sum(lp[0:1, :] * lp[1:2, :], axis=-1, keepdims=True)
    b = jnp.sum(lp[2:3, :] * lp[3:4, :], axis=-1, keepdims=True)
    return jnp.exp(a) - jnp.exp(b) + lam_init


def _mod_kernel(c_ref, w_ref, b_ref, o_ref):
    c = c_ref[...]
    s = c * jax.nn.sigmoid(c)
    o_ref[...] = _dot(s.astype(BF), w_ref[...].astype(BF)) + b_ref[...]


def _modulation(cvec, w_ada, b_ada):
    tn = 1024
    n = MOD_CHUNKS * D_MODEL
    return pl.pallas_call(
        _mod_kernel,
        grid=(DEPTH, n // tn),
        in_specs=[
            pl.BlockSpec((MOD_ROWS, D_MODEL), lambda l, j: (0, 0)),
            pl.BlockSpec((None, D_MODEL, tn), lambda l, j: (l, 0, j)),
            pl.BlockSpec((None, 1, tn), lambda l, j: (l, 0, j)),
        ],
        out_specs=pl.BlockSpec((None, MOD_ROWS, tn), lambda l, j: (l, 0, j)),
        out_shape=jax.ShapeDtypeStruct((DEPTH, MOD_ROWS, n), F32),
        compiler_params=_params(("arbitrary", "arbitrary")),
        name="modulation",
    )(cvec, w_ada, b_ada.reshape(DEPTH, 1, n))


W_IN_COLS = 4672
KR_COL = 2048


def _regroup_kernel(w_ref, o_ref):
    o_ref[:, 0:KR_COL] = w_ref[:, 0:KR_COL]
    o_ref[:, KR_COL:Z_MAIN] = w_ref[:, KR_COL + B_ROPE:W_IN_COLS]
    o_ref[:, Z_MAIN:Z_MAIN + B_ROPE] = w_ref[:, KR_COL:KR_COL + B_ROPE]
    o_ref[:, Z_MAIN + B_ROPE:] = jnp.zeros((w_ref.shape[0], Z_WIDTH - Z_MAIN - B_ROPE), BF)


def _regroup_w_in(w_in_b):
    tr = 256
    return pl.pallas_call(
        _regroup_kernel,
        grid=(DEPTH, D_MODEL // tr),
        in_specs=[pl.BlockSpec((None, tr, Z_WIDTH), lambda l, i: (l, i, 0))],
        out_specs=pl.BlockSpec((None, tr, Z_WIDTH), lambda l, i: (l, i, 0)),
        out_shape=jax.ShapeDtypeStruct((DEPTH, D_MODEL, Z_WIDTH), BF),
        compiler_params=_params(("arbitrary", "arbitrary")),
        name="regroup_w_in",
    )(w_in_b)


def _also_cast(kernel, n_in, n_cast):
    def wrapped(*refs):
        for k in range(n_cast):
            refs[n_in + n_cast + 1 + k][...] = refs[n_in + k][...].astype(BF)
        kernel(*refs[:n_in], refs[n_in + n_cast], *refs[n_in + 2 * n_cast + 1:])
    return wrapped


SUB = 256


def _sub_tiles(rows, sub=SUB):
    return [slice(s, s + sub) for s in range(0, rows, sub)]


def _inproj_kernel(x_ref, mod_ref, g_ref, w_ref, z_ref):
    for rows in _sub_tiles(x_ref.shape[0]):
        h = _rms(x_ref[rows, :], g_ref[...]) * (1.0 + mod_ref[1:2, :]) + mod_ref[0:1, :]
        z_ref[rows, :] = _dot(h.astype(BF), w_ref[...])


def _in_projection(x, mod, mod_row, l, g, w, tm):
    m = x.shape[0]
    return pl.pallas_call(
        _inproj_kernel,
        grid=(m // tm,),
        in_specs=[
            pl.BlockSpec((tm, D_MODEL), lambda i: (i, 0)),
            pl.BlockSpec((None, None, MOD_CHUNKS, D_MODEL), lambda i: (l, mod_row(i), 0, 0)),
            pl.BlockSpec((1, D_MODEL), lambda i: (0, 0)),
            pl.BlockSpec((None, D_MODEL, Z_WIDTH), lambda i: (l, 0, 0), pipeline_mode=pl.Buffered(1)),
        ],
        out_specs=pl.BlockSpec((tm, Z_WIDTH), lambda i: (i, 0)),
        out_shape=jax.ShapeDtypeStruct((m, Z_WIDTH), F32),
        compiler_params=_params(("arbitrary",)),
        name="in_projection",
    )(x, mod, g, w)


def _outproj_kernel(oa_ref, ob_ref, oc_ref, od_ref, w_ref, x_ref, mod_ref, gpost_ref, gpre_ref, y_ref, h_ref):
    for rows in _sub_tiles(x_ref.shape[0], SUB // 2):
        o = jnp.concatenate([oa_ref[rows, :], ob_ref[rows, :], oc_ref[rows, :], od_ref[rows, :]], axis=1)
        y = x_ref[rows, :] + mod_ref[2:3, :] * _rms(_dot(o, w_ref[...]), gpost_ref[...])
        y_ref[rows, :] = y
        h_ref[rows, :] = (_rms(y, gpre_ref[...]) * (1.0 + mod_ref[4:5, :]) + mod_ref[3:4, :]).astype(BF)


def _out_projection(o_parts, w, x, mod, mod_row, l, gpost, gpre, tm):
    m = x.shape[0]
    o_spec = pl.BlockSpec((tm, GROUP), lambda i: (i, 0))
    row_spec = pl.BlockSpec((tm, D_MODEL), lambda i: (i, 0))
    return pl.pallas_call(
        _outproj_kernel,
        grid=(m // tm,),
        in_specs=[
            o_spec, o_spec, o_spec, o_spec,
            pl.BlockSpec((D_MODEL, D_MODEL), lambda i: (0, 0)),
            row_spec,
            pl.BlockSpec((None, None, MOD_CHUNKS, D_MODEL), lambda i: (l, mod_row(i), 0, 0)),
            pl.BlockSpec((1, D_MODEL), lambda i: (0, 0)),
            pl.BlockSpec((1, D_MODEL), lambda i: (0, 0)),
        ],
        out_specs=[row_spec, row_spec],
        out_shape=[jax.ShapeDtypeStruct((m, D_MODEL), F32), jax.ShapeDtypeStruct((m, D_MODEL), BF)],
        compiler_params=_params(("arbitrary",)),
        name="out_projection",
    )(*o_parts, w, x, mod, gpost, gpre)


def _mlp_kernel(h_ref, x_ref, mod_ref, gpost_ref, wu_ref, wd_ref, y_ref):
    k = pl.program_id(1)
    last = pl.num_programs(1) - 1

    def ff(rows):
        u = jnp.square(jnp.maximum(_dot(h_ref[rows, :], wu_ref[...]), 0.0))
        return _dot(u.astype(BF), wd_ref[...])

    @pl.when(k == 0)
    def _():
        y_ref[...] = ff(slice(None))

    @pl.when((k > 0) & (k < last))
    def _():
        y_ref[...] += ff(slice(None))

    @pl.when(k == last)
    def _():
        for rows in _sub_tiles(x_ref.shape[0]):
            acc = y_ref[rows, :] + ff(rows)
            y_ref[rows, :] = x_ref[rows, :] + mod_ref[5:6, :] * _rms(acc, gpost_ref[...])


def _mlp(h, x, mod, mod_row, l, gpost, wu, wd, tm, tf):
    m = x.shape[0]
    assert D_FF // tf >= 2
    row_spec = pl.BlockSpec((tm, D_MODEL), lambda i, k: (i, 0))
    return pl.pallas_call(
        _mlp_kernel,
        grid=(m // tm, D_FF // tf),
        in_specs=[
            row_spec, row_spec,
            pl.BlockSpec((None, None, MOD_CHUNKS, D_MODEL), lambda i, k: (l, mod_row(i), 0, 0)),
            pl.BlockSpec((1, D_MODEL), lambda i, k: (0, 0)),
            pl.BlockSpec((D_MODEL, tf), lambda i, k: (0, k)),
            pl.BlockSpec((tf, D_MODEL), lambda i, k: (k, 0)),
        ],
        out_specs=row_spec,
        out_shape=jax.ShapeDtypeStruct((m, D_MODEL), F32),
        compiler_params=_params(("arbitrary", "arbitrary")),
        name="mlp",
    )(h, x, mod, gpost, wu, wd)


N_STATES = 7
STATE_HEADS = (2, 2, None, 2, 2, 4, 4)
RPS = 2


def _ctx_mixer_body(lam_init, z_ref, sink_ref, gkv_ref, wuk_ref, wuv_ref, gq_ref, gk_ref, lam_ref, gout_ref,
                    o_refs, st_refs):
    oa_ref, ob_ref, oc_ref, od_ref = o_refs
    ka_ref, va_ref, mla_ref, kc_ref, vc_ref, kd_ref, vd_ref = st_refs
    hd = HEAD_DIM

    c = HEAD_DIM ** -0.5 * LOG2E
    for hk in range(2):
        k = z_ref[:, 512 + hk * hd:512 + (hk + 1) * hd]
        v = z_ref[:, 768 + hk * hd:768 + (hk + 1) * hd]
        ka_ref[hk] = k
        va_ref[hk] = v
        kb, v1 = k.astype(BF), _with_ones(v.astype(BF))
        for g in range(2):
            h = hk * 2 + g
            q = (z_ref[:, h * hd:(h + 1) * hd] * c).astype(BF)
            oa_ref[:, h * hd:(h + 1) * hd] = _attend(q, kb, v1, sink_ref[h] * LOG2E).astype(BF)

    c = (HEAD_DIM + B_ROPE) ** -0.5 * LOG2E
    c_lat = _rms(z_ref[:, 1792:2048], gkv_ref[...])
    kr = z_ref[:, Z_MAIN:Z_MAIN + B_ROPE]
    mla_ref[:, 0:B_RANK] = c_lat
    mla_ref[:, B_RANK:B_RANK + B_ROPE] = kr
    cb, krb = c_lat.astype(BF), kr.astype(BF)
    for h in range(4):
        kn = _dot(cb, wuk_ref[:, h * hd:(h + 1) * hd]).astype(BF)
        v1 = _with_ones(_dot(cb, wuv_ref[:, h * hd:(h + 1) * hd]).astype(BF))
        qn = (z_ref[:, 1024 + h * hd:1024 + (h + 1) * hd] * c).astype(BF)
        qr = (z_ref[:, 1536 + h * B_ROPE:1536 + (h + 1) * B_ROPE] * c).astype(BF)
        s = _dot_nt(qn, kn) + _dot_nt(qr, krb)
        m = jnp.max(s, axis=-1, keepdims=True)
        o = _dot(jnp.exp2(s - m).astype(BF), v1)
        ob_ref[:, h * hd:(h + 1) * hd] = (o[:, :hd] / o[:, hd:]).astype(BF)

    c = HEAD_DIM ** -0.5 * LOG2E
    for hk in range(2):
        k = _rms(z_ref[:, 2560 + hk * hd:2560 + (hk + 1) * hd], gk_ref[...])
        v = z_ref[:, 2816 + hk * hd:2816 + (hk + 1) * hd]
        kc_ref[hk] = k
        vc_ref[hk] = v
        kb, v1 = k.astype(BF), _with_ones(v.astype(BF))
        for g in range(2):
            h = hk * 2 + g
            q = (_rms(z_ref[:, 2048 + h * hd:2048 + (h + 1) * hd], gq_ref[...]) * c).astype(BF)
            oc_ref[:, h * hd:(h + 1) * hd] = _attend(q, kb, v1).astype(BF)

    c = D_HALF ** -0.5 * LOG2E
    lam = _lam(lam_ref, lam_init)
    lane = lax.broadcasted_iota(jnp.int32, (SEQ, hd), 1)
    for h in range(4):
        k = z_ref[:, 3584 + h * hd:3584 + (h + 1) * hd]
        v = z_ref[:, 4096 + h * hd:4096 + (h + 1) * hd]
        kd_ref[h] = k
        vd_ref[h] = v
        kb, v1 = k.astype(BF), _with_ones(v.astype(BF))
        q = z_ref[:, 3072 + h * hd:3072 + (h + 1) * hd] * c
        o1 = _attend(jnp.where(lane < D_HALF, q, 0.0).astype(BF), kb, v1)
        o2 = _attend(jnp.where(lane >= D_HALF, q, 0.0).astype(BF), kb, v1)
        od_ref[:, h * hd:(h + 1) * hd] = (_rms(o1 - lam * o2, gout_ref[...]) * (1.0 - lam_init)).astype(BF)


def _ctx_layer_kernel(lam_init, n_alias, x_ref, mod_ref, g_ref, w_ref, sink_ref, gkv_ref, wuk_ref, wuv_ref, gq_ref, gk_ref,
                      lam_ref, gout_ref, *refs):
    refs = refs[n_alias:]
    o_refs, st_refs, z_s = refs[:4], refs[4:4 + N_STATES], refs[4 + N_STATES]
    if not n_alias:
        for s in st_refs:
            s[:, 1:] = jnp.zeros((RPS, DEPTH - 1) + s.shape[2:], F32)
        st_refs = [s.at[:, 0] for s in st_refs]
    for r in range(RPS):
        rows = slice(r * SEQ, (r + 1) * SEQ)
        h = _rms(x_ref[rows, :], g_ref[...]) * (1.0 + mod_ref[1:2, :]) + mod_ref[0:1, :]
        z_s[rows, :] = _dot(h.astype(BF), w_ref[...])
    for r in range(RPS):
        rows = slice(r * SEQ, (r + 1) * SEQ)
        _ctx_mixer_body(lam_init, z_s.at[rows, :], sink_ref, gkv_ref, wuk_ref, wuv_ref, gq_ref, gk_ref, lam_ref, gout_ref,
                        [o.at[rows, :] for o in o_refs], [s.at[r] for s in st_refs])


def _const(shape):
    return pl.BlockSpec(shape, lambda *_: (0,) * len(shape))


def _ctx_layer(x, mod, l, g, w, p, lam_init, prev_states):
    m = x.shape[0]
    o_spec = pl.BlockSpec((RPS * SEQ, GROUP), lambda b: (b, 0))
    o_shape = jax.ShapeDtypeStruct((m, GROUP), BF)

    def st_dims(h):
        return (SEQ, B_RANK + B_ROPE) if h is None else (h, SEQ, HEAD_DIM)

    def st_spec(h):
        dims = st_dims(h)
        if l == 0:
            return pl.BlockSpec((RPS, DEPTH) + dims, lambda b: (b, 0) + (0,) * len(dims))
        return pl.BlockSpec((RPS, None) + dims, lambda b: (b, l) + (0,) * len(dims))

    in_specs = [
        pl.BlockSpec((RPS * SEQ, D_MODEL), lambda b: (b, 0)),
        pl.BlockSpec((None, None, MOD_CHUNKS, D_MODEL), lambda b: (l, 0, 0, 0)),
        _const((1, D_MODEL)),
        pl.BlockSpec((None, D_MODEL, Z_WIDTH), lambda b: (l, 0, 0), pipeline_mode=pl.Buffered(1)),
        pl.BlockSpec(memory_space=pltpu.SMEM),
        _const((1, B_RANK)), _const((B_RANK, GROUP)), _const((B_RANK, GROUP)),
        _const((1, HEAD_DIM)), _const((1, HEAD_DIM)), _const((4, D_HALF)), _const((1, HEAD_DIM)),
    ]
    n_alias = len(prev_states)
    aliases = {len(in_specs) + k: 4 + k for k in range(n_alias)}
    return pl.pallas_call(
        functools.partial(_ctx_layer_kernel, lam_init, n_alias),
        grid=(BATCH // RPS,),
        in_specs=in_specs + [pl.BlockSpec(memory_space=pl.ANY)] * n_alias,
        out_specs=[o_spec] * 4 + [st_spec(h) for h in STATE_HEADS],
        out_shape=[o_shape] * 4 + [jax.ShapeDtypeStruct((BATCH, DEPTH) + st_dims(h), F32) for h in STATE_HEADS],
        scratch_shapes=[pltpu.VMEM((RPS * SEQ, Z_WIDTH), F32)],
        input_output_aliases=aliases,
        compiler_params=_params(("arbitrary",), CTX_VMEM_LIMIT if l == 0 else VMEM_LIMIT),
        name="ctx_layer",
    )(x, mod, g, w, p['a_sink'], p['b_g_kv'], p['w_uk'], p['w_uv'], p['c_gq'], p['c_gk'], p['d_lam'], p['d_g_out'],
      *prev_states)


QBLK = 256
N_QBLK = DEC_SEQ // QBLK
N_WBLK = DEC_SEQ // WINDOW


def _lat_a_kernel(z_ref, ck_ref, cv_ref, cos_ref, sin_ref, sink_ref, o_ref, q_s, k_s, v_s):
    hd = HEAD_DIM
    cos, sin = cos_ref[...], sin_ref[...]
    c = HEAD_DIM ** -0.5 * LOG2E
    for hk in range(2):
        k_s[hk, 0:WINDOW, :] = jnp.zeros((WINDOW, hd), BF)
        k_s[hk, WINDOW + DEC_SEQ:, :] = jnp.zeros((WINDOW, hd), BF)
        v_s[hk, 0:WINDOW, :] = jnp.zeros((WINDOW, 2 * hd), BF)
        v_s[hk, WINDOW + DEC_SEQ:, :] = jnp.zeros((WINDOW, 2 * hd), BF)
        k = _rope(z_ref[:, 512 + hk * hd:512 + (hk + 1) * hd], cos, sin, 32)
        k_s[hk, WINDOW:WINDOW + DEC_SEQ, :] = k.astype(BF)
        v_s[hk, WINDOW:WINDOW + DEC_SEQ, :] = _with_ones(z_ref[:, 768 + hk * hd:768 + (hk + 1) * hd].astype(BF))
        for g in range(2):
            h = hk * 2 + g
            q = (_rope(z_ref[:, h * hd:(h + 1) * hd], cos, sin, 32) * c).astype(BF)
            for i in range(N_WBLK):
                q_s[hk, i, g * WINDOW:(g + 1) * WINDOW, :] = q[i * WINDOW:(i + 1) * WINDOW, :]

    r_io = lax.broadcasted_iota(jnp.int32, (2 * WINDOW, 3 * WINDOW), 0) & (WINDOW - 1)
    c_io = lax.broadcasted_iota(jnp.int32, (2 * WINDOW, 3 * WINDOW), 1)
    second = lax.broadcasted_iota(jnp.int32, (2 * WINDOW, 1), 0) >= WINDOW
    for hk in range(2):
        kc = ck_ref[hk].astype(BF)
        vc1 = _with_ones(cv_ref[hk].astype(BF))
        sink2 = jnp.where(second, sink_ref[2 * hk + 1] * LOG2E, sink_ref[2 * hk] * LOG2E)
        for i in range(N_WBLK):
            r0 = i * WINDOW
            q = q_s[hk, i]
            kl = k_s[hk, r0:r0 + 3 * WINDOW, :]
            vl1 = v_s[hk, r0:r0 + 3 * WINDOW, :]
            kpos = (i - 1) * WINDOW + c_io
            qpos = i * WINDOW + r_io
            ok = (jnp.abs(qpos - kpos) <= WINDOW) & (kpos >= 0) & (kpos < DEC_SEQ)
            sc = _dot_nt(q, kc)
            sl = _dot_nt(q, kl)
            m = jnp.maximum(jnp.maximum(jnp.max(sc, axis=-1, keepdims=True),
                                        jnp.max(jnp.where(ok, sl, NEG), axis=-1, keepdims=True)), sink2)
            pw = jnp.where(ok, jnp.exp2(sl - m), 0.0)
            o = _dot(jnp.exp2(sc - m).astype(BF), vc1) + _dot(pw.astype(BF), vl1)
            o = (o[:, :hd] / (o[:, hd:] + jnp.exp2(sink2 - m))).astype(BF)
            o_ref[r0:r0 + WINDOW, 2 * hk * hd:(2 * hk + 1) * hd] = o[0:WINDOW, :]
            o_ref[r0:r0 + WINDOW, (2 * hk + 1) * hd:(2 * hk + 2) * hd] = o[WINDOW:, :]


def _lat_b_kernel(z_ref, zr_ref, mla_ref, gkv_ref, wuk_ref, wuv_ref, cos_ref, sin_ref, o_ref, c_s, kr_s, k_s, v_s, q_s):
    hd = HEAD_DIM
    cos, sin = cos_ref[...], sin_ref[...]
    c = (HEAD_DIM + B_ROPE) ** -0.5 * LOG2E
    c_s[0:PAST_LEN, :] = mla_ref[:, 0:B_RANK].astype(BF)
    c_s[PAST_LEN:, :] = _rms(z_ref[:, 768:1024], gkv_ref[...]).astype(BF)
    kr_s[0:PAST_LEN, B_ROPE:] = jnp.zeros((PAST_LEN, hd - B_ROPE), BF)
    kr_s[0:PAST_LEN, 0:B_ROPE] = mla_ref[:, B_RANK:B_RANK + B_ROPE].astype(BF)
    kr_s[PAST_LEN:, :] = _rope(zr_ref[...], cos, sin, 16).astype(BF)
    lane = lax.broadcasted_iota(jnp.int32, (DEC_SEQ, hd), 1)
    for h in range(4):
        k_s[h, :, 0:hd] = _dot(c_s[...], wuk_ref[:, h * hd:(h + 1) * hd]).astype(BF)
        k_s[h, :, hd:2 * hd] = kr_s[...]
        v_s[h] = _with_ones(_dot(c_s[...], wuv_ref[:, h * hd:(h + 1) * hd]).astype(BF))
        q_s[h, :, 0:hd] = (z_ref[:, h * hd:(h + 1) * hd] * c).astype(BF)
        qr = _rope(z_ref[:, 512 + (h // 2) * hd:512 + (h // 2 + 1) * hd], cos, sin, 16) * c
        if h % 2:
            qr = pltpu.roll(qr, B_ROPE, 1)
        q_s[h, :, hd:2 * hd] = jnp.where(lane < B_ROPE, qr, 0.0).astype(BF)
    for h in range(4):
        for i in range(N_QBLK):
            r0 = i * QBLK
            o = _attend(q_s[h, r0:r0 + QBLK, :], k_s[h], v_s[h])
            o_ref[r0:r0 + QBLK, h * hd:(h + 1) * hd] = o.astype(BF)


def _lat_c_kernel(z_ref, ck_ref, cv_ref, gq_ref, gk_ref, cos_ref, sin_ref, o_ref, k_s, v_s, q_s):
    hd = HEAD_DIM
    cos, sin = cos_ref[...], sin_ref[...]
    c = HEAD_DIM ** -0.5 * LOG2E
    for hk in range(2):
        k_s[hk, 0:PAST_LEN, :] = ck_ref[hk].astype(BF)
        k = _rms(z_ref[:, 512 + hk * hd:512 + (hk + 1) * hd], gk_ref[...])
        k_s[hk, PAST_LEN:, :] = _rope(k, cos, sin, 32).astype(BF)
        v_s[hk, 0:PAST_LEN, :] = _with_ones(cv_ref[hk].astype(BF))
        v_s[hk, PAST_LEN:, :] = _with_ones(z_ref[:, 768 + hk * hd:768 + (hk + 1) * hd].astype(BF))
    for h in range(4):
        q = _rms(z_ref[:, h * hd:(h + 1) * hd], gq_ref[...])
        q_s[h] = (_rope(q, cos, sin, 32) * c).astype(BF)
    for h in range(4):
        for i in range(N_QBLK):
            r0 = i * QBLK
            o = _attend(q_s[h, r0:r0 + QBLK, :], k_s[h // 2], v_s[h // 2])
            o_ref[r0:r0 + QBLK, h * hd:(h + 1) * hd] = o.astype(BF)


def _lat_d_kernel(lam_init, z_ref, dk_ref, dv_ref, lam_ref, gout_ref, cos_ref, sin_ref, o_ref, k_s, v_s, q1_s, q2_s):
    hd = HEAD_DIM
    cos, sin = cos_ref[...], sin_ref[...]
    c = D_HALF ** -0.5 * LOG2E
    lam = _lam(lam_ref, lam_init)
    lane = lax.broadcasted_iota(jnp.int32, (DEC_SEQ, hd), 1)
    for h in range(4):
        k_s[h, 0:PAST_LEN, :] = dk_ref[h].astype(BF)
        k_s[h, PAST_LEN:, :] = _rope(z_ref[:, 512 + h * hd:512 + (h + 1) * hd], cos, sin, 16).astype(BF)
        v_s[h, 0:PAST_LEN, :] = _with_ones(dv_ref[h].astype(BF))
        v_s[h, PAST_LEN:, :] = _with_ones(z_ref[:, 1024 + h * hd:1024 + (h + 1) * hd].astype(BF))
        q = _rope(z_ref[:, h * hd:(h + 1) * hd], cos, sin, 16) * c
        q1_s[h] = jnp.where(lane < D_HALF, q, 0.0).astype(BF)
        q2_s[h] = jnp.where(lane >= D_HALF, q, 0.0).astype(BF)
    for h in range(4):
        for i in range(N_QBLK):
            r0 = i * QBLK
            o1 = _attend(q1_s[h, r0:r0 + QBLK, :], k_s[h], v_s[h])
            o2 = _attend(q2_s[h, r0:r0 + QBLK, :], k_s[h], v_s[h])
            o = _rms(o1 - lam * o2, gout_ref[...]) * (1.0 - lam_init)
            o_ref[r0:r0 + QBLK, h * hd:(h + 1) * hd] = o.astype(BF)


def _lat_mixers(z, caches, l, p, tabs, lam_init, casts):
    m = z.shape[0]
    hd = HEAD_DIM
    cos128, sin128, cos64, sin64 = tabs
    o_spec = pl.BlockSpec((DEC_SEQ, GROUP), lambda b: (b, 0))
    o_shape = jax.ShapeDtypeStruct((m, GROUP), BF)
    tab = _const((DEC_SEQ, hd))

    def cache_spec(h):
        return pl.BlockSpec((None, None, h, PAST_LEN, hd), lambda b: (b, l, 0, 0, 0))

    def zspec(width, col0):
        return pl.BlockSpec((DEC_SEQ, width), lambda b: (b, col0 // width))

    def scratch(*shape):
        return pltpu.VMEM(shape, BF)

    def cast_specs(w, layer, axis):
        r, c = w.shape[1:]
        if axis == 'rows':
            blk = (r // DEC_BATCH, c)
            return pl.BlockSpec((None,) + blk, lambda b: (layer, b, 0)), pl.BlockSpec(blk, lambda b: (b, 0))
        blk = (r, c // DEC_BATCH)
        return pl.BlockSpec((None,) + blk, lambda b: (layer, 0, b)), pl.BlockSpec(blk, lambda b: (0, b))

    def call(kernel, name, in_specs, args, scratch_shapes):
        todo = casts.get(name, [])
        specs = [cast_specs(*t) for t in todo]
        if todo:
            kernel = _also_cast(kernel, len(in_specs), len(todo))
        outs = pl.pallas_call(
            kernel, grid=(DEC_BATCH,),
            in_specs=in_specs + [s[0] for s in specs],
            out_specs=[o_spec] + [s[1] for s in specs],
            out_shape=[o_shape] + [jax.ShapeDtypeStruct(t[0].shape[1:], BF) for t in todo],
            scratch_shapes=scratch_shapes, compiler_params=_params(("arbitrary",)), name=name,
        )(*args, *[t[0] for t in todo])
        return outs[0], list(outs[1:])

    o_a, c_a = call(_lat_a_kernel, "lat_mixer_a",
                    [zspec(1024, 0), cache_spec(2), cache_spec(2), tab, tab, pl.BlockSpec(memory_space=pltpu.SMEM)],
                    (z, caches['a_k'], caches['a_v'], cos128, sin128, p['a_sink']),
                    [scratch(2, N_WBLK, 2 * WINDOW, hd), scratch(2, DEC_SEQ + 2 * WINDOW, hd),
                     scratch(2, DEC_SEQ + 2 * WINDOW, 2 * hd)])

    o_b, c_b = call(_lat_b_kernel, "lat_mixer_b",
                    [zspec(1024, 1024), zspec(LANES, Z_MAIN),
                     pl.BlockSpec((None, None, PAST_LEN, B_RANK + B_ROPE), lambda b: (b, l, 0, 0)),
                     _const((1, B_RANK)), _const((B_RANK, GROUP)), _const((B_RANK, GROUP)), tab, tab],
                    (z, z, caches['mla'], p['b_g_kv'], p['w_uk'], p['w_uv'], cos64, sin64),
                    [scratch(KEYS, B_RANK), scratch(KEYS, hd), scratch(4, KEYS, 2 * hd), scratch(4, KEYS, 2 * hd),
                     scratch(4, DEC_SEQ, 2 * hd)])

    o_c, c_c = call(_lat_c_kernel, "lat_mixer_c",
                    [zspec(1024, 2048), cache_spec(2), cache_spec(2), _const((1, hd)), _const((1, hd)), tab, tab],
                    (z, caches['c_k'], caches['c_v'], p['c_gq'], p['c_gk'], cos128, sin128),
                    [scratch(2, KEYS, hd), scratch(2, KEYS, 2 * hd), scratch(4, DEC_SEQ, hd)])

    o_d, c_d = call(functools.partial(_lat_d_kernel, lam_init), "lat_mixer_d",
                    [zspec(1536, 3072), cache_spec(4), cache_spec(4), _const((4, D_HALF)), _const((1, hd)), tab, tab],
                    (z, caches['d_k'], caches['d_v'], p['d_lam'], p['d_g_out'], cos64, sin64),
                    [scratch(4, KEYS, hd), scratch(4, KEYS, 2 * hd), scratch(4, DEC_SEQ, hd), scratch(4, DEC_SEQ, hd)])

    return (o_a, o_b, o_c, o_d), {"lat_mixer_a": c_a, "lat_mixer_b": c_b, "lat_mixer_c": c_c, "lat_mixer_d": c_d}


def _rope_tables():
    t = jnp.arange(DEC_SEQ)
    row = (t // GRID_W).astype(F32)[:, None]
    col = (t % GRID_W).astype(F32)[:, None]

    def table(half):
        inv = ROPE_BASE ** (-jnp.arange(half, dtype=F32) / half)
        ar, ac = row * inv[None, :], col * inv[None, :]
        cos = jnp.concatenate([jnp.cos(ar), jnp.cos(ar), jnp.cos(ac), jnp.cos(ac)], axis=-1)
        sin = jnp.concatenate([-jnp.sin(ar), jnp.sin(ar), -jnp.sin(ac), jnp.sin(ac)], axis=-1)
        return cos, sin

    cos128, sin128 = table(32)
    cos64, sin64 = table(16)
    return cos128, sin128, jnp.tile(cos64, (1, 2)), jnp.tile(sin64, (1, 2))


def kernel(x_prompt, x_sample, cache_a_k, cache_a_v, cache_mla, cache_c_k, cache_c_v, cache_d_k, cache_d_v,
           c, c_ctx, w_ada, b_ada, g_attn_pre, g_attn_post, g_mlp_pre, g_mlp_post, w_in, a_sink,
           b_g_kv, b_w_uk, b_w_uv, c_gq, c_gk, d_lam, d_g_out, w_out, w_up, w_down):
    w_in_p = _regroup_w_in(
        jnp.zeros((DEPTH, D_MODEL, Z_WIDTH), BF).at[:, :, :W_IN_COLS].set(w_in.astype(BF)))
    w_uk_b = b_w_uk.reshape(DEPTH, B_RANK, GROUP).astype(BF)
    w_uv_b = b_w_uv.reshape(DEPTH, B_RANK, GROUP).astype(BF)
    assert DEPTH == 2
    first_layer_casts = {"lat_mixer_a": [(w_out, 1, 'rows'), (w_down, 0, 'rows')],
                         "lat_mixer_b": [(w_up, 0, 'cols'), (w_out, 0, 'rows')],
                         "lat_mixer_c": [(w_down, 1, 'rows')],
                         "lat_mixer_d": [(w_up, 1, 'cols')]}

    cvec = jnp.concatenate([c_ctx[None, :], c, jnp.zeros((MOD_ROWS - 1 - DEC_BATCH, D_MODEL), F32)], axis=0)
    mod = _modulation(cvec, w_ada, b_ada).reshape(DEPTH, MOD_ROWS, MOD_CHUNKS, D_MODEL)
    tabs = _rope_tables()

    tm = 512
    tm_mlp, tf_mlp = 1024, 512
    ctx_row = lambda tm: (lambda i: 0)
    lat_row = lambda tm: (lambda i: 1 + (i * tm) // DEC_SEQ)
    caches = {'a_k': cache_a_k, 'a_v': cache_a_v, 'mla': cache_mla, 'c_k': cache_c_k, 'c_v': cache_c_v,
              'd_k': cache_d_k, 'd_v': cache_d_v}

    xc = x_prompt.reshape(BATCH * SEQ, D_MODEL)
    xl = x_sample.reshape(DEC_BATCH * DEC_SEQ, D_MODEL)
    states = ()
    w_out_b, w_down_b, w_up_b = {}, {}, {}
    for l in range(DEPTH):
        lam_init = 0.8 - 0.6 * math.exp(-0.3 * l)
        p = {'a_sink': a_sink[l], 'b_g_kv': b_g_kv[l][None], 'w_uk': w_uk_b[l], 'w_uv': w_uv_b[l],
             'c_gq': c_gq[l][None], 'c_gk': c_gk[l][None], 'd_lam': d_lam[l], 'd_g_out': d_g_out[l][None]}
        g_pre, g_post = g_attn_pre[l][None], g_attn_post[l][None]
        g_mpre, g_mpost = g_mlp_pre[l][None], g_mlp_post[l][None]

        zl = _in_projection(xl, mod, lat_row(tm), l, g_pre, w_in_p, tm)
        o_parts, cast = _lat_mixers(zl, caches, l, p, tabs, lam_init, first_layer_casts if l == 0 else {})
        if l == 0:
            (w_out_b[1], w_down_b[0]), (w_up_b[0], w_out_b[0]) = cast["lat_mixer_a"], cast["lat_mixer_b"]
            (w_down_b[1],), (w_up_b[1],) = cast["lat_mixer_c"], cast["lat_mixer_d"]

        outs = _ctx_layer(xc, mod, l, g_pre, w_in_p, p, lam_init, states)
        states = tuple(outs[4:])
        xc, hc = _out_projection(outs[:4], w_out_b[l], xc, mod, ctx_row(tm), l, g_post, g_mpre, tm)
        xc = _mlp(hc, xc, mod, ctx_row(tm_mlp), l, g_mpost, w_up_b[l], w_down_b[l], tm_mlp, tf_mlp)

        xl, hl = _out_projection(o_parts, w_out_b[l], xl, mod, lat_row(tm), l, g_post, g_mpre, tm)
        xl = _mlp(hl, xl, mod, lat_row(tm_mlp), l, g_mpost, w_up_b[l], w_down_b[l], tm_mlp, tf_mlp)

    y_prompt = xc.reshape(BATCH, SEQ, D_MODEL)
    y_sample = xl.reshape(DEC_BATCH, DEC_SEQ, D_MODEL)
    return (y_prompt, y_sample) + states
```
